```python
import jax
import jax.numpy as jnp
from jax import lax
import numpy as np

D_MODEL = 4096
BATCH = 1
SEQ = 16384
DEPTH = 4

GRID_W = 64
CTX_LEN = 256
EPS = 1e-6

MLA_HEADS = 16
MLA_Q_RANK = 1024
MLA_KV_RANK = 512
MLA_NOPE_DIM = 128
MLA_ROPE_DIM = 64
MLA_V_DIM = 128
MLA_QK_DIM = MLA_NOPE_DIM + MLA_ROPE_DIM
MLA_WIDTH = MLA_HEADS * MLA_V_DIM
ROPE_BASE = 10000.0
Q_BLOCK = 128

CONV_GROUPS = 16
CONV_WIDTH = D_MODEL // 2
CONV_TAPS = 3

AB_KV_COLS = MLA_KV_RANK + MLA_ROPE_DIM
AB_IN_DIM = AB_KV_COLS + MLA_Q_RANK + 3 * CONV_WIDTH
AB_OUT_IN = MLA_WIDTH + CONV_WIDTH

GLA_HEADS = 8
GLA_KEY_DIM = D_MODEL // 2
GLA_VAL_DIM = D_MODEL
GLA_DK = GLA_KEY_DIM // GLA_HEADS
GLA_DV = GLA_VAL_DIM // GLA_HEADS
GLA_GATE_RANK = 16
GLA_GATE_TEMP = 16.0
GLA_CHUNK = 64
GLA_STATE_COLS = GLA_KEY_DIM + GLA_VAL_DIM + 2 * GLA_GATE_RANK
GLA_IN_DIM = GLA_STATE_COLS + GLA_KEY_DIM + GLA_VAL_DIM

N_EXPERTS = 16
N_GROUPS = 4
EXPERTS_PER_GROUP = N_EXPERTS // N_GROUPS
GROUP_SCORE_K = 2
TOP_K = 2
D_FF_EXPERT = 256

N_A_LAYERS = (DEPTH + 1) // 2
N_C_LAYERS = DEPTH // 2

kernel_name = 'hybrid_mla_conv_gla_grouped_moe_dit_trunk'


def rms_norm(x, gain):
    x32 = x.astype(jnp.float32)
    y = x32 * lax.rsqrt(jnp.mean(x32 * x32, axis=-1, keepdims=True) + EPS)
    return (y * gain.astype(jnp.float32)).astype(x.dtype)


def modulate(x, gain, shift, scale):
    return rms_norm(x, gain) * (1 + scale) + shift


def rope_1d(x, pos):
    half = x.shape[-1] // 2
    inv_freq = ROPE_BASE ** (-jnp.arange(half, dtype=jnp.float32) / half)
    ang = pos.astype(jnp.float32)[:, None] * inv_freq[None, :]
    cos = jnp.cos(ang)[None, :, None, :]
    sin = jnp.sin(ang)[None, :, None, :]
    x32 = x.astype(jnp.float32)
    x1, x2 = x32[..., :half], x32[..., half:]
    return jnp.concatenate([x1 * cos - x2 * sin, x1 * sin + x2 * cos], axis=-1).astype(x.dtype)


def axial_rope_2d(x, rows, cols):
    half = x.shape[-1] // 2
    return jnp.concatenate([rope_1d(x[..., :half], rows), rope_1d(x[..., half:], cols)], axis=-1)


def block_attention(q, k, v):
    b, tq, h, d = q.shape
    nb = tq // Q_BLOCK
    scale = d ** -0.5
    qb = q.reshape(b, nb, Q_BLOCK, h, d).transpose(1, 0, 2, 3, 4)

    def one_block(q_blk):
        s = jnp.einsum('bqhd,bkhd->bhqk', q_blk, k).astype(jnp.float32) * scale
        p = jax.nn.softmax(s, axis=-1).astype(v.dtype)
        return jnp.einsum('bhqk,bkhd->bqhd', p, v)

    o = lax.map(one_block, qb)
    return o.transpose(1, 0, 2, 3, 4).reshape(b, tq, h, v.shape[-1])


def mla_queries(cq, q_a_gain, w_uq, q_gain, pos):
    b, t, _ = cq.shape
    q = (rms_norm(cq, q_a_gain) @ w_uq).reshape(b, t, MLA_HEADS, MLA_QK_DIM)
    q_nope = rms_norm(q[..., :MLA_NOPE_DIM], q_gain[:MLA_NOPE_DIM])
    q_rope = rms_norm(q[..., MLA_NOPE_DIM:], q_gain[MLA_NOPE_DIM:])
    if pos is not None:
        q_rope = axial_rope_2d(q_rope, pos[0], pos[1])
    return jnp.concatenate([q_nope, q_rope], axis=-1)


def mla_keys_values(ckv, k_rope, kv_a_gain, w_ukv, k_gain, pos):
    b, t, _ = ckv.shape
    kv = (rms_norm(ckv, kv_a_gain) @ w_ukv).reshape(b, t, MLA_HEADS, MLA_NOPE_DIM + MLA_V_DIM)
    k_nope = rms_norm(kv[..., :MLA_NOPE_DIM], k_gain[:MLA_NOPE_DIM])
    v = kv[..., MLA_NOPE_DIM:]
    k_rope = rms_norm(k_rope[:, :, None, :], k_gain[MLA_NOPE_DIM:])
    if pos is not None:
        k_rope = axial_rope_2d(k_rope, pos[0], pos[1])
    k = jnp.concatenate([k_nope, jnp.broadcast_to(k_rope, (b, t, MLA_HEADS, MLA_ROPE_DIM))], axis=-1)
    return k, v


def short_conv3(u, w, bias):
    up = jnp.pad(u, ((0, 0), (1, 1), (0, 0)))
    return up[:, :-2] * w[0] + up[:, 1:-1] * w[1] + up[:, 2:] * w[2] + bias


def mixer_mla_conv(hx, hc, rows, cols, w_in, q_a_gain, w_uq, q_gain, kv_a_gain, w_ukv, k_gain,
                   conv_w, conv_b, w_out, need_ctx):
    px = hx @ w_in
    pc = hc @ (w_in if need_ctx else w_in[:, :AB_KV_COLS])
    k_c, v_c = mla_keys_values(pc[..., :MLA_KV_RANK], pc[..., MLA_KV_RANK:AB_KV_COLS],
                               kv_a_gain, w_ukv, k_gain, None)
    k_x, v_x = mla_keys_values(px[..., :MLA_KV_RANK], px[..., MLA_KV_RANK:AB_KV_COLS],
                               kv_a_gain, w_ukv, k_gain, (rows, cols))
    k_all = jnp.concatenate([k_c, k_x], axis=1)
    v_all = jnp.concatenate([v_c, v_x], axis=1)

    def branch_out(p, k, v, pos):
        o1 = AB_KV_COLS + MLA_Q_RANK
        cq = p[..., AB_KV_COLS:o1]
        gb = p[..., o1:o1 + CONV_WIDTH]
        gc = p[..., o1 + CONV_WIDTH:o1 + 2 * CONV_WIDTH]
        u = p[..., o1 + 2 * CONV_WIDTH:]
        q = mla_queries(cq, q_a_gain, w_uq, q_gain, pos)
        o = block_attention(q, k, v)
        o = o.reshape(o.shape[0], o.shape[1], MLA_WIDTH)
        y = gb * short_conv3(gc * u, conv_w, conv_b)
        return jnp.concatenate([o, y], axis=-1) @ w_out

    out_x = branch_out(px, k_all, v_all, (rows, cols))
    out_c = branch_out(pc, k_c, v_c, None) if need_ctx else None
    return out_x, out_c


def flip_t(z):
    return jnp.flip(z, axis=1)


def gla_log_decay(a, w_a2, b_a2):
    b, t, _ = a.shape
    z = (a @ w_a2 + b_a2).astype(jnp.float32)
    return (jax.nn.log_sigmoid(z) / GLA_GATE_TEMP).reshape(b, t, GLA_HEADS, GLA_DK)


def gla_final_state(k, v, log_a):
    bc = jnp.cumsum(log_a, axis=1)
    kw = k * jnp.exp(bc[:, -1:] - bc)
    return jnp.einsum('bthk,bthv->bhkv', kw, v)


def gla_chunked(q, k, v, log_a, s0):
    bsz, t, h, dk = q.shape
    dv = v.shape[-1]
    n = t // GLA_CHUNK

    def to_chunks(z):
        return z.reshape(bsz, n, GLA_CHUNK, h, z.shape[-1]).transpose(1, 0, 2, 3, 4)

    lower = jnp.tril(jnp.ones((GLA_CHUNK, GLA_CHUNK), dtype=bool))[None, :, :, None, None]

    def step(s, inp):
        qc, kc, vc, lac = inp
        bc = jnp.cumsum(lac, axis=1)
        inter = jnp.einsum('blhk,bhkv->blhv', qc * jnp.exp(bc), s)
        diff = bc[:, :, None] - bc[:, None, :]
        decay = jnp.exp(jnp.where(lower, diff, -jnp.inf))
        att = jnp.einsum('bthk,bshk,btshk->bhts', qc, kc, decay)
        intra = jnp.einsum('bhts,bshv->bthv', att, vc)
        b_last = bc[:, -1]
        s_new = jnp.exp(b_last)[..., None] * s + jnp.einsum(
            'blhk,blhv->bhkv', kc * jnp.exp(b_last[:, None] - bc), vc)
        return s_new, inter + intra

    _, o = lax.scan(step, s0, (to_chunks(q), to_chunks(k), to_chunks(v), to_chunks(log_a)))
    return o.transpose(1, 0, 2, 3, 4).reshape(bsz, t, h, dv)


def bidir_gla(q, k, v, la_f, la_b, s_f, s_b):
    o_f = gla_chunked(q, k, v, la_f, s_f)
    o_b = flip_t(gla_chunked(flip_t(q), flip_t(k), flip_t(v), flip_t(la_b), s_b))
    return o_f + o_b


def mixer_gla(hx, hc, w_in, w_af, b_af, w_ab, b_ab, o_gain, w_out, need_ctx):
    px = hx @ w_in
    pc = hc @ (w_in if need_ctx else w_in[:, :GLA_STATE_COLS])
    o1 = GLA_KEY_DIM
    o2 = o1 + GLA_VAL_DIM
    o3 = o2 + GLA_GATE_RANK

    def heads(z, d):
        return z.reshape(z.shape[0], z.shape[1], GLA_HEADS, d).astype(jnp.float32)

    def state_inputs(p):
        k = heads(p[..., :o1], GLA_DK)
        v = heads(p[..., o1:o2], GLA_DV)
        la_f = gla_log_decay(p[..., o2:o3], w_af, b_af)
        la_b = gla_log_decay(p[..., o3:GLA_STATE_COLS], w_ab, b_ab)
        return k, v, la_f, la_b

    def query_gate(p):
        q = heads(p[..., GLA_STATE_COLS:GLA_STATE_COLS + GLA_KEY_DIM], GLA_DK) * (GLA_DK ** -0.5)
        g = p[..., GLA_STATE_COLS + GLA_KEY_DIM:]
        return q, g

    def read_out(o, g):
        o = rms_norm(o, o_gain).reshape(o.shape[0], o.shape[1], GLA_VAL_DIM).astype(g.dtype)
        return (o * jax.nn.silu(g)) @ w_out

    k_c, v_c, laf_c, lab_c = state_inputs(pc)
    s_f = gla_final_state(k_c, v_c, laf_c)
    s_b = gla_final_state(flip_t(k_c), flip_t(v_c), flip_t(lab_c))
    k_x, v_x, laf_x, lab_x = state_inputs(px)
    q_x, g_x = query_gate(px)
    out_x = read_out(bidir_gla(q_x, k_x, v_x, laf_x, lab_x, s_f, s_b), g_x)
    out_c = None
    if need_ctx:
        q_c, g_c = query_gate(pc)
        zero = jnp.zeros_like(s_f)
        out_c = read_out(bidir_gla(q_c, k_c, v_c, laf_c, lab_c, zero, zero), g_c)
    return out_x, out_c


def moe_ffn(h, w_router, b_router, w_gate, w_up, w_down):
    b, t, _ = h.shape
    s = jax.nn.sigmoid(jnp.einsum('btd,de->bte', h, w_router).astype(jnp.float32))
    s_sel = s + b_router.astype(jnp.float32)
    grp_score = lax.top_k(s_sel.reshape(b, t, N_GROUPS, EXPERTS_PER_GROUP), GROUP_SCORE_K)[0].sum(-1)
    grp_idx = jnp.argmax(grp_score, axis=-1)
    in_group = (jnp.arange(N_EXPERTS) // EXPERTS_PER_GROUP)[None, None, :] == grp_idx[..., None]
    _, e_idx = lax.top_k(jnp.where(in_group, s_sel, -jnp.inf), TOP_K)
    w_sel = jnp.take_along_axis(s, e_idx, axis=-1)
    w_sel = w_sel / jnp.sum(w_sel, axis=-1, keepdims=True)
    gates = jnp.sum(jax.nn.one_hot(e_idx, N_EXPERTS, dtype=jnp.float32) * w_sel[..., None], axis=-2)
    a = jnp.einsum('btd,edf->btef', h, w_gate)
    u = jnp.einsum('btd,edf->btef', h, w_up)
    hid = jax.nn.silu(a) * u * gates[..., None].astype(h.dtype)
    return jnp.einsum('btef,efd->btd', hid, w_down)


def setup_inputs(seed: int = 0) -> dict:
    key = jax.random.key(seed)
    ks = iter(jax.random.split(key, 40))
    f32 = jnp.float32

    def normal(shape, std):
        return jax.random.normal(next(ks), shape, f32) * std

    def gain(shape):
        return 1.0 + normal(shape, 0.1)

    na, nc = N_A_LAYERS, N_C_LAYERS
    return {
        'x': normal((BATCH, SEQ, D_MODEL), 1.0),
        'c': normal((BATCH, D_MODEL), 1.0),
        'ctx': normal((BATCH, CTX_LEN, D_MODEL), 1.0),
        'c_ctx': normal((D_MODEL,), 1.0),
        'ada_w': normal((DEPTH, D_MODEL, 6 * D_MODEL), 0.3 * D_MODEL ** -0.5),
        'ada_b': normal((DEPTH, 6 * D_MODEL), 0.01),
        'norm_mix_gain': gain((DEPTH, D_MODEL)),
        'norm_ffn_gain': gain((DEPTH, D_MODEL)),
        'ab_w_in': normal((na, D_MODEL, AB_IN_DIM), D_MODEL ** -0.5),
        'ab_q_a_gain': gain((na, MLA_Q_RANK)),
        'ab_w_uq': normal((na, MLA_Q_RANK, MLA_HEADS * MLA_QK_DIM), MLA_Q_RANK ** -0.5),
        'ab_q_gain': gain((na, MLA_QK_DIM)),
        'ab_kv_a_gain': gain((na, MLA_KV_RANK)),
        'ab_w_ukv': normal((na, MLA_KV_RANK, MLA_HEADS * (MLA_NOPE_DIM + MLA_V_DIM)), MLA_KV_RANK ** -0.5),
        'ab_k_gain': gain((na, MLA_QK_DIM)),
        'ab_conv_w': normal((na, CONV_TAPS, CONV_WIDTH), CONV_TAPS ** -0.5),
        'ab_conv_b': normal((na, CONV_WIDTH), 0.01),
        'ab_w_out': normal((na, AB_OUT_IN, D_MODEL), AB_OUT_IN ** -0.5),
        'gla_w_in': normal((nc, D_MODEL, GLA_IN_DIM), D_MODEL ** -0.5),
        'gla_w_af': normal((nc, GLA_GATE_RANK, GLA_KEY_DIM), GLA_GATE_RANK ** -0.5),
        'gla_b_af': normal((nc, GLA_KEY_DIM), 0.1),
        'gla_w_ab': normal((nc, GLA_GATE_RANK, GLA_KEY_DIM), GLA_GATE_RANK ** -0.5),
        'gla_b_ab': normal((nc, GLA_KEY_DIM), 0.1),
        'gla_o_gain': gain((nc, GLA_DV)),
        'gla_w_out': normal((nc, GLA_VAL_DIM, D_MODEL), GLA_VAL_DIM ** -0.5),
        'router_w': normal((D_MODEL, N_EXPERTS), D_MODEL ** -0.5),
        'router_b': normal((N_EXPERTS,), 0.01),
        'moe_w_gate': normal((DEPTH, N_EXPERTS, D_MODEL, D_FF_EXPERT), D_MODEL ** -0.5),
        'moe_w_up': normal((DEPTH, N_EXPERTS, D_MODEL, D_FF_EXPERT), D_MODEL ** -0.5),
        'moe_w_down': normal((DEPTH, N_EXPERTS, D_FF_EXPERT, D_MODEL), D_FF_EXPERT ** -0.5),
    }


def reference(x, c, ctx, c_ctx, ada_w, ada_b, norm_mix_gain, norm_ffn_gain,
              ab_w_in, ab_q_a_gain, ab_w_uq, ab_q_gain, ab_kv_a_gain, ab_w_ukv, ab_k_gain,
              ab_conv_w, ab_conv_b, ab_w_out,
              gla_w_in, gla_w_af, gla_b_af, gla_w_ab, gla_b_ab, gla_o_gain, gla_w_out,
              router_w, router_b, moe_w_gate, moe_w_up, moe_w_down):
    b, s, d = x.shape
    n_rows = s // GRID_W
    rows = jnp.repeat(jnp.arange(n_rows, dtype=jnp.int32), GRID_W)
    cols = jnp.arange(n_rows * GRID_W, dtype=jnp.int32) % GRID_W
    cx = ctx
    cond_x = jax.nn.silu(c)
    cond_c = jax.nn.silu(c_ctx)
    for i in range(DEPTH):
        last = i == DEPTH - 1
        j = i // 2
        mod_x = (cond_x @ ada_w[i] + ada_b[i]).reshape(b, 6, 1, d)
        mod_c = (cond_c @ ada_w[i] + ada_b[i]).reshape(6, d)
        hx = modulate(x, norm_mix_gain[i], mod_x[:, 0], mod_x[:, 1])
        hc = modulate(cx, norm_mix_gain[i], mod_c[0], mod_c[1])
        if i % 2 == 0:
            yx, yc = mixer_mla_conv(hx, hc, rows, cols, ab_w_in[j], ab_q_a_gain[j], ab_w_uq[j], ab_q_gain[j],
                                    ab_kv_a_gain[j], ab_w_ukv[j], ab_k_gain[j], ab_conv_w[j], ab_conv_b[j],
                                    ab_w_out[j], not last)
        else:
            yx, yc = mixer_gla(hx, hc, gla_w_in[j], gla_w_af[j], gla_b_af[j], gla_w_ab[j], gla_b_ab[j],
                               gla_o_gain[j], gla_w_out[j], not last)
        x = x + mod_x[:, 2] * yx
        hx2 = modulate(x, norm_ffn_gain[i], mod_x[:, 3], mod_x[:, 4])
        x = x + mod_x[:, 5] * moe_ffn(hx2, router_w, router_b, moe_w_gate[i], moe_w_up[i], moe_w_down[i])
        if not last:
            cx = cx + mod_c[2] * yc
            hc2 = modulate(cx, norm_ffn_gain[i], mod_c[3], mod_c[4])
            cx = cx + mod_c[5] * moe_ffn(hc2, router_w, router_b, moe_w_gate[i], moe_w_up[i], moe_w_down[i])
    return x
```

```python
import functools

import numpy as np
import jax
import jax.numpy as jnp
from jax import lax
from jax.experimental import pallas as pl
from jax.experimental.pallas import tpu as pltpu

F32 = jnp.float32
BF16 = jnp.bfloat16
HIGHEST = lax.Precision.HIGHEST

LANE = 128
SUBLANE = 8
VMEM_LIMIT_BYTES = 56 * 1024 * 1024

GRID_W = 64
EPS = 1e-6
MLA_HEADS = 16
MLA_Q_RANK = 1024
MLA_KV_RANK = 512
MLA_NOPE = 128
MLA_ROPE = 64
MLA_V = 128
MLA_QK = MLA_NOPE + MLA_ROPE
MLA_HEAD_PAD = 2 * LANE
ROPE_BASE = 10000.0
CONV_WIDTH = 2048
GLA_HEADS = 8
GLA_DK = 256
GLA_DV = 512
GLA_KEY = GLA_HEADS * GLA_DK
GLA_VAL = GLA_HEADS * GLA_DV
GLA_RANK = 16
GLA_TEMP = 16.0
GLA_CHUNK = 64
N_EXPERTS = 16
N_GROUPS = 4
GROUP_SIZE = N_EXPERTS // N_GROUPS
D_FF = 256


def _cp(*sem):
    return pltpu.CompilerParams(dimension_semantics=sem, vmem_limit_bytes=VMEM_LIMIT_BYTES)


def _dot(a, b):
    return jnp.dot(a, b, preferred_element_type=F32)


def _dot_nt(a, b):
    return lax.dot_general(a, b, (((1,), (1,)), ((), ())), preferred_element_type=F32)


def _dot_tn(a, b):
    return lax.dot_general(a, b, (((0,), (0,)), ((), ())), preferred_element_type=F32)


def _dot_f32(a, b):
    return jnp.dot(a, b, precision=HIGHEST, preferred_element_type=F32)


def _rms(v, n=None):
    ss = jnp.sum(v * v, axis=-1, keepdims=True)
    return v * lax.rsqrt(ss * (1.0 / (n if n is not None else v.shape[-1])) + EPS)


def _adaln_kernel(c_ref, cc_ref, w_ref, b_ref, o_ref, acc_ref):
    k = pl.program_id(2)

    @pl.when(k == 0)
    def _():
        acc_ref[...] = jnp.zeros_like(acc_ref)

    w = w_ref[...]
    tk, tn = w.shape
    for r, ref in enumerate((c_ref, cc_ref)):
        cv = ref[...]
        cs = cv * jax.nn.sigmoid(cv)
        acc_ref[r] += jnp.sum((w * cs).reshape(tk // SUBLANE, SUBLANE, tn), axis=0)

    @pl.when(k == pl.num_programs(2) - 1)
    def _():
        o_ref[...] = jnp.sum(acc_ref[...], axis=1) + b_ref[...]


def _adaln(c_col, cc_col, ada_w, ada_b):
    depth, d, n = ada_w.shape
    tk, tn = 512, 2048
    return pl.pallas_call(
        _adaln_kernel,
        out_shape=jax.ShapeDtypeStruct((depth, 2, n), F32),
        grid=(depth, n // tn, d // tk),
        in_specs=[
            pl.BlockSpec((tk, 1), lambda l, j, k: (k, 0)),
            pl.BlockSpec((tk, 1), lambda l, j, k: (k, 0)),
            pl.BlockSpec((None, tk, tn), lambda l, j, k: (l, k, j)),
            pl.BlockSpec((None, 1, tn), lambda l, j, k: (l, 0, j)),
        ],
        out_specs=pl.BlockSpec((None, 2, tn), lambda l, j, k: (l, 0, j)),
        scratch_shapes=[pltpu.VMEM((2, SUBLANE, tn), F32)],
        compiler_params=_cp("parallel", "parallel", "arbitrary"),
        name="adaln_mod",
    )(c_col, cc_col, ada_w, ada_b.reshape(depth, 1, n))


def _route_gates(logits, bias_col):
    tm = logits.shape[0]
    s = jax.nn.sigmoid(logits.T[:N_EXPERTS])
    sel = s + bias_col
    row = [sel[e:e + 1] for e in range(N_EXPERTS)]
    score = []
    for g in range(N_GROUPS):
        a, b, c2, d2 = row[GROUP_SIZE * g:GROUP_SIZE * (g + 1)]
        hi1, lo1 = jnp.maximum(a, b), jnp.minimum(a, b)
        hi2, lo2 = jnp.maximum(c2, d2), jnp.minimum(c2, d2)
        score.append(jnp.maximum(hi1, hi2) + jnp.maximum(jnp.minimum(hi1, hi2), jnp.maximum(lo1, lo2)))
    best, gidx = score[0], jnp.zeros((1, tm), jnp.int32)
    for g in range(1, N_GROUPS):
        upd = score[g] > best
        gidx = jnp.where(upd, g, gidx)
        best = jnp.where(upd, score[g], best)
    neg = jnp.full((1, tm), -jnp.inf, F32)
    cand = [jnp.where(gidx == (e // GROUP_SIZE), row[e], neg) for e in range(N_EXPERTS)]

    def argmax_first(vals):
        bv, bi = vals[0], jnp.zeros((1, tm), jnp.int32)
        for e in range(1, N_EXPERTS):
            upd = vals[e] > bv
            bi = jnp.where(upd, e, bi)
            bv = jnp.where(upd, vals[e], bv)
        return bi

    i1 = argmax_first(cand)
    i2 = argmax_first([jnp.where(i1 == e, neg, cand[e]) for e in range(N_EXPERTS)])
    eidx = lax.broadcasted_iota(jnp.int32, (N_EXPERTS, tm), 0)
    w1 = jnp.sum(jnp.where(eidx == i1, s, 0.0), axis=0, keepdims=True)
    w2 = jnp.sum(jnp.where(eidx == i2, s, 0.0), axis=0, keepdims=True)
    den = w1 + w2
    efull = lax.broadcasted_iota(jnp.int32, (LANE, tm), 0)
    gates_t = jnp.where(efull == i1, w1 / den, 0.0) + jnp.where(efull == i2, w2 / den, 0.0)
    return gates_t.T


def _modulate_kernel(x_ref, g_ref, sh_ref, sc_ref, *rest, route):
    x = x_ref[...]
    h = _rms(x) * g_ref[...] * (1.0 + sc_ref[...]) + sh_ref[...]
    if route:
        wr_ref, br_ref, h_ref, gates_ref = rest
        logits = _dot_f32(h, wr_ref[...])
        gates_ref[...] = _route_gates(logits, br_ref[...])
    else:
        (h_ref,) = rest
    h_ref[...] = h.astype(h_ref.dtype)


def _modulate(x, gain, shift, scale, router=None):
    m, d = x.shape
    tm = min(256, m)
    vec = pl.BlockSpec((1, d), lambda i: (0, 0))
    in_specs = [pl.BlockSpec((tm, d), lambda i: (i, 0)), vec, vec, vec]
    args = [x, gain, shift, scale]
    out_shape = [jax.ShapeDtypeStruct((m, d), BF16)]
    out_specs = [pl.BlockSpec((tm, d), lambda i: (i, 0))]
    if router is not None:
        wr, br = router
        in_specs += [pl.BlockSpec((d, LANE), lambda i: (0, 0)), pl.BlockSpec((N_EXPERTS, 1), lambda i: (0, 0))]
        args += [wr, br]
        out_shape.append(jax.ShapeDtypeStruct((m, LANE), F32))
        out_specs.append(pl.BlockSpec((tm, LANE), lambda i: (i, 0)))
    out = pl.pallas_call(
        functools.partial(_modulate_kernel, route=router is not None),
        out_shape=out_shape, grid=(m // tm,), in_specs=in_specs, out_specs=out_specs,
        compiler_params=_cp("parallel"),
        name="modulate_route" if router is not None else "modulate",
    )(*args)
    return out if router is not None else out[0]


def _mm_kernel(*refs, n_pairs, residual):
    a_refs, w_refs = refs[:n_pairs], refs[n_pairs:2 * n_pairs]
    o_ref = refs[-1]
    acc = _dot(a_refs[0][...], w_refs[0][...])
    for a_ref, w_ref in zip(a_refs[1:], w_refs[1:]):
        acc += _dot(a_ref[...], w_ref[...])
    if residual:
        x_ref, g_ref = refs[2 * n_pairs:2 * n_pairs + 2]
        acc = x_ref[...] + g_ref[...] * acc
    o_ref[...] = acc.astype(o_ref.dtype)


def _mm(pairs, *, out_dtype, residual=None, tm=1024, tn=512, name="mm"):
    m = pairs[0][0].shape[0]
    n = pairs[0][1].shape[1]
    tm, tn = min(tm, m), min(tn, n)
    assert m % tm == 0 and n % tn == 0, (m, n, tm, tn)
    in_specs = [pl.BlockSpec((tm, a.shape[1]), lambda i, j: (i, 0)) for a, _ in pairs]
    in_specs += [pl.BlockSpec((w.shape[0], tn), lambda i, j: (0, j)) for _, w in pairs]
    args = [a for a, _ in pairs] + [w for _, w in pairs]
    if residual is not None:
        in_specs += [pl.BlockSpec((tm, tn), lambda i, j: (i, j)), pl.BlockSpec((1, tn), lambda i, j: (0, j))]
        args += list(residual)
    return pl.pallas_call(
        functools.partial(_mm_kernel, n_pairs=len(pairs), residual=residual is not None),
        out_shape=jax.ShapeDtypeStruct((m, n), out_dtype),
        grid=(m // tm, n // tn), in_specs=in_specs,
        out_specs=pl.BlockSpec((tm, tn), lambda i, j: (i, j)),
        compiler_params=_cp("parallel", "arbitrary"),
        name=name,
    )(*args)


def _rope_rotate(v, cos, sin_signed):
    lane = lax.broadcasted_iota(jnp.int32, (1, LANE), 1)
    first_half = (lane % 32) < 16
    partner = jnp.where(first_half, pltpu.roll(v, LANE - 16, axis=1), pltpu.roll(v, 16, axis=1))
    return v * cos + partner * sin_signed


def _q_up_kernel(cq_ref, ga_ref, w_ref, gn_ref, gr_ref, cos_ref, sin_ref, o_ref, *, scale):
    cqn = (_rms(cq_ref[...].astype(F32)) * ga_ref[...]).astype(BF16)
    q = _dot(cqn, w_ref[...])
    cos, sin = cos_ref[...], sin_ref[...]
    gn, gr = gn_ref[...] * scale, gr_ref[...] * scale
    for h in range(MLA_HEADS):
        o = h * MLA_HEAD_PAD
        o_ref[:, o:o + LANE] = (_rms(q[:, o:o + LANE]) * gn).astype(BF16)
        qr = _rms(q[:, o + LANE:o + 2 * LANE], MLA_ROPE) * gr
        o_ref[:, o + LANE:o + 2 * LANE] = _rope_rotate(qr, cos, sin).astype(BF16)


def _q_up(px, ga, w, gn, gr, cos, sin):
    m = px.shape[0]
    tm = min(512, m)
    n = MLA_HEADS * MLA_HEAD_PAD
    vec = lambda width: pl.BlockSpec((1, width), lambda i: (0, 0))
    return pl.pallas_call(
        functools.partial(_q_up_kernel, scale=MLA_QK ** -0.5),
        out_shape=jax.ShapeDtypeStruct((m, n), BF16),
        grid=(m // tm,),
        in_specs=[pl.BlockSpec((tm, MLA_Q_RANK), lambda i: (i, 0)), vec(MLA_Q_RANK),
                  pl.BlockSpec((MLA_Q_RANK, n), lambda i: (0, 0)), vec(LANE), vec(LANE),
                  pl.BlockSpec((tm, LANE), lambda i: (i, 0)), pl.BlockSpec((tm, LANE), lambda i: (i, 0))],
        out_specs=pl.BlockSpec((tm, n), lambda i: (i, 0)),
        compiler_params=_cp("parallel"),
        name="mla_q_up",
    )(px, ga, w, gn, gr, cos, sin)


def _kv_up_kernel(ckv_ref, kr_ref, ga_ref, w_ref, gn_ref, gr_ref, cos_ref, sin_ref, k_ref, v_ref):
    ckvn = (_rms(ckv_ref[...].astype(F32)) * ga_ref[...]).astype(BF16)
    kv = _dot(ckvn, w_ref[...])
    kr = _rms(kr_ref[...].astype(F32), MLA_ROPE) * gr_ref[...]
    kr = _rope_rotate(kr, cos_ref[...], sin_ref[...]).astype(BF16)
    gn = gn_ref[...]
    for h in range(MLA_HEADS):
        o = h * MLA_HEAD_PAD
        k_ref[:, o:o + LANE] = (_rms(kv[:, o:o + LANE]) * gn).astype(BF16)
        k_ref[:, o + LANE:o + 2 * LANE] = kr
        v_ref[:, h * MLA_V:(h + 1) * MLA_V] = kv[:, o + LANE:o + 2 * LANE].astype(BF16)


def _kv_up(px, ckv_blk, kr_blk, ga, w, gn, gr, cos, sin):
    m = px.shape[0]
    tm = min(512, m)
    vec = lambda width: pl.BlockSpec((1, width), lambda i: (0, 0))
    return pl.pallas_call(
        _kv_up_kernel,
        out_shape=[jax.ShapeDtypeStruct((m, MLA_HEADS * MLA_HEAD_PAD), BF16),
                   jax.ShapeDtypeStruct((m, MLA_HEADS * MLA_V), BF16)],
        grid=(m // tm,),
        in_specs=[pl.BlockSpec((tm, MLA_KV_RANK), lambda i: (i, ckv_blk)),
                  pl.BlockSpec((tm, LANE), lambda i: (i, kr_blk)), vec(MLA_KV_RANK),
                  pl.BlockSpec((MLA_KV_RANK, MLA_HEADS * MLA_HEAD_PAD), lambda i: (0, 0)), vec(LANE), vec(LANE),
                  pl.BlockSpec((tm, LANE), lambda i: (i, 0)), pl.BlockSpec((tm, LANE), lambda i: (i, 0))],
        out_specs=[pl.BlockSpec((tm, MLA_HEADS * MLA_HEAD_PAD), lambda i: (i, 0)),
                   pl.BlockSpec((tm, MLA_HEADS * MLA_V), lambda i: (i, 0))],
        compiler_params=_cp("parallel"),
        name="mla_kv_up",
    )(px, px, ga, w, gn, gr, cos, sin)


def _attn_kernel(q_ref, k_ref, v_ref, o_ref, m_ref, l_ref, acc_ref):
    j = pl.program_id(2)

    @pl.when(j == 0)
    def _():
        m_ref[...] = jnp.full_like(m_ref, -jnp.inf)
        l_ref[...] = jnp.zeros_like(l_ref)
        acc_ref[...] = jnp.zeros_like(acc_ref)

    s = _dot_nt(q_ref[...], k_ref[...])
    m_prev = m_ref[...]
    m_new = jnp.maximum(m_prev, jnp.max(s, axis=-1, keepdims=True))
    alpha = jnp.exp(m_prev - m_new)
    p = jnp.exp(s - m_new)
    l_ref[...] = alpha * l_ref[...] + jnp.sum(p, axis=-1, keepdims=True)
    acc_ref[...] = alpha * acc_ref[...] + _dot(p.astype(BF16), v_ref[...])
    m_ref[...] = m_new

    @pl.when(j == pl.num_programs(2) - 1)
    def _():
        o_ref[...] = (acc_ref[...] / l_ref[...]).astype(o_ref.dtype)


def _attn_tiles(tq_total, tk_total):
    tq = min(1024, tq_total)
    tk = tk_total
    for cand in (1280, 1024, 512, 256):
        if tk_total % cand == 0:
            tk = cand
            break
    return tq, tk


def _attention(q, k, v):
    tq_total, tk_total = q.shape[0], k.shape[0]
    tq, tk = _attn_tiles(tq_total, tk_total)
    return pl.pallas_call(
        _attn_kernel,
        out_shape=jax.ShapeDtypeStruct((tq_total, MLA_HEADS * MLA_V), BF16),
        grid=(MLA_HEADS, tq_total // tq, tk_total // tk),
        in_specs=[pl.BlockSpec((tq, MLA_HEAD_PAD), lambda h, i, j: (i, h)),
                  pl.BlockSpec((tk, MLA_HEAD_PAD), lambda h, i, j: (j, h)),
                  pl.BlockSpec((tk, MLA_V), lambda h, i, j: (j, h))],
        out_specs=pl.BlockSpec((tq, MLA_V), lambda h, i, j: (i, h)),
        scratch_shapes=[pltpu.VMEM((tq, 1), F32), pltpu.VMEM((tq, 1), F32), pltpu.VMEM((tq, MLA_V), F32)],
        compiler_params=_cp("parallel", "parallel", "arbitrary"),
        name="mla_attention",
    )(q, k, v)


CONV_TILE = 512
HALO = 16


def _conv_kernel(gb_ref, gc_ref, u_ref, gcp_ref, up_ref, gcn_ref, un_ref, w_ref, b_ref, o_ref):
    i = pl.program_id(0)
    z = gc_ref[...].astype(F32) * u_ref[...].astype(F32)
    tm = z.shape[0]
    z_before = gcp_ref[HALO - 1:HALO, :].astype(F32) * up_ref[HALO - 1:HALO, :].astype(F32)
    z_after = gcn_ref[0:1, :].astype(F32) * un_ref[0:1, :].astype(F32)
    z_before = jnp.where(i > 0, z_before, 0.0)
    z_after = jnp.where(i < pl.num_programs(0) - 1, z_after, 0.0)
    rowid = lax.broadcasted_iota(jnp.int32, z.shape, 0)
    z_prev = jnp.where(rowid == 0, z_before, pltpu.roll(z, 1, axis=0))
    z_next = jnp.where(rowid == tm - 1, z_after, pltpu.roll(z, tm - 1, axis=0))
    w = w_ref[...]
    y = z_prev * w[0:1] + z * w[1:2] + z_next * w[2:3] + b_ref[...]
    o_ref[...] = (gb_ref[...].astype(F32) * y).astype(o_ref.dtype)


def _conv(px, col0, conv_w, conv_b):
    m = px.shape[0]
    tm = min(512, m)
    cb = CONV_WIDTH // CONV_TILE
    b0 = col0 // CONV_TILE
    nhalo = m // HALO
    main = lambda off: pl.BlockSpec((tm, CONV_TILE), lambda i, j: (i, b0 + off * cb + j))
    prev = lambda off: pl.BlockSpec(
        (HALO, CONV_TILE), lambda i, j: (jnp.maximum(i * (tm // HALO) - 1, 0), b0 + off * cb + j))
    nxt = lambda off: pl.BlockSpec(
        (HALO, CONV_TILE), lambda i, j: (jnp.minimum((i + 1) * (tm // HALO), nhalo - 1), b0 + off * cb + j))
    return pl.pallas_call(
        _conv_kernel,
        out_shape=jax.ShapeDtypeStruct((m, CONV_WIDTH), BF16),
        grid=(m // tm, cb),
        in_specs=[main(0), main(1), main(2), prev(1), prev(2), nxt(1), nxt(2),
                  pl.BlockSpec((3, CONV_TILE), lambda i, j: (0, j)),
                  pl.BlockSpec((1, CONV_TILE), lambda i, j: (0, j))],
        out_specs=pl.BlockSpec((tm, CONV_TILE), lambda i, j: (i, j)),
        compiler_params=_cp("parallel", "parallel"),
        name="gated_conv3",
    )(px, px, px, px, px, px, px, conv_w, conv_b)


def _moe_up_kernel(h_ref, wg_ref, wu_ref, gates_ref, o_ref):
    e = pl.program_id(1)
    h = h_ref[...]
    a = _dot(h, wg_ref[...])
    u = _dot(h, wu_ref[...])
    gates = gates_ref[...]
    lane = lax.broadcasted_iota(jnp.int32, gates.shape, 1)
    g = jnp.sum(jnp.where(lane == e, gates, 0.0), axis=-1, keepdims=True)
    o_ref[...] = (a * jax.nn.sigmoid(a) * u * g).astype(o_ref.dtype)


def _moe_up(h, wg, wu, gates):
    m, d = h.shape
    tm = min(1024, m)
    return pl.pallas_call(
        _moe_up_kernel,
        out_shape=jax.ShapeDtypeStruct((m, N_EXPERTS * D_FF), BF16),
        grid=(m // tm, N_EXPERTS),
        in_specs=[pl.BlockSpec((tm, d), lambda i, e: (i, 0)),
                  pl.BlockSpec((None, d, D_FF), lambda i, e: (e, 0, 0)),
                  pl.BlockSpec((None, d, D_FF), lambda i, e: (e, 0, 0)),
                  pl.BlockSpec((tm, LANE), lambda i, e: (i, 0))],
        out_specs=pl.BlockSpec((tm, D_FF), lambda i, e: (i, e)),
        compiler_params=_cp("parallel", "arbitrary"),
        name="moe_up",
    )(h, wg, wu, gates)


GLA_LEVELS = 6
GLA_EBLOCKS = 2 + 2 * GLA_LEVELS


def _gla_tables(reverse):
    n = GLA_CHUNK
    t = np.arange(n)[:, None]
    r = np.arange(n)[None, :]
    blocks = [r <= t, r > t]
    masks = []
    for lvl in range(GLA_LEVELS):
        bs = 1 << lvl
        blocks.append((r >= (t & ~(bs - 1))) & (r <= t))
        blocks.append((r > t) & (r <= (t | (bs - 1))))
        masks.append(((t >> (lvl + 1)) == (r >> (lvl + 1))) & (((t >> lvl) & 1) == 1) & (((r >> lvl) & 1) == 0))
    masks.append(t == r)
    e = np.stack(blocks).astype(np.float32)
    msk = np.stack(masks).astype(np.float32)
    if reverse:
        e, msk = e[:, ::-1, ::-1], msk[:, ::-1, ::-1]
    return jnp.asarray(e.reshape(-1, n)), jnp.asarray(msk)


def _gla_intra_kernel(k_ref, v_ref, q_ref, a_ref, wa_ref, ba_ref, e_ref, m_ref,
                      oi_ref, qd_ref, kd_ref, dl_ref, *, nchunk):
    n = GLA_CHUNK
    scale = GLA_DK ** -0.5

    def chunk(c, carry):
        rows = pl.ds(pl.multiple_of(c * n, n), n)
        z = _dot_f32(a_ref[rows, :].astype(F32), wa_ref[...]) + ba_ref[...]
        la = (jnp.minimum(z, 0.0) - jnp.log1p(jnp.exp(-jnp.abs(z)))) * (1.0 / GLA_TEMP)
        ex = _dot_f32(e_ref[...], la)
        q = q_ref[rows, :].astype(F32) * scale
        k = k_ref[rows, :].astype(F32)
        qd_ref[rows, :] = (q * jnp.exp(ex[0:n])).astype(BF16)
        kd_ref[rows, :] = (k * jnp.exp(ex[n:2 * n])).astype(BF16)
        dl_ref[pl.ds(c, 1), :] = jnp.exp(jnp.sum(la, axis=0, keepdims=True))
        att = m_ref[GLA_LEVELS] * _dot_nt(q.astype(BF16), k.astype(BF16))
        for lvl in range(GLA_LEVELS):
            qs = (q * jnp.exp(ex[(2 + 2 * lvl) * n:(3 + 2 * lvl) * n])).astype(BF16)
            ks = (k * jnp.exp(ex[(3 + 2 * lvl) * n:(4 + 2 * lvl) * n])).astype(BF16)
            att += m_ref[lvl] * _dot_nt(qs, ks)
        oi_ref[rows, :] = _dot(att.astype(BF16), v_ref[rows, :])
        return carry

    lax.fori_loop(0, nchunk, chunk, 0)


def _gla_blocks(m):
    tb = min(512, m)
    return tb, tb // GLA_CHUNK


def _gla_intra(px, cols, wa_pad, ba, reverse):
    m = px.shape[0]
    tb, nchunk = _gla_blocks(m)
    kb, vb, qb, ab = cols
    e, msk = _gla_tables(reverse)
    return pl.pallas_call(
        functools.partial(_gla_intra_kernel, nchunk=nchunk),
        out_shape=[jax.ShapeDtypeStruct((m, GLA_VAL), F32), jax.ShapeDtypeStruct((m, GLA_KEY), BF16),
                   jax.ShapeDtypeStruct((m, GLA_KEY), BF16), jax.ShapeDtypeStruct((m // GLA_CHUNK, GLA_KEY), F32)],
        grid=(m // tb, GLA_HEADS),
        in_specs=[pl.BlockSpec((tb, GLA_DK), lambda i, h: (i, kb + h)),
                  pl.BlockSpec((tb, GLA_DV), lambda i, h: (i, vb + h)),
                  pl.BlockSpec((tb, GLA_DK), lambda i, h: (i, qb + h)),
                  pl.BlockSpec((tb, LANE), lambda i, h: (i, ab)),
                  pl.BlockSpec((LANE, GLA_DK), lambda i, h: (0, h)),
                  pl.BlockSpec((1, GLA_DK), lambda i, h: (0, h)),
                  pl.BlockSpec(e.shape, lambda i, h: (0, 0)),
                  pl.BlockSpec(msk.shape, lambda i, h: (0, 0, 0))],
        out_specs=[pl.BlockSpec((tb, GLA_DV), lambda i, h: (i, h)),
                   pl.BlockSpec((tb, GLA_DK), lambda i, h: (i, h)),
                   pl.BlockSpec((tb, GLA_DK), lambda i, h: (i, h)),
                   pl.BlockSpec((nchunk, GLA_DK), lambda i, h: (i, h))],
        compiler_params=_cp("parallel", "parallel"),
        name="gla_intra_bwd" if reverse else "gla_intra_fwd",
    )(px, px, px, px, wa_pad, ba, e, msk)


def _gla_scan_kernel(qd_ref, kd_ref, v_ref, dl_ref, oi_ref, s0_ref, o_ref, sf_ref, s_ref, *, nchunk, reverse):
    i = pl.program_id(1)
    n = GLA_CHUNK

    @pl.when(i == 0)
    def _():
        s_ref[...] = s0_ref[...]

    def chunk(cc, carry):
        c = (nchunk - 1 - cc) if reverse else cc
        rows = pl.ds(pl.multiple_of(c * n, n), n)
        st = s_ref[...]
        o_ref[rows, :] = oi_ref[rows, :] + _dot_nt(qd_ref[rows, :], st.astype(BF16))
        s_ref[...] = st * dl_ref[pl.ds(c, 1), :] + _dot_tn(v_ref[rows, :], kd_ref[rows, :])
        return carry

    lax.fori_loop(0, nchunk, chunk, 0)

    @pl.when(i == pl.num_programs(1) - 1)
    def _():
        sf_ref[...] = s_ref[...]


def _gla_scan(px, vb, qd, kd, dl, oi, s0, reverse):
    m = px.shape[0]
    tb, nchunk = _gla_blocks(m)
    nb = m // tb
    blk = (lambda i: nb - 1 - i) if reverse else (lambda i: i)
    return pl.pallas_call(
        functools.partial(_gla_scan_kernel, nchunk=nchunk, reverse=reverse),
        out_shape=[jax.ShapeDtypeStruct((m, GLA_VAL), F32),
                   jax.ShapeDtypeStruct((GLA_HEADS, GLA_DV, GLA_DK), F32)],
        grid=(GLA_HEADS, nb),
        in_specs=[pl.BlockSpec((tb, GLA_DK), lambda h, i: (blk(i), h)),
                  pl.BlockSpec((tb, GLA_DK), lambda h, i: (blk(i), h)),
                  pl.BlockSpec((tb, GLA_DV), lambda h, i: (blk(i), vb + h)),
                  pl.BlockSpec((nchunk, GLA_DK), lambda h, i: (blk(i), h)),
                  pl.BlockSpec((tb, GLA_DV), lambda h, i: (blk(i), h)),
                  pl.BlockSpec((None, GLA_DV, GLA_DK), lambda h, i: (h, 0, 0))],
        out_specs=[pl.BlockSpec((tb, GLA_DV), lambda h, i: (blk(i), h)),
                   pl.BlockSpec((None, GLA_DV, GLA_DK), lambda h, i: (h, 0, 0))],
        scratch_shapes=[pltpu.VMEM((GLA_DV, GLA_DK), F32)],
        compiler_params=_cp("parallel", "arbitrary"),
        name="gla_scan_bwd" if reverse else "gla_scan_fwd",
    )(qd, kd, px, dl, oi, s0)


def _gla_gate_kernel(of_ref, ob_ref, g_ref, gain_ref, o_ref):
    gain = gain_ref[...]
    for h in range(GLA_HEADS):
        sl = slice(h * GLA_DV, (h + 1) * GLA_DV)
        o = _rms(of_ref[:, sl] + ob_ref[:, sl]) * gain
        g = g_ref[:, sl].astype(F32)
        o_ref[:, sl] = (o * (g * jax.nn.sigmoid(g))).astype(BF16)


def _gla_gate(o_f, o_b, px, g_blk, gain):
    m = o_f.shape[0]
    tm = min(256, m)
    return pl.pallas_call(
        _gla_gate_kernel,
        out_shape=jax.ShapeDtypeStruct((m, GLA_VAL), BF16),
        grid=(m // tm,),
        in_specs=[pl.BlockSpec((tm, GLA_VAL), lambda i: (i, 0)), pl.BlockSpec((tm, GLA_VAL), lambda i: (i, 0)),
                  pl.BlockSpec((tm, GLA_VAL), lambda i: (i, g_blk)), pl.BlockSpec((1, GLA_DV), lambda i: (0, 0))],
        out_specs=pl.BlockSpec((tm, GLA_VAL), lambda i: (i, 0)),
        compiler_params=_cp("parallel"),
        name="gla_gate",
    )(o_f, o_b, px, gain)


def _pad_cols(w, mult):
    pad = (-w.shape[1]) % mult
    return jnp.pad(w, ((0, 0), (0, pad))) if pad else w


def _row(v, width=None):
    v = v.reshape(1, -1).astype(F32)
    return _pad_cols(v, width) if width else v


MM_TN = 512

AB_CKV_BLK = MLA_Q_RANK // MLA_KV_RANK
AB_CONV_COL = MLA_Q_RANK + MLA_KV_RANK
AB_KR_BLK = (AB_CONV_COL + 3 * CONV_WIDTH) // LANE
GLA_K_BLK = 0
GLA_V_BLK = GLA_KEY // GLA_DV
GLA_Q_BLK = (GLA_KEY + GLA_VAL) // GLA_DK
GLA_G_BLK = (2 * GLA_KEY + GLA_VAL) // GLA_VAL
GLA_A_BLK = (2 * GLA_KEY + 2 * GLA_VAL) // LANE


def _ab_w_in(w):
    ckv = w[:, :MLA_KV_RANK]
    kr = w[:, MLA_KV_RANK:MLA_KV_RANK + MLA_ROPE]
    cq = w[:, MLA_KV_RANK + MLA_ROPE:MLA_KV_RANK + MLA_ROPE + MLA_Q_RANK]
    conv = w[:, MLA_KV_RANK + MLA_ROPE + MLA_Q_RANK:]
    return _pad_cols(jnp.concatenate([cq, ckv, conv, kr], axis=1), MM_TN).astype(BF16)


def _ab_w_uq(w):
    w = w.reshape(MLA_Q_RANK, MLA_HEADS, MLA_QK)
    w = jnp.pad(w, ((0, 0), (0, 0), (0, MLA_HEAD_PAD - MLA_QK)))
    return w.reshape(MLA_Q_RANK, MLA_HEADS * MLA_HEAD_PAD).astype(BF16)


def _gla_w_in(w):
    o1, o2 = GLA_KEY, GLA_KEY + GLA_VAL
    o3 = o2 + 2 * GLA_RANK
    k, v, a, q, g = w[:, :o1], w[:, o1:o2], w[:, o2:o3], w[:, o3:o3 + GLA_KEY], w[:, o3 + GLA_KEY:]
    return _pad_cols(jnp.concatenate([k, v, q, g, a], axis=1), MM_TN).astype(BF16)


def _gla_wa_pad(w_a, offset):
    return jnp.zeros((LANE, GLA_KEY), F32).at[offset:offset + GLA_RANK].set(w_a.astype(F32))


def _rope_tables(pos_rows, pos_cols):
    half = MLA_ROPE // 4
    inv_freq = ROPE_BASE ** (-jnp.arange(half, dtype=F32) / half)
    ar = pos_rows.astype(F32)[:, None] * inv_freq[None, :]
    ac = pos_cols.astype(F32)[:, None] * inv_freq[None, :]
    zeros = jnp.zeros((ar.shape[0], LANE - MLA_ROPE), F32)
    cos = jnp.concatenate([jnp.cos(ar), jnp.cos(ar), jnp.cos(ac), jnp.cos(ac), zeros], axis=1)
    sin = jnp.concatenate([-jnp.sin(ar), jnp.sin(ar), -jnp.sin(ac), jnp.sin(ac), zeros], axis=1)
    return cos, sin


def _mixer_mla_conv(hx, hc, p, rope_x, rope_c, need_ctx):
    w_in, w_uq, w_ukv, w_out = p["w_in"], p["w_uq"], p["w_ukv"], p["w_out"]

    def project(h, rope):
        px = _mm([(h, w_in)], out_dtype=BF16, name="ab_in_proj")
        k, v = _kv_up(px, AB_CKV_BLK, AB_KR_BLK, p["kv_a_gain"], w_ukv, p["k_gain_n"], p["k_gain_r"], *rope)
        return px, k, v

    def branch_out(px, k, v, rope):
        q = _q_up(px, p["q_a_gain"], w_uq, p["q_gain_n"], p["q_gain_r"], *rope)
        o = _attention(q, k, v)
        y = _conv(px, AB_CONV_COL, p["conv_w"], p["conv_b"])
        return o, y

    px_c, k_c, v_c = project(hc, rope_c)
    px_x, k_x, v_x = project(hx, rope_x)
    k_all = jnp.concatenate([k_c, k_x], axis=0)
    v_all = jnp.concatenate([v_c, v_x], axis=0)
    out_x = branch_out(px_x, k_all, v_all, rope_x)
    out_c = branch_out(px_c, k_c, v_c, rope_c) if need_ctx else None
    half = MLA_HEADS * MLA_V
    return out_x, out_c, (w_out[:half], w_out[half:])


def _mixer_gla(hx, hc, p, need_ctx):
    cols = (GLA_K_BLK, GLA_V_BLK, GLA_Q_BLK, GLA_A_BLK)
    zero_state = jnp.zeros((GLA_HEADS, GLA_DV, GLA_DK), F32)

    def run(h, s_f, s_b):
        px = _mm([(h, p["w_in"])], out_dtype=BF16, name="gla_in_proj")
        outs, states = [], []
        for reverse, wa, ba, s0 in ((False, p["wa_f"], p["ba_f"], s_f), (True, p["wa_b"], p["ba_b"], s_b)):
            oi, qd, kd, dl = _gla_intra(px, cols, wa, ba, reverse)
            o, s_fin = _gla_scan(px, GLA_V_BLK, qd, kd, dl, oi, s0, reverse)
            outs.append(o)
            states.append(s_fin)
        return px, outs, states

    px_c, o_c, (s_f, s_b) = run(hc, zero_state, zero_state)
    px_x, o_x, _ = run(hx, s_f, s_b)
    out_x = _gla_gate(o_x[0], o_x[1], px_x, GLA_G_BLK, p["o_gain"])
    out_c = _gla_gate(o_c[0], o_c[1], px_c, GLA_G_BLK, p["o_gain"]) if need_ctx else None
    return out_x, out_c


def _moe(xres, gain, shift, scale, gate, router, wg, wu, wd):
    h, gates = _modulate(xres, gain, shift, scale, router=router)
    hid = _moe_up(h, wg, wu, gates)
    return _mm([(hid, wd)], out_dtype=F32, residual=(xres, gate), name="moe_down")


def kernel(x, c, ctx, c_ctx, ada_w, ada_b, norm_mix_gain, norm_ffn_gain, ab_w_in, ab_q_a_gain, ab_w_uq, ab_q_gain, ab_kv_a_gain, ab_w_ukv, ab_k_gain, ab_conv_w, ab_conv_b, ab_w_out, gla_w_in, gla_w_af, gla_b_af, gla_w_ab, gla_b_ab, gla_o_gain, gla_w_out, router_w, router_b, moe_w_gate, moe_w_up, moe_w_down):
    b, s, d = x.shape
    assert b == 1 and s % GRID_W == 0
    depth = ada_w.shape[0]
    xs, cx = x[0], ctx[0]
    n_ctx = cx.shape[0]

    mod = _adaln(c.reshape(d, 1), c_ctx.reshape(d, 1), ada_w, ada_b).reshape(depth, 2, 6, 1, d)
    tok = jnp.arange(s, dtype=jnp.int32)
    rope_x = _rope_tables(tok // GRID_W, tok % GRID_W)
    rope_c = (jnp.concatenate([jnp.ones((n_ctx, MLA_ROPE), F32), jnp.zeros((n_ctx, LANE - MLA_ROPE), F32)], axis=1),
              jnp.zeros((n_ctx, LANE), F32))
    router = (_pad_cols(router_w.astype(F32), LANE), router_b.reshape(N_EXPERTS, 1).astype(F32))

    for i in range(depth):
        last = i == depth - 1
        j = i // 2
        mx, mc = mod[i, 0], mod[i, 1]
        gain_mix = _row(norm_mix_gain[i])
        hx = _modulate(xs, gain_mix, mx[0], mx[1])
        hc = _modulate(cx, gain_mix, mc[0], mc[1])
        if i % 2 == 0:
            p = dict(
                w_in=_ab_w_in(ab_w_in[j]), w_uq=_ab_w_uq(ab_w_uq[j]), w_ukv=ab_w_ukv[j].astype(BF16),
                w_out=ab_w_out[j].astype(BF16),
                q_a_gain=_row(ab_q_a_gain[j]), kv_a_gain=_row(ab_kv_a_gain[j]),
                q_gain_n=_row(ab_q_gain[j][:MLA_NOPE]), q_gain_r=_row(ab_q_gain[j][MLA_NOPE:], LANE),
                k_gain_n=_row(ab_k_gain[j][:MLA_NOPE]), k_gain_r=_row(ab_k_gain[j][MLA_NOPE:], LANE),
                conv_w=ab_conv_w[j].astype(F32), conv_b=_row(ab_conv_b[j]))
            (o_x, y_x), out_c, (w_o, w_y) = _mixer_mla_conv(hx, hc, p, rope_x, rope_c, not last)
            xs = _mm([(o_x, w_o), (y_x, w_y)], out_dtype=F32, residual=(xs, mx[2]), name="ab_out_proj")
            if not last:
                cx = _mm([(out_c[0], w_o), (out_c[1], w_y)], out_dtype=F32, residual=(cx, mc[2]), name="ab_out_proj")
        else:
            p = dict(
                w_in=_gla_w_in(gla_w_in[j]),
                wa_f=_gla_wa_pad(gla_w_af[j], 0), wa_b=_gla_wa_pad(gla_w_ab[j], GLA_RANK),
                ba_f=_row(gla_b_af[j]), ba_b=_row(gla_b_ab[j]), o_gain=_row(gla_o_gain[j]))
            w_out = gla_w_out[j].astype(BF16)
            y_x, y_c = _mixer_gla(hx, hc, p, not last)
            xs = _mm([(y_x, w_out)], out_dtype=F32, residual=(xs, mx[2]), name="gla_out_proj")
            if not last:
                cx = _mm([(y_c, w_out)], out_dtype=F32, residual=(cx, mc[2]), name="gla_out_proj")
        gain_ffn = _row(norm_ffn_gain[i])
        wg, wu = moe_w_gate[i].astype(BF16), moe_w_up[i].astype(BF16)
        wd = moe_w_down[i].reshape(N_EXPERTS * D_FF, d).astype(BF16)
        xs = _moe(xs, gain_ffn, mx[3], mx[4], mx[5], router, wg, wu, wd)
        if not last:
            cx = _moe(cx, gain_ffn, mc[3], mc[4], mc[5], router, wg, wu, wd)
    return xs[None]
```

```python
import functools

import numpy as np
import jax
import jax.numpy as jnp
from jax import lax
from jax.experimental import pallas as pl
from jax.experimental.pallas import tpu as pltpu

F32 = jnp.float32
BF16 = jnp.bfloat16
HIGHEST = lax.Precision.HIGHEST

LANE = 128
SUBLANE = 8
VMEM_LIMIT_BYTES = 56 * 1024 * 1024

GRID_W = 64
EPS = 1e-6
MLA_HEADS = 16
MLA_Q_RANK = 1024
MLA_KV_RANK = 512
MLA_NOPE = 128
MLA_ROPE = 64
MLA_V = 128
MLA_QK = MLA_NOPE + MLA_ROPE
MLA_HEAD_PAD = 2 * LANE
ROPE_BASE = 10000.0
LOG2E = 1.4426950408889634
CONV_WIDTH = 2048
GLA_HEADS = 8
GLA_DK = 256
GLA_DV = 512
GLA_KEY = GLA_HEADS * GLA_DK
GLA_VAL = GLA_HEADS * GLA_DV
GLA_RANK = 16
GLA_TEMP = 16.0
GLA_CHUNK = 64
N_EXPERTS = 16
N_GROUPS = 4
GROUP_SIZE = N_EXPERTS // N_GROUPS
D_FF = 256


def _cp(*sem):
    return pltpu.CompilerParams(dimension_semantics=sem, vmem_limit_bytes=VMEM_LIMIT_BYTES)


def _dot(a, b):
    return jnp.dot(a, b, preferred_element_type=F32)


def _dot_nt(a, b):
    return lax.dot_general(a, b, (((1,), (1,)), ((), ())), preferred_element_type=F32)


def _dot_tn(a, b):
    return lax.dot_general(a, b, (((0,), (0,)), ((), ())), preferred_element_type=F32)


def _dot_f32(a, b):
    return jnp.dot(a, b, precision=HIGHEST, preferred_element_type=F32)


def _rms(v, n=None):
    ss = jnp.sum(v * v, axis=-1, keepdims=True)
    return v * lax.rsqrt(ss * (1.0 / (n if n is not None else v.shape[-1])) + EPS)


def _adaln_kernel(c_ref, cc_ref, w_ref, b_ref, o_ref, acc_ref):
    k = pl.program_id(2)

    @pl.when(k == 0)
    def _():
        acc_ref[...] = jnp.zeros_like(acc_ref)

    w = w_ref[...]
    tk, tn = w.shape
    for r, ref in enumerate((c_ref, cc_ref)):
        cv = ref[...]
        cs = cv * jax.nn.sigmoid(cv)
        acc_ref[r] += jnp.sum((w * cs).reshape(tk // SUBLANE, SUBLANE, tn), axis=0)

    @pl.when(k == pl.num_programs(2) - 1)
    def _():
        o_ref[...] = jnp.sum(acc_ref[...], axis=1) + b_ref[...]


def _adaln(c_col, cc_col, ada_w, ada_b):
    depth, d, n = ada_w.shape
    tk, tn = 512, 2048
    return pl.pallas_call(
        _adaln_kernel,
        out_shape=jax.ShapeDtypeStruct((depth, 2, n), F32),
        grid=(depth, n // tn, d // tk),
        in_specs=[
            pl.BlockSpec((tk, 1), lambda l, j, k: (k, 0)),
            pl.BlockSpec((tk, 1), lambda l, j, k: (k, 0)),
            pl.BlockSpec((None, tk, tn), lambda l, j, k: (l, k, j)),
            pl.BlockSpec((None, 1, tn), lambda l, j, k: (l, 0, j)),
        ],
        out_specs=pl.BlockSpec((None, 2, tn), lambda l, j, k: (l, 0, j)),
        scratch_shapes=[pltpu.VMEM((2, SUBLANE, tn), F32)],
        compiler_params=_cp("parallel", "parallel", "arbitrary"),
        name="adaln_mod",
    )(c_col, cc_col, ada_w, ada_b.reshape(depth, 1, n))


def _route_gates(logits, bias_col):
    tm = logits.shape[0]
    s = jax.nn.sigmoid(logits.T[:N_EXPERTS])
    sel = s + bias_col
    row = [sel[e:e + 1] for e in range(N_EXPERTS)]
    score = []
    for g in range(N_GROUPS):
        a, b, c2, d2 = row[GROUP_SIZE * g:GROUP_SIZE * (g + 1)]
        hi1, lo1 = jnp.maximum(a, b), jnp.minimum(a, b)
        hi2, lo2 = jnp.maximum(c2, d2), jnp.minimum(c2, d2)
        score.append(jnp.maximum(hi1, hi2) + jnp.maximum(jnp.minimum(hi1, hi2), jnp.maximum(lo1, lo2)))
    best, gidx = score[0], jnp.zeros((1, tm), jnp.int32)
    for g in range(1, N_GROUPS):
        upd = score[g] > best
        gidx = jnp.where(upd, g, gidx)
        best = jnp.where(upd, score[g], best)
    neg = jnp.full((1, tm), -jnp.inf, F32)
    cand = [jnp.where(gidx == (e // GROUP_SIZE), row[e], neg) for e in range(N_EXPERTS)]

    def argmax_first(vals):
        bv, bi = vals[0], jnp.zeros((1, tm), jnp.int32)
        for e in range(1, N_EXPERTS):
            upd = vals[e] > bv
            bi = jnp.where(upd, e, bi)
            bv = jnp.where(upd, vals[e], bv)
        return bi

    i1 = argmax_first(cand)
    i2 = argmax_first([jnp.where(i1 == e, neg, cand[e]) for e in range(N_EXPERTS)])
    eidx = lax.broadcasted_iota(jnp.int32, (N_EXPERTS, tm), 0)
    w1 = jnp.sum(jnp.where(eidx == i1, s, 0.0), axis=0, keepdims=True)
    w2 = jnp.sum(jnp.where(eidx == i2, s, 0.0), axis=0, keepdims=True)
    den = w1 + w2
    efull = lax.broadcasted_iota(jnp.int32, (LANE, tm), 0)
    gates_t = jnp.where(efull == i1, w1 / den, 0.0) + jnp.where(efull == i2, w2 / den, 0.0)
    return gates_t.T


def _modulate_kernel(x_ref, g_ref, sh_ref, sc_ref, *rest, route):
    x = x_ref[...]
    h = _rms(x) * g_ref[...] * (1.0 + sc_ref[...]) + sh_ref[...]
    if route:
        wr_ref, br_ref, h_ref, gates_ref = rest
        logits = _dot_f32(h, wr_ref[...])
        gates_ref[...] = _route_gates(logits, br_ref[...])
    else:
        (h_ref,) = rest
    h_ref[...] = h.astype(h_ref.dtype)


def _modulate(x, gain, shift, scale, router=None):
    m, d = x.shape
    tm = min(256, m)
    vec = pl.BlockSpec((1, d), lambda i: (0, 0))
    in_specs = [pl.BlockSpec((tm, d), lambda i: (i, 0)), vec, vec, vec]
    args = [x, gain, shift, scale]
    out_shape = [jax.ShapeDtypeStruct((m, d), BF16)]
    out_specs = [pl.BlockSpec((tm, d), lambda i: (i, 0))]
    if router is not None:
        wr, br = router
        in_specs += [pl.BlockSpec((d, LANE), lambda i: (0, 0)), pl.BlockSpec((N_EXPERTS, 1), lambda i: (0, 0))]
        args += [wr, br]
        out_shape.append(jax.ShapeDtypeStruct((m, LANE), F32))
        out_specs.append(pl.BlockSpec((tm, LANE), lambda i: (i, 0)))
    out = pl.pallas_call(
        functools.partial(_modulate_kernel, route=router is not None),
        out_shape=out_shape, grid=(m // tm,), in_specs=in_specs, out_specs=out_specs,
        compiler_params=_cp("parallel"),
        name="modulate_route" if router is not None else "modulate",
    )(*args)
    return out if router is not None else out[0]


def _mm_kernel(*refs, n_pairs, residual):
    a_refs, w_refs = refs[:n_pairs], refs[n_pairs:2 * n_pairs]
    o_ref = refs[-1]
    acc = _dot(a_refs[0][...], w_refs[0][...])
    for a_ref, w_ref in zip(a_refs[1:], w_refs[1:]):
        acc += _dot(a_ref[...], w_ref[...])
    if residual:
        x_ref, g_ref = refs[2 * n_pairs:2 * n_pairs + 2]
        acc = x_ref[...] + g_ref[...] * acc
    o_ref[...] = acc.astype(o_ref.dtype)


def _mm(pairs, *, out_dtype, residual=None, tm=1024, tn=512, name="mm"):
    m = pairs[0][0].shape[0]
    n = pairs[0][1].shape[1]
    tm, tn = min(tm, m), min(tn, n)
    assert m % tm == 0 and n % tn == 0, (m, n, tm, tn)
    in_specs = [pl.BlockSpec((tm, a.shape[1]), lambda i, j: (i, 0)) for a, _ in pairs]
    in_specs += [pl.BlockSpec((w.shape[0], tn), lambda i, j: (0, j)) for _, w in pairs]
    args = [a for a, _ in pairs] + [w for _, w in pairs]
    if residual is not None:
        in_specs += [pl.BlockSpec((tm, tn), lambda i, j: (i, j)), pl.BlockSpec((1, tn), lambda i, j: (0, j))]
        args += list(residual)
    return pl.pallas_call(
        functools.partial(_mm_kernel, n_pairs=len(pairs), residual=residual is not None),
        out_shape=jax.ShapeDtypeStruct((m, n), out_dtype),
        grid=(m // tm, n // tn), in_specs=in_specs,
        out_specs=pl.BlockSpec((tm, tn), lambda i, j: (i, j)),
        compiler_params=_cp("parallel", "arbitrary"),
        name=name,
    )(*args)


def _rope_rotate(v, cos, sin_signed):
    lane = lax.broadcasted_iota(jnp.int32, (1, LANE), 1)
    first_half = (lane % 32) < 16
    partner = jnp.where(first_half, pltpu.roll(v, LANE - 16, axis=1), pltpu.roll(v, 16, axis=1))
    return v * cos + partner * sin_signed


def _q_up_kernel(cq_ref, ga_ref, w_ref, gn_ref, gr_ref, cos_ref, sin_ref, o_ref, *, scale):
    cqn = (_rms(cq_ref[...].astype(F32)) * ga_ref[...]).astype(BF16)
    q = _dot(cqn, w_ref[...])
    cos, sin = cos_ref[...], sin_ref[...]
    gn, gr = gn_ref[...] * scale, gr_ref[...] * scale
    for h in range(MLA_HEADS):
        o = h * MLA_HEAD_PAD
        o_ref[:, o:o + LANE] = (_rms(q[:, o:o + LANE]) * gn).astype(BF16)
        qr = _rms(q[:, o + LANE:o + 2 * LANE], MLA_ROPE) * gr
        o_ref[:, o + LANE:o + 2 * LANE] = _rope_rotate(qr, cos, sin).astype(BF16)


def _q_up(px, ga, w, gn, gr, cos, sin):
    m = px.shape[0]
    tm = min(512, m)
    n = MLA_HEADS * MLA_HEAD_PAD
    vec = lambda width: pl.BlockSpec((1, width), lambda i: (0, 0))
    return pl.pallas_call(
        functools.partial(_q_up_kernel, scale=MLA_QK ** -0.5 * LOG2E),
        out_shape=jax.ShapeDtypeStruct((m, n), BF16),
        grid=(m // tm,),
        in_specs=[pl.BlockSpec((tm, MLA_Q_RANK), lambda i: (i, 0)), vec(MLA_Q_RANK),
                  pl.BlockSpec((MLA_Q_RANK, n), lambda i: (0, 0)), vec(LANE), vec(LANE),
                  pl.BlockSpec((tm, LANE), lambda i: (i, 0)), pl.BlockSpec((tm, LANE), lambda i: (i, 0))],
        out_specs=pl.BlockSpec((tm, n), lambda i: (i, 0)),
        compiler_params=_cp("parallel"),
        name="mla_q_up",
    )(px, ga, w, gn, gr, cos, sin)


def _kv_up_kernel(ckv_ref, kr_ref, ga_ref, w_ref, gn_ref, gr_ref, cos_ref, sin_ref, k_ref, v_ref):
    ckvn = (_rms(ckv_ref[...].astype(F32)) * ga_ref[...]).astype(BF16)
    kv = _dot(ckvn, w_ref[...])
    kr = _rms(kr_ref[...].astype(F32), MLA_ROPE) * gr_ref[...]
    kr = _rope_rotate(kr, cos_ref[...], sin_ref[...]).astype(BF16)
    gn = gn_ref[...]
    for h in range(MLA_HEADS):
        o = h * MLA_HEAD_PAD
        k_ref[:, o:o + LANE] = (_rms(kv[:, o:o + LANE]) * gn).astype(BF16)
        k_ref[:, o + LANE:o + 2 * LANE] = kr
        v_ref[:, h * MLA_V:(h + 1) * MLA_V] = kv[:, o + LANE:o + 2 * LANE].astype(BF16)


def _kv_up(px, ckv_blk, kr_blk, ga, w, gn, gr, cos, sin):
    m = px.shape[0]
    tm = min(512, m)
    vec = lambda width: pl.BlockSpec((1, width), lambda i: (0, 0))
    return pl.pallas_call(
        _kv_up_kernel,
        out_shape=[jax.ShapeDtypeStruct((m, MLA_HEADS * MLA_HEAD_PAD), BF16),
                   jax.ShapeDtypeStruct((m, MLA_HEADS * MLA_V), BF16)],
        grid=(m // tm,),
        in_specs=[pl.BlockSpec((tm, MLA_KV_RANK), lambda i: (i, ckv_blk)),
                  pl.BlockSpec((tm, LANE), lambda i: (i, kr_blk)), vec(MLA_KV_RANK),
                  pl.BlockSpec((MLA_KV_RANK, MLA_HEADS * MLA_HEAD_PAD), lambda i: (0, 0)), vec(LANE), vec(LANE),
                  pl.BlockSpec((tm, LANE), lambda i: (i, 0)), pl.BlockSpec((tm, LANE), lambda i: (i, 0))],
        out_specs=[pl.BlockSpec((tm, MLA_HEADS * MLA_HEAD_PAD), lambda i: (i, 0)),
                   pl.BlockSpec((tm, MLA_HEADS * MLA_V), lambda i: (i, 0))],
        compiler_params=_cp("parallel"),
        name="mla_kv_up",
    )(px, px, ga, w, gn, gr, cos, sin)


def _attn_kernel(q_ref, k_ref, v_ref, o_ref, sa_ref, sb_ref, m_ref, l_ref, acc_ref, *, ck, n_chunks):
    m_ref[...] = jnp.full_like(m_ref, -jnp.inf)
    l_ref[...] = jnp.zeros_like(l_ref)
    acc_ref[...] = jnp.zeros_like(acc_ref)

    def rows(c):
        return pl.ds(pl.multiple_of(c * ck, ck), ck)

    def scores(c, s_ref):
        s_ref[...] = _dot_nt(q_ref[...], k_ref[rows(c), :])

    def update(c, s_ref):
        s = s_ref[...]
        m_prev = m_ref[...]
        m_new = jnp.maximum(m_prev, jnp.max(s, axis=-1, keepdims=True))
        alpha = jnp.exp2(m_prev - m_new)
        p = jnp.exp2(s - m_new)
        l_ref[...] = alpha * l_ref[...] + jnp.sum(p, axis=-1, keepdims=True)
        acc_ref[...] = alpha * acc_ref[...] + _dot(p.astype(BF16), v_ref[rows(c), :])
        m_ref[...] = m_new

    scores(0, sa_ref)
    n_pairs = (n_chunks - 1) // 2

    def pair(t, carry):
        c = 2 * t
        scores(c + 1, sb_ref)
        update(c, sa_ref)
        scores(c + 2, sa_ref)
        update(c + 1, sb_ref)
        return carry

    if n_pairs:
        lax.fori_loop(0, n_pairs, pair, 0)
    if n_chunks - 2 * n_pairs == 2:
        scores(n_chunks - 1, sb_ref)
        update(n_chunks - 2, sa_ref)
        update(n_chunks - 1, sb_ref)
    else:
        update(n_chunks - 1, sa_ref)
    o_ref[...] = (acc_ref[...] / l_ref[...]).astype(o_ref.dtype)


def _attn_tiles(tq_total, tk_total):
    tq = min(1024, tq_total)
    ck = tk_total
    for cand in (1280, 1024, 512, 256):
        if tk_total % cand == 0:
            ck = cand
            break
    return tq, ck


def _attention(q, k, v):
    tq_total, tk_total = q.shape[0], k.shape[0]
    tq, ck = _attn_tiles(tq_total, tk_total)
    return pl.pallas_call(
        functools.partial(_attn_kernel, ck=ck, n_chunks=tk_total // ck),
        out_shape=jax.ShapeDtypeStruct((tq_total, MLA_HEADS * MLA_V), BF16),
        grid=(MLA_HEADS, tq_total // tq),
        in_specs=[pl.BlockSpec((tq, MLA_HEAD_PAD), lambda h, i: (i, h)),
                  pl.BlockSpec((tk_total, MLA_HEAD_PAD), lambda h, i: (0, h)),
                  pl.BlockSpec((tk_total, MLA_V), lambda h, i: (0, h))],
        out_specs=pl.BlockSpec((tq, MLA_V), lambda h, i: (i, h)),
        scratch_shapes=[pltpu.VMEM((tq, ck), F32), pltpu.VMEM((tq, ck), F32),
                        pltpu.VMEM((tq, 1), F32), pltpu.VMEM((tq, 1), F32), pltpu.VMEM((tq, MLA_V), F32)],
        compiler_params=_cp("parallel", "arbitrary"),
        name="mla_attention",
    )(q, k, v)


CONV_TILE = 512
HALO = 16


def _conv_kernel(gb_ref, gc_ref, u_ref, gcp_ref, up_ref, gcn_ref, un_ref, w_ref, b_ref, o_ref):
    i = pl.program_id(0)
    z = gc_ref[...].astype(F32) * u_ref[...].astype(F32)
    tm = z.shape[0]
    z_before = gcp_ref[HALO - 1:HALO, :].astype(F32) * up_ref[HALO - 1:HALO, :].astype(F32)
    z_after = gcn_ref[0:1, :].astype(F32) * un_ref[0:1, :].astype(F32)
    z_before = jnp.where(i > 0, z_before, 0.0)
    z_after = jnp.where(i < pl.num_programs(0) - 1, z_after, 0.0)
    rowid = lax.broadcasted_iota(jnp.int32, z.shape, 0)
    z_prev = jnp.where(rowid == 0, z_before, pltpu.roll(z, 1, axis=0))
    z_next = jnp.where(rowid == tm - 1, z_after, pltpu.roll(z, tm - 1, axis=0))
    w = w_ref[...]
    y = z_prev * w[0:1] + z * w[1:2] + z_next * w[2:3] + b_ref[...]
    o_ref[...] = (gb_ref[...].astype(F32) * y).astype(o_ref.dtype)


def _conv(px, col0, conv_w, conv_b):
    m = px.shape[0]
    tm = min(512, m)
    cb = CONV_WIDTH // CONV_TILE
    b0 = col0 // CONV_TILE
    nhalo = m // HALO
    main = lambda off: pl.BlockSpec((tm, CONV_TILE), lambda i, j: (i, b0 + off * cb + j))
    prev = lambda off: pl.BlockSpec(
        (HALO, CONV_TILE), lambda i, j: (jnp.maximum(i * (tm // HALO) - 1, 0), b0 + off * cb + j))
    nxt = lambda off: pl.BlockSpec(
        (HALO, CONV_TILE), lambda i, j: (jnp.minimum((i + 1) * (tm // HALO), nhalo - 1), b0 + off * cb + j))
    return pl.pallas_call(
        _conv_kernel,
        out_shape=jax.ShapeDtypeStruct((m, CONV_WIDTH), BF16),
        grid=(m // tm, cb),
        in_specs=[main(0), main(1), main(2), prev(1), prev(2), nxt(1), nxt(2),
                  pl.BlockSpec((3, CONV_TILE), lambda i, j: (0, j)),
                  pl.BlockSpec((1, CONV_TILE), lambda i, j: (0, j))],
        out_specs=pl.BlockSpec((tm, CONV_TILE), lambda i, j: (i, j)),
        compiler_params=_cp("parallel", "parallel"),
        name="gated_conv3",
    )(px, px, px, px, px, px, px, conv_w, conv_b)


def _moe_up_kernel(h_ref, wg_ref, wu_ref, gates_ref, o_ref):
    e = pl.program_id(1)
    h = h_ref[...]
    a = _dot(h, wg_ref[...])
    u = _dot(h, wu_ref[...])
    gates = gates_ref[...]
    lane = lax.broadcasted_iota(jnp.int32, gates.shape, 1)
    g = jnp.sum(jnp.where(lane == e, gates, 0.0), axis=-1, keepdims=True)
    o_ref[...] = (a * jax.nn.sigmoid(a) * u * g).astype(o_ref.dtype)


def _moe_up(h, wg, wu, gates):
    m, d = h.shape
    tm = min(1024, m)
    return pl.pallas_call(
        _moe_up_kernel,
        out_shape=jax.ShapeDtypeStruct((m, N_EXPERTS * D_FF), BF16),
        grid=(m // tm, N_EXPERTS),
        in_specs=[pl.BlockSpec((tm, d), lambda i, e: (i, 0)),
                  pl.BlockSpec((None, d, D_FF), lambda i, e: (e, 0, 0)),
                  pl.BlockSpec((None, d, D_FF), lambda i, e: (e, 0, 0)),
                  pl.BlockSpec((tm, LANE), lambda i, e: (i, 0))],
        out_specs=pl.BlockSpec((tm, D_FF), lambda i, e: (i, e)),
        compiler_params=_cp("parallel", "arbitrary"),
        name="moe_up",
    )(h, wg, wu, gates)


GLA_LEVELS = 6
GLA_MILD_DECAY = 50.0


def _gla_tables(reverse):
    n = GLA_CHUNK
    t = np.arange(n)[:, None]
    r = np.arange(n)[None, :]
    blocks = [r <= t, r > t]
    masks = []
    for lvl in range(GLA_LEVELS):
        bs = 1 << lvl
        blocks.append((r >= (t & ~(bs - 1))) & (r <= t))
        blocks.append((r > t) & (r <= (t | (bs - 1))))
        masks.append(((t >> (lvl + 1)) == (r >> (lvl + 1))) & (((t >> lvl) & 1) == 1) & (((r >> lvl) & 1) == 0))
    masks.append(t == r)
    masks.append(r <= t)
    e = np.stack(blocks).astype(np.float32)
    msk = np.stack(masks).astype(np.float32)
    if reverse:
        e, msk = e[:, ::-1, ::-1], msk[:, ::-1, ::-1]
    return jnp.asarray(e.reshape(-1, n)), jnp.asarray(msk)


def _gla_intra_kernel(k_ref, v_ref, q_ref, a_ref, wa_ref, ba_ref, e_ref, m_ref,
                      oi_ref, qd_ref, kd_ref, dl_ref, *, nchunk):
    n = GLA_CHUNK
    scale = GLA_DK ** -0.5

    def chunk(c, carry):
        rows = pl.ds(pl.multiple_of(c * n, n), n)
        z = _dot_f32(a_ref[rows, :].astype(F32), wa_ref[...]) + ba_ref[...]
        la = (jnp.minimum(z, 0.0) - jnp.log1p(jnp.exp(-jnp.abs(z)))) * (1.0 / GLA_TEMP)
        ex = _dot_f32(e_ref[0:2 * n, :], la)
        q = q_ref[rows, :].astype(F32) * scale
        k = k_ref[rows, :].astype(F32)
        qd = (q * jnp.exp(ex[0:n])).astype(BF16)
        qd_ref[rows, :] = qd
        kd_ref[rows, :] = (k * jnp.exp(ex[n:2 * n])).astype(BF16)
        total = jnp.sum(la, axis=0, keepdims=True)
        dl_ref[pl.ds(c, 1), :] = jnp.exp(total)
        mild = jnp.min(total) > -GLA_MILD_DECAY

        @pl.when(mild)
        def _():
            ks = (k * jnp.exp(-ex[0:n])).astype(BF16)
            att = m_ref[GLA_LEVELS + 1] * _dot_nt(qd, ks)
            oi_ref[rows, :] = _dot(att.astype(BF16), v_ref[rows, :])

        @pl.when(jnp.logical_not(mild))
        def _():
            exl = _dot_f32(e_ref[2 * n:, :], la)
            att = m_ref[GLA_LEVELS] * _dot_nt(q.astype(BF16), k.astype(BF16))
            for lvl in range(GLA_LEVELS):
                qs = (q * jnp.exp(exl[2 * lvl * n:(2 * lvl + 1) * n])).astype(BF16)
                ks = (k * jnp.exp(exl[(2 * lvl + 1) * n:(2 * lvl + 2) * n])).astype(BF16)
                att += m_ref[lvl] * _dot_nt(qs, ks)
            oi_ref[rows, :] = _dot(att.astype(BF16), v_ref[rows, :])

        return carry

    lax.fori_loop(0, nchunk, chunk, 0)


def _gla_blocks(m):
    tb = min(512, m)
    return tb, tb // GLA_CHUNK


def _gla_intra(px, cols, wa_pad, ba, reverse):
    m = px.shape[0]
    tb, nchunk = _gla_blocks(m)
    kb, vb, qb, ab = cols
    e, msk = _gla_tables(reverse)
    return pl.pallas_call(
        functools.partial(_gla_intra_kernel, nchunk=nchunk),
        out_shape=[jax.ShapeDtypeStruct((m, GLA_VAL), F32), jax.ShapeDtypeStruct((m, GLA_KEY), BF16),
                   jax.ShapeDtypeStruct((m, GLA_KEY), BF16), jax.ShapeDtypeStruct((m // GLA_CHUNK, GLA_KEY), F32)],
        grid=(m // tb, GLA_HEADS),
        in_specs=[pl.BlockSpec((tb, GLA_DK), lambda i, h: (i, kb + h)),
                  pl.BlockSpec((tb, GLA_DV), lambda i, h: (i, vb + h)),
                  pl.BlockSpec((tb, GLA_DK), lambda i, h: (i, qb + h)),
                  pl.BlockSpec((tb, LANE), lambda i, h: (i, ab)),
                  pl.BlockSpec((LANE, GLA_DK), lambda i, h: (0, h)),
                  pl.BlockSpec((1, GLA_DK), lambda i, h: (0, h)),
                  pl.BlockSpec(e.shape, lambda i, h: (0, 0)),
                  pl.BlockSpec(msk.shape, lambda i, h: (0, 0, 0))],
        out_specs=[pl.BlockSpec((tb, GLA_DV), lambda i, h: (i, h)),
                   pl.BlockSpec((tb, GLA_DK), lambda i, h: (i, h)),
                   pl.BlockSpec((tb, GLA_DK), lambda i, h: (i, h)),
                   pl.BlockSpec((nchunk, GLA_DK), lambda i, h: (i, h))],
        compiler_params=_cp("parallel", "parallel"),
        name="gla_intra_bwd" if reverse else "gla_intra_fwd",
    )(px, px, px, px, wa_pad, ba, e, msk)


def _gla_scan_kernel(qd_ref, kd_ref, v_ref, dl_ref, oi_ref, s0_ref, o_ref, sf_ref, s_ref, *, nchunk, reverse):
    i = pl.program_id(1)
    n = GLA_CHUNK

    @pl.when(i == 0)
    def _():
        s_ref[...] = s0_ref[...]

    def chunk(cc, carry):
        c = (nchunk - 1 - cc) if reverse else cc
        rows = pl.ds(pl.multiple_of(c * n, n), n)
        st = s_ref[...]
        o_ref[rows, :] = oi_ref[rows, :] + _dot_nt(qd_ref[rows, :], st.astype(BF16))
        s_ref[...] = st * dl_ref[pl.ds(c, 1), :] + _dot_tn(v_ref[rows, :], kd_ref[rows, :])
        return carry

    lax.fori_loop(0, nchunk, chunk, 0)

    @pl.when(i == pl.num_programs(1) - 1)
    def _():
        sf_ref[...] = s_ref[...]


def _gla_scan(px, vb, qd, kd, dl, oi, s0, reverse):
    m = px.shape[0]
    tb, nchunk = _gla_blocks(m)
    nb = m // tb
    blk = (lambda i: nb - 1 - i) if reverse else (lambda i: i)
    return pl.pallas_call(
        functools.partial(_gla_scan_kernel, nchunk=nchunk, reverse=reverse),
        out_shape=[jax.ShapeDtypeStruct((m, GLA_VAL), F32),
                   jax.ShapeDtypeStruct((GLA_HEADS, GLA_DV, GLA_DK), F32)],
        grid=(GLA_HEADS, nb),
        in_specs=[pl.BlockSpec((tb, GLA_DK), lambda h, i: (blk(i), h)),
                  pl.BlockSpec((tb, GLA_DK), lambda h, i: (blk(i), h)),
                  pl.BlockSpec((tb, GLA_DV), lambda h, i: (blk(i), vb + h)),
                  pl.BlockSpec((nchunk, GLA_DK), lambda h, i: (blk(i), h)),
                  pl.BlockSpec((tb, GLA_DV), lambda h, i: (blk(i), h)),
                  pl.BlockSpec((None, GLA_DV, GLA_DK), lambda h, i: (h, 0, 0))],
        out_specs=[pl.BlockSpec((tb, GLA_DV), lambda h, i: (blk(i), h)),
                   pl.BlockSpec((None, GLA_DV, GLA_DK), lambda h, i: (h, 0, 0))],
        scratch_shapes=[pltpu.VMEM((GLA_DV, GLA_DK), F32)],
        compiler_params=_cp("parallel", "arbitrary"),
        name="gla_scan_bwd" if reverse else "gla_scan_fwd",
    )(qd, kd, px, dl, oi, s0)


def _gla_gate_kernel(of_ref, ob_ref, g_ref, gain_ref, o_ref):
    gain = gain_ref[...]
    for h in range(GLA_HEADS):
        sl = slice(h * GLA_DV, (h + 1) * GLA_DV)
        o = _rms(of_ref[:, sl] + ob_ref[:, sl]) * gain
        g = g_ref[:, sl].astype(F32)
        o_ref[:, sl] = (o * (g * jax.nn.sigmoid(g))).astype(BF16)


def _gla_gate(o_f, o_b, px, g_blk, gain):
    m = o_f.shape[0]
    tm = min(256, m)
    return pl.pallas_call(
        _gla_gate_kernel,
        out_shape=jax.ShapeDtypeStruct((m, GLA_VAL), BF16),
        grid=(m // tm,),
        in_specs=[pl.BlockSpec((tm, GLA_VAL), lambda i: (i, 0)), pl.BlockSpec((tm, GLA_VAL), lambda i: (i, 0)),
                  pl.BlockSpec((tm, GLA_VAL), lambda i: (i, g_blk)), pl.BlockSpec((1, GLA_DV), lambda i: (0, 0))],
        out_specs=pl.BlockSpec((tm, GLA_VAL), lambda i: (i, 0)),
        compiler_params=_cp("parallel"),
        name="gla_gate",
    )(o_f, o_b, px, gain)


def _pad_cols(w, mult):
    pad = (-w.shape[1]) % mult
    return jnp.pad(w, ((0, 0), (0, pad))) if pad else w


def _row(v, width=None):
    v = v.reshape(1, -1).astype(F32)
    return _pad_cols(v, width) if width else v


MM_TN = 512

AB_CKV_BLK = MLA_Q_RANK // MLA_KV_RANK
AB_CONV_COL = MLA_Q_RANK + MLA_KV_RANK
AB_KR_BLK = (AB_CONV_COL + 3 * CONV_WIDTH) // LANE
GLA_K_BLK = 0
GLA_V_BLK = GLA_KEY // GLA_DV
GLA_Q_BLK = (GLA_KEY + GLA_VAL) // GLA_DK
GLA_G_BLK = (2 * GLA_KEY + GLA_VAL) // GLA_VAL
GLA_A_BLK = (2 * GLA_KEY + 2 * GLA_VAL) // LANE


def _ab_w_in(w):
    ckv = w[:, :MLA_KV_RANK]
    kr = w[:, MLA_KV_RANK:MLA_KV_RANK + MLA_ROPE]
    cq = w[:, MLA_KV_RANK + MLA_ROPE:MLA_KV_RANK + MLA_ROPE + MLA_Q_RANK]
    conv = w[:, MLA_KV_RANK + MLA_ROPE + MLA_Q_RANK:]
    return _pad_cols(jnp.concatenate([cq, ckv, conv, kr], axis=1), MM_TN).astype(BF16)


def _ab_w_uq(w):
    w = w.reshape(MLA_Q_RANK, MLA_HEADS, MLA_QK)
    w = jnp.pad(w, ((0, 0), (0, 0), (0, MLA_HEAD_PAD - MLA_QK)))
    return w.reshape(MLA_Q_RANK, MLA_HEADS * MLA_HEAD_PAD).astype(BF16)


def _gla_w_in(w):
    o1, o2 = GLA_KEY, GLA_KEY + GLA_VAL
    o3 = o2 + 2 * GLA_RANK
    k, v, a, q, g = w[:, :o1], w[:, o1:o2], w[:, o2:o3], w[:, o3:o3 + GLA_KEY], w[:, o3 + GLA_KEY:]
    return _pad_cols(jnp.concatenate([k, v, q, g, a], axis=1), MM_TN).astype(BF16)


def _gla_wa_pad(w_a, offset):
    return jnp.zeros((LANE, GLA_KEY), F32).at[offset:offset + GLA_RANK].set(w_a.astype(F32))


def _rope_tables(pos_rows, pos_cols):
    half = MLA_ROPE // 4
    inv_freq = ROPE_BASE ** (-jnp.arange(half, dtype=F32) / half)
    ar = pos_rows.astype(F32)[:, None] * inv_freq[None, :]
    ac = pos_cols.astype(F32)[:, None] * inv_freq[None, :]
    zeros = jnp.zeros((ar.shape[0], LANE - MLA_ROPE), F32)
    cos = jnp.concatenate([jnp.cos(ar), jnp.cos(ar), jnp.cos(ac), jnp.cos(ac), zeros], axis=1)
    sin = jnp.concatenate([-jnp.sin(ar), jnp.sin(ar), -jnp.sin(ac), jnp.sin(ac), zeros], axis=1)
    return cos, sin


def _mixer_mla_conv(hx, hc, p, rope_x, rope_c, need_ctx):
    w_in, w_uq, w_ukv, w_out = p["w_in"], p["w_uq"], p["w_ukv"], p["w_out"]

    def project(h, rope):
        px = _mm([(h, w_in)], out_dtype=BF16, name="ab_in_proj")
        k, v = _kv_up(px, AB_CKV_BLK, AB_KR_BLK, p["kv_a_gain"], w_ukv, p["k_gain_n"], p["k_gain_r"], *rope)
        return px, k, v

    def branch_out(px, k, v, rope):
        q = _q_up(px, p["q_a_gain"], w_uq, p["q_gain_n"], p["q_gain_r"], *rope)
        o = _attention(q, k, v)
        y = _conv(px, AB_CONV_COL, p["conv_w"], p["conv_b"])
        return o, y

    px_c, k_c, v_c = project(hc, rope_c)
    px_x, k_x, v_x = project(hx, rope_x)
    k_all = jnp.concatenate([k_c, k_x], axis=0)
    v_all = jnp.concatenate([v_c, v_x], axis=0)
    out_x = branch_out(px_x, k_all, v_all, rope_x)
    out_c = branch_out(px_c, k_c, v_c, rope_c) if need_ctx else None
    half = MLA_HEADS * MLA_V
    return out_x, out_c, (w_out[:half], w_out[half:])


def _mixer_gla(hx, hc, p, need_ctx):
    cols = (GLA_K_BLK, GLA_V_BLK, GLA_Q_BLK, GLA_A_BLK)
    zero_state = jnp.zeros((GLA_HEADS, GLA_DV, GLA_DK), F32)

    def run(h, s_f, s_b):
        px = _mm([(h, p["w_in"])], out_dtype=BF16, name="gla_in_proj")
        outs, states = [], []
        for reverse, wa, ba, s0 in ((False, p["wa_f"], p["ba_f"], s_f), (True, p["wa_b"], p["ba_b"], s_b)):
            oi, qd, kd, dl = _gla_intra(px, cols, wa, ba, reverse)
            o, s_fin = _gla_scan(px, GLA_V_BLK, qd, kd, dl, oi, s0, reverse)
            outs.append(o)
            states.append(s_fin)
        return px, outs, states

    px_c, o_c, (s_f, s_b) = run(hc, zero_state, zero_state)
    px_x, o_x, _ = run(hx, s_f, s_b)
    out_x = _gla_gate(o_x[0], o_x[1], px_x, GLA_G_BLK, p["o_gain"])
    out_c = _gla_gate(o_c[0], o_c[1], px_c, GLA_G_BLK, p["o_gain"]) if need_ctx else None
    return out_x, out_c


def _moe(xres, gain, shift, scale, gate, router, wg, wu, wd):
    h, gates = _modulate(xres, gain, shift, scale, router=router)
    hid = _moe_up(h, wg, wu, gates)
    return _mm([(hid, wd)], out_dtype=F32, residual=(xres, gate), name="moe_down")


def kernel(x, c, ctx, c_ctx, ada_w, ada_b, norm_mix_gain, norm_ffn_gain, ab_w_in, ab_q_a_gain, ab_w_uq, ab_q_gain, ab_kv_a_gain, ab_w_ukv, ab_k_gain, ab_conv_w, ab_conv_b, ab_w_out, gla_w_in, gla_w_af, gla_b_af, gla_w_ab, gla_b_ab, gla_o_gain, gla_w_out, router_w, router_b, moe_w_gate, moe_w_up, moe_w_down):
    b, s, d = x.shape
    assert b == 1 and s % GRID_W == 0
    depth = ada_w.shape[0]
    xs, cx = x[0], ctx[0]
    n_ctx = cx.shape[0]

    mod = _adaln(c.reshape(d, 1), c_ctx.reshape(d, 1), ada_w, ada_b).reshape(depth, 2, 6, 1, d)
    tok = jnp.arange(s, dtype=jnp.int32)
    rope_x = _rope_tables(tok // GRID_W, tok % GRID_W)
    rope_c = (jnp.concatenate([jnp.ones((n_ctx, MLA_ROPE), F32), jnp.zeros((n_ctx, LANE - MLA_ROPE), F32)], axis=1),
              jnp.zeros((n_ctx, LANE), F32))
    router = (_pad_cols(router_w.astype(F32), LANE), router_b.reshape(N_EXPERTS, 1).astype(F32))

    for i in range(depth):
        last = i == depth - 1
        j = i // 2
        mx, mc = mod[i, 0], mod[i, 1]
        gain_mix = _row(norm_mix_gain[i])
        hx = _modulate(xs, gain_mix, mx[0], mx[1])
        hc = _modulate(cx, gain_mix, mc[0], mc[1])
        if i % 2 == 0:
            p = dict(
                w_in=_ab_w_in(ab_w_in[j]), w_uq=_ab_w_uq(ab_w_uq[j]), w_ukv=ab_w_ukv[j].astype(BF16),
                w_out=ab_w_out[j].astype(BF16),
                q_a_gain=_row(ab_q_a_gain[j]), kv_a_gain=_row(ab_kv_a_gain[j]),
                q_gain_n=_row(ab_q_gain[j][:MLA_NOPE]), q_gain_r=_row(ab_q_gain[j][MLA_NOPE:], LANE),
                k_gain_n=_row(ab_k_gain[j][:MLA_NOPE]), k_gain_r=_row(ab_k_gain[j][MLA_NOPE:], LANE),
                conv_w=ab_conv_w[j].astype(F32), conv_b=_row(ab_conv_b[j]))
            (o_x, y_x), out_c, (w_o, w_y) = _mixer_mla_conv(hx, hc, p, rope_x, rope_c, not last)
            xs = _mm([(o_x, w_o), (y_x, w_y)], out_dtype=F32, residual=(xs, mx[2]), name="ab_out_proj")
            if not last:
                cx = _mm([(out_c[0], w_o), (out_c[1], w_y)], out_dtype=F32, residual=(cx, mc[2]), name="ab_out_proj")
        else:
            p = dict(
                w_in=_gla_w_in(gla_w_in[j]),
                wa_f=_gla_wa_pad(gla_w_af[j], 0), wa_b=_gla_wa_pad(gla_w_ab[j], GLA_RANK),
                ba_f=_row(gla_b_af[j]), ba_b=_row(gla_b_ab[j]), o_gain=_row(gla_o_gain[j]))
            w_out = gla_w_out[j].astype(BF16)
            y_x, y_c = _mixer_gla(hx, hc, p, not last)
            xs = _mm([(y_x, w_out)], out_dtype=F32, residual=(xs, mx[2]), name="gla_out_proj")
            if not last:
                cx = _mm([(y_c, w_out)], out_dtype=F32, residual=(cx, mc[2]), name="gla_out_proj")
        gain_ffn = _row(norm_ffn_gain[i])
        wg, wu = moe_w_gate[i].astype(BF16), moe_w_up[i].astype(BF16)
        wd = moe_w_down[i].reshape(N_EXPERTS * D_FF, d).astype(BF16)
        xs = _moe(xs, gain_ffn, mx[3], mx[4], mx[5], router, wg, wu, wd)
        if not last:
            cx = _moe(cx, gain_ffn, mc[3], mc[4], mc[5], router, wg, wu, wd)
    return xs[None]
```

```python
import functools

import numpy as np
import jax
import jax.numpy as jnp
from jax import lax
from jax.experimental import pallas as pl
from jax.experimental.pallas import tpu as pltpu

F32 = jnp.float32
BF16 = jnp.bfloat16
HIGHEST = lax.Precision.HIGHEST

LANE = 128
SUBLANE = 8
VMEM_LIMIT_BYTES = 56 * 1024 * 1024

GRID_W = 64
EPS = 1e-6
MLA_HEADS = 16
MLA_Q_RANK = 1024
MLA_KV_RANK = 512
MLA_NOPE = 128
MLA_ROPE = 64
MLA_V = 128
MLA_QK = MLA_NOPE + MLA_ROPE
MLA_HEAD_PAD = 2 * LANE
ROPE_BASE = 10000.0
LOG2E = 1.4426950408889634
CONV_WIDTH = 2048
GLA_HEADS = 8
GLA_DK = 256
GLA_DV = 512
GLA_KEY = GLA_HEADS * GLA_DK
GLA_VAL = GLA_HEADS * GLA_DV
GLA_RANK = 16
GLA_TEMP = 16.0
GLA_CHUNK = 64
N_EXPERTS = 16
N_GROUPS = 4
GROUP_SIZE = N_EXPERTS // N_GROUPS
D_FF = 256


def _cp(*sem):
    return pltpu.CompilerParams(dimension_semantics=sem, vmem_limit_bytes=VMEM_LIMIT_BYTES)


def _dot(a, b):
    return jnp.dot(a, b, preferred_element_type=F32)


def _dot_nt(a, b):
    return lax.dot_general(a, b, (((1,), (1,)), ((), ())), preferred_element_type=F32)


def _dot_tn(a, b):
    return lax.dot_general(a, b, (((0,), (0,)), ((), ())), preferred_element_type=F32)


def _dot_f32(a, b):
    return jnp.dot(a, b, precision=HIGHEST, preferred_element_type=F32)


def _rms(v, n=None):
    ss = jnp.sum(v * v, axis=-1, keepdims=True)
    return v * lax.rsqrt(ss * (1.0 / (n if n is not None else v.shape[-1])) + EPS)


def _adaln_kernel(c_ref, cc_ref, w_ref, b_ref, o_ref, acc_ref):
    k = pl.program_id(2)

    @pl.when(k == 0)
    def _():
        acc_ref[...] = jnp.zeros_like(acc_ref)

    w = w_ref[...]
    tk, tn = w.shape
    for r, ref in enumerate((c_ref, cc_ref)):
        cv = ref[...]
        cs = cv * jax.nn.sigmoid(cv)
        acc_ref[r] += jnp.sum((w * cs).reshape(tk // SUBLANE, SUBLANE, tn), axis=0)

    @pl.when(k == pl.num_programs(2) - 1)
    def _():
        o_ref[...] = jnp.sum(acc_ref[...], axis=1) + b_ref[...]


def _adaln(c_col, cc_col, ada_w, ada_b):
    depth, d, n = ada_w.shape
    tk, tn = 512, 2048
    return pl.pallas_call(
        _adaln_kernel,
        out_shape=jax.ShapeDtypeStruct((depth, 2, n), F32),
        grid=(depth, n // tn, d // tk),
        in_specs=[
            pl.BlockSpec((tk, 1), lambda l, j, k: (k, 0)),
            pl.BlockSpec((tk, 1), lambda l, j, k: (k, 0)),
            pl.BlockSpec((None, tk, tn), lambda l, j, k: (l, k, j)),
            pl.BlockSpec((None, 1, tn), lambda l, j, k: (l, 0, j)),
        ],
        out_specs=pl.BlockSpec((None, 2, tn), lambda l, j, k: (l, 0, j)),
        scratch_shapes=[pltpu.VMEM((2, SUBLANE, tn), F32)],
        compiler_params=_cp("parallel", "parallel", "arbitrary"),
        name="adaln_mod",
    )(c_col, cc_col, ada_w, ada_b.reshape(depth, 1, n))


def _route_gates(logits, bias_col):
    tm = logits.shape[0]
    s = jax.nn.sigmoid(logits.T[:N_EXPERTS])
    sel = s + bias_col
    row = [sel[e:e + 1] for e in range(N_EXPERTS)]
    score = []
    for g in range(N_GROUPS):
        a, b, c2, d2 = row[GROUP_SIZE * g:GROUP_SIZE * (g + 1)]
        hi1, lo1 = jnp.maximum(a, b), jnp.minimum(a, b)
        hi2, lo2 = jnp.maximum(c2, d2), jnp.minimum(c2, d2)
        score.append(jnp.maximum(hi1, hi2) + jnp.maximum(jnp.minimum(hi1, hi2), jnp.maximum(lo1, lo2)))
    best, gidx = score[0], jnp.zeros((1, tm), jnp.int32)
    for g in range(1, N_GROUPS):
        upd = score[g] > best
        gidx = jnp.where(upd, g, gidx)
        best = jnp.where(upd, score[g], best)
    neg = jnp.full((1, tm), -jnp.inf, F32)
    cand = [jnp.where(gidx == (e // GROUP_SIZE), row[e], neg) for e in range(N_EXPERTS)]

    def argmax_first(vals):
        bv, bi = vals[0], jnp.zeros((1, tm), jnp.int32)
        for e in range(1, N_EXPERTS):
            upd = vals[e] > bv
            bi = jnp.where(upd, e, bi)
            bv = jnp.where(upd, vals[e], bv)
        return bi

    i1 = argmax_first(cand)
    i2 = argmax_first([jnp.where(i1 == e, neg, cand[e]) for e in range(N_EXPERTS)])
    eidx = lax.broadcasted_iota(jnp.int32, (N_EXPERTS, tm), 0)
    w1 = jnp.sum(jnp.where(eidx == i1, s, 0.0), axis=0, keepdims=True)
    w2 = jnp.sum(jnp.where(eidx == i2, s, 0.0), axis=0, keepdims=True)
    den = w1 + w2
    efull = lax.broadcasted_iota(jnp.int32, (LANE, tm), 0)
    gates_t = jnp.where(efull == i1, w1 / den, 0.0) + jnp.where(efull == i2, w2 / den, 0.0)
    return gates_t.T


def _modulate_kernel(x_ref, g_ref, sh_ref, sc_ref, *rest, route):
    x = x_ref[...]
    h = _rms(x) * g_ref[...] * (1.0 + sc_ref[...]) + sh_ref[...]
    if route:
        wr_ref, br_ref, h_ref, gates_ref = rest
        logits = _dot_f32(h, wr_ref[...])
        gates_ref[...] = _route_gates(logits, br_ref[...])
    else:
        (h_ref,) = rest
    h_ref[...] = h.astype(h_ref.dtype)


def _modulate(x, gain, shift, scale, router=None):
    m, d = x.shape
    tm = min(256, m)
    vec = pl.BlockSpec((1, d), lambda i: (0, 0))
    in_specs = [pl.BlockSpec((tm, d), lambda i: (i, 0)), vec, vec, vec]
    args = [x, gain, shift, scale]
    out_shape = [jax.ShapeDtypeStruct((m, d), BF16)]
    out_specs = [pl.BlockSpec((tm, d), lambda i: (i, 0))]
    if router is not None:
        wr, br = router
        in_specs += [pl.BlockSpec((d, LANE), lambda i: (0, 0)), pl.BlockSpec((N_EXPERTS, 1), lambda i: (0, 0))]
        args += [wr, br]
        out_shape.append(jax.ShapeDtypeStruct((m, LANE), F32))
        out_specs.append(pl.BlockSpec((tm, LANE), lambda i: (i, 0)))
    out = pl.pallas_call(
        functools.partial(_modulate_kernel, route=router is not None),
        out_shape=out_shape, grid=(m // tm,), in_specs=in_specs, out_specs=out_specs,
        compiler_params=_cp("parallel"),
        name="modulate_route" if router is not None else "modulate",
    )(*args)
    return out if router is not None else out[0]


def _mm_kernel(*refs, n_pairs, residual):
    a_refs, w_refs = refs[:n_pairs], refs[n_pairs:2 * n_pairs]
    o_ref = refs[-1]
    acc = _dot(a_refs[0][...], w_refs[0][...])
    for a_ref, w_ref in zip(a_refs[1:], w_refs[1:]):
        acc += _dot(a_ref[...], w_ref[...])
    if residual:
        x_ref, g_ref = refs[2 * n_pairs:2 * n_pairs + 2]
        acc = x_ref[...] + g_ref[...] * acc
    o_ref[...] = acc.astype(o_ref.dtype)


def _mm(pairs, *, out_dtype, residual=None, tm=1024, tn=512, name="mm"):
    m = pairs[0][0].shape[0]
    n = pairs[0][1].shape[1]
    tm, tn = min(tm, m), min(tn, n)
    assert m % tm == 0 and n % tn == 0, (m, n, tm, tn)
    in_specs = [pl.BlockSpec((tm, a.shape[1]), lambda i, j: (i, 0)) for a, _ in pairs]
    in_specs += [pl.BlockSpec((w.shape[0], tn), lambda i, j: (0, j)) for _, w in pairs]
    args = [a for a, _ in pairs] + [w for _, w in pairs]
    if residual is not None:
        in_specs += [pl.BlockSpec((tm, tn), lambda i, j: (i, j)), pl.BlockSpec((1, tn), lambda i, j: (0, j))]
        args += list(residual)
    return pl.pallas_call(
        functools.partial(_mm_kernel, n_pairs=len(pairs), residual=residual is not None),
        out_shape=jax.ShapeDtypeStruct((m, n), out_dtype),
        grid=(m // tm, n // tn), in_specs=in_specs,
        out_specs=pl.BlockSpec((tm, tn), lambda i, j: (i, j)),
        compiler_params=_cp("parallel", "arbitrary"),
        name=name,
    )(*args)


def _rope_rotate(v, cos, sin_signed):
    lane = lax.broadcasted_iota(jnp.int32, (1, LANE), 1)
    first_half = (lane % 32) < 16
    partner = jnp.where(first_half, pltpu.roll(v, LANE - 16, axis=1), pltpu.roll(v, 16, axis=1))
    return v * cos + partner * sin_signed


def _q_up_kernel(cq_ref, ga_ref, w_ref, gn_ref, gr_ref, cos_ref, sin_ref, o_ref, *, scale):
    cqn = (_rms(cq_ref[...].astype(F32)) * ga_ref[...]).astype(BF16)
    q = _dot(cqn, w_ref[...])
    cos, sin = cos_ref[...], sin_ref[...]
    gn, gr = gn_ref[...] * scale, gr_ref[...] * scale
    for h in range(MLA_HEADS):
        o = h * MLA_HEAD_PAD
        o_ref[:, o:o + LANE] = (_rms(q[:, o:o + LANE]) * gn).astype(BF16)
        qr = _rms(q[:, o + LANE:o + 2 * LANE], MLA_ROPE) * gr
        o_ref[:, o + LANE:o + 2 * LANE] = _rope_rotate(qr, cos, sin).astype(BF16)


def _q_up(px, ga, w, gn, gr, cos, sin):
    m = px.shape[0]
    tm = min(512, m)
    n = MLA_HEADS * MLA_HEAD_PAD
    vec = lambda width: pl.BlockSpec((1, width), lambda i: (0, 0))
    return pl.pallas_call(
        functools.partial(_q_up_kernel, scale=MLA_QK ** -0.5 * LOG2E),
        out_shape=jax.ShapeDtypeStruct((m, n), BF16),
        grid=(m // tm,),
        in_specs=[pl.BlockSpec((tm, MLA_Q_RANK), lambda i: (i, 0)), vec(MLA_Q_RANK),
                  pl.BlockSpec((MLA_Q_RANK, n), lambda i: (0, 0)), vec(LANE), vec(LANE),
                  pl.BlockSpec((tm, LANE), lambda i: (i, 0)), pl.BlockSpec((tm, LANE), lambda i: (i, 0))],
        out_specs=pl.BlockSpec((tm, n), lambda i: (i, 0)),
        compiler_params=_cp("parallel"),
        name="mla_q_up",
    )(px, ga, w, gn, gr, cos, sin)


def _kv_up_kernel(ckv_ref, kr_ref, ga_ref, w_ref, gn_ref, gr_ref, cos_ref, sin_ref, k_ref, v_ref):
    ckvn = (_rms(ckv_ref[...].astype(F32)) * ga_ref[...]).astype(BF16)
    kv = _dot(ckvn, w_ref[...])
    kr = _rms(kr_ref[...].astype(F32), MLA_ROPE) * gr_ref[...]
    kr = _rope_rotate(kr, cos_ref[...], sin_ref[...]).astype(BF16)
    gn = gn_ref[...]
    for h in range(MLA_HEADS):
        o = h * MLA_HEAD_PAD
        k_ref[:, o:o + LANE] = (_rms(kv[:, o:o + LANE]) * gn).astype(BF16)
        k_ref[:, o + LANE:o + 2 * LANE] = kr
        v_ref[:, h * MLA_V:(h + 1) * MLA_V] = kv[:, o + LANE:o + 2 * LANE].astype(BF16)


def _kv_up(px, ckv_blk, kr_blk, ga, w, gn, gr, cos, sin):
    m = px.shape[0]
    tm = min(512, m)
    vec = lambda width: pl.BlockSpec((1, width), lambda i: (0, 0))
    return pl.pallas_call(
        _kv_up_kernel,
        out_shape=[jax.ShapeDtypeStruct((m, MLA_HEADS * MLA_HEAD_PAD), BF16),
                   jax.ShapeDtypeStruct((m, MLA_HEADS * MLA_V), BF16)],
        grid=(m // tm,),
        in_specs=[pl.BlockSpec((tm, MLA_KV_RANK), lambda i: (i, ckv_blk)),
                  pl.BlockSpec((tm, LANE), lambda i: (i, kr_blk)), vec(MLA_KV_RANK),
                  pl.BlockSpec((MLA_KV_RANK, MLA_HEADS * MLA_HEAD_PAD), lambda i: (0, 0)), vec(LANE), vec(LANE),
                  pl.BlockSpec((tm, LANE), lambda i: (i, 0)), pl.BlockSpec((tm, LANE), lambda i: (i, 0))],
        out_specs=[pl.BlockSpec((tm, MLA_HEADS * MLA_HEAD_PAD), lambda i: (i, 0)),
                   pl.BlockSpec((tm, MLA_HEADS * MLA_V), lambda i: (i, 0))],
        compiler_params=_cp("parallel"),
        name="mla_kv_up",
    )(px, px, ga, w, gn, gr, cos, sin)


def _attn_kernel(q_ref, k_ref, v_ref, o_ref, sa_ref, sb_ref, m_ref, l_ref, acc_ref, *, ck, n_chunks):
    m_ref[...] = jnp.full_like(m_ref, -jnp.inf)
    l_ref[...] = jnp.zeros_like(l_ref)
    acc_ref[...] = jnp.zeros_like(acc_ref)

    def rows(c):
        return pl.ds(pl.multiple_of(c * ck, ck), ck)

    def scores(c, s_ref):
        s_ref[...] = _dot_nt(q_ref[...], k_ref[rows(c), :])

    def update(c, s_ref):
        m_prev = m_ref[...]
        m_new = jnp.maximum(m_prev, jnp.max(s_ref[...], axis=-1, keepdims=True))
        alpha = jnp.exp2(m_prev - m_new)
        m_ref[...] = m_new
        lsum = alpha * l_ref[...]
        acc = alpha * acc_ref[...]
        for j in range(ck // ATTN_SUB):
            p = jnp.exp2(s_ref[:, j * ATTN_SUB:(j + 1) * ATTN_SUB] - m_new)
            for t in range(ATTN_SUB // LANE):
                lsum += p[:, t * LANE:(t + 1) * LANE]
            acc += _dot(p.astype(BF16), v_ref[pl.ds(pl.multiple_of(c * ck + j * ATTN_SUB, ATTN_SUB), ATTN_SUB), :])
        l_ref[...] = lsum
        acc_ref[...] = acc

    bufs = (sa_ref, sb_ref)
    scores(0, sa_ref)
    group = 4 if (n_chunks - 1) % 4 == 0 else 2
    n_groups = (n_chunks - 1) // group

    def body(t, carry):
        c = group * t
        for u in range(group):
            scores(c + u + 1, bufs[(u + 1) % 2])
            update(c + u, bufs[u % 2])
        return carry

    if n_groups:
        lax.fori_loop(0, n_groups, body, 0)
    done = group * n_groups
    for c in range(done, n_chunks):
        if c + 1 < n_chunks:
            scores(c + 1, bufs[(c + 1) % 2])
        update(c, bufs[c % 2])
    o_ref[...] = (acc_ref[...] / jnp.sum(l_ref[...], axis=-1, keepdims=True)).astype(o_ref.dtype)


ATTN_SUB = 2 * LANE


def _attn_tiles(tq_total, tk_total):
    tq = min(1024, tq_total)
    ck = tk_total
    for cand in (1280, 1024, 512, 256):
        if tk_total % cand == 0:
            ck = cand
            break
    return tq, ck


def _attention(q, k, v):
    tq_total, tk_total = q.shape[0], k.shape[0]
    tq, ck = _attn_tiles(tq_total, tk_total)
    return pl.pallas_call(
        functools.partial(_attn_kernel, ck=ck, n_chunks=tk_total // ck),
        out_shape=jax.ShapeDtypeStruct((tq_total, MLA_HEADS * MLA_V), BF16),
        grid=(MLA_HEADS, tq_total // tq),
        in_specs=[pl.BlockSpec((tq, MLA_HEAD_PAD), lambda h, i: (i, h)),
                  pl.BlockSpec((tk_total, MLA_HEAD_PAD), lambda h, i: (0, h), pipeline_mode=pl.Buffered(1)),
                  pl.BlockSpec((tk_total, MLA_V), lambda h, i: (0, h), pipeline_mode=pl.Buffered(1))],
        out_specs=pl.BlockSpec((tq, MLA_V), lambda h, i: (i, h)),
        scratch_shapes=[pltpu.VMEM((tq, ck), F32), pltpu.VMEM((tq, ck), F32),
                        pltpu.VMEM((tq, 1), F32), pltpu.VMEM((tq, LANE), F32), pltpu.VMEM((tq, MLA_V), F32)],
        compiler_params=_cp("parallel", "arbitrary"),
        name="mla_attention",
    )(q, k, v)


CONV_TILE = 512
HALO = 16


def _conv_kernel(gb_ref, gc_ref, u_ref, gcp_ref, up_ref, gcn_ref, un_ref, w_ref, b_ref, o_ref):
    i = pl.program_id(0)
    z = gc_ref[...].astype(F32) * u_ref[...].astype(F32)
    tm = z.shape[0]
    z_before = gcp_ref[HALO - 1:HALO, :].astype(F32) * up_ref[HALO - 1:HALO, :].astype(F32)
    z_after = gcn_ref[0:1, :].astype(F32) * un_ref[0:1, :].astype(F32)
    z_before = jnp.where(i > 0, z_before, 0.0)
    z_after = jnp.where(i < pl.num_programs(0) - 1, z_after, 0.0)
    rowid = lax.broadcasted_iota(jnp.int32, z.shape, 0)
    z_prev = jnp.where(rowid == 0, z_before, pltpu.roll(z, 1, axis=0))
    z_next = jnp.where(rowid == tm - 1, z_after, pltpu.roll(z, tm - 1, axis=0))
    w = w_ref[...]
    y = z_prev * w[0:1] + z * w[1:2] + z_next * w[2:3] + b_ref[...]
    o_ref[...] = (gb_ref[...].astype(F32) * y).astype(o_ref.dtype)


def _conv(px, col0, conv_w, conv_b):
    m = px.shape[0]
    tm = min(512, m)
    cb = CONV_WIDTH // CONV_TILE
    b0 = col0 // CONV_TILE
    nhalo = m // HALO
    main = lambda off: pl.BlockSpec((tm, CONV_TILE), lambda i, j: (i, b0 + off * cb + j))
    prev = lambda off: pl.BlockSpec(
        (HALO, CONV_TILE), lambda i, j: (jnp.maximum(i * (tm // HALO) - 1, 0), b0 + off * cb + j))
    nxt = lambda off: pl.BlockSpec(
        (HALO, CONV_TILE), lambda i, j: (jnp.minimum((i + 1) * (tm // HALO), nhalo - 1), b0 + off * cb + j))
    return pl.pallas_call(
        _conv_kernel,
        out_shape=jax.ShapeDtypeStruct((m, CONV_WIDTH), BF16),
        grid=(m // tm, cb),
        in_specs=[main(0), main(1), main(2), prev(1), prev(2), nxt(1), nxt(2),
                  pl.BlockSpec((3, CONV_TILE), lambda i, j: (0, j)),
                  pl.BlockSpec((1, CONV_TILE), lambda i, j: (0, j))],
        out_specs=pl.BlockSpec((tm, CONV_TILE), lambda i, j: (i, j)),
        compiler_params=_cp("parallel", "parallel"),
        name="gated_conv3",
    )(px, px, px, px, px, px, px, conv_w, conv_b)


def _moe_up_kernel(h_ref, wg_ref, wu_ref, gates_ref, o_ref):
    e = pl.program_id(1)
    h = h_ref[...]
    a = _dot(h, wg_ref[...])
    u = _dot(h, wu_ref[...])
    gates = gates_ref[...]
    lane = lax.broadcasted_iota(jnp.int32, gates.shape, 1)
    g = jnp.sum(jnp.where(lane == e, gates, 0.0), axis=-1, keepdims=True)
    o_ref[...] = (a * jax.nn.sigmoid(a) * u * g).astype(o_ref.dtype)


def _moe_up(h, wg, wu, gates):
    m, d = h.shape
    tm = min(1024, m)
    return pl.pallas_call(
        _moe_up_kernel,
        out_shape=jax.ShapeDtypeStruct((m, N_EXPERTS * D_FF), BF16),
        grid=(m // tm, N_EXPERTS),
        in_specs=[pl.BlockSpec((tm, d), lambda i, e: (i, 0)),
                  pl.BlockSpec((None, d, D_FF), lambda i, e: (e, 0, 0)),
                  pl.BlockSpec((None, d, D_FF), lambda i, e: (e, 0, 0)),
                  pl.BlockSpec((tm, LANE), lambda i, e: (i, 0))],
        out_specs=pl.BlockSpec((tm, D_FF), lambda i, e: (i, e)),
        compiler_params=_cp("parallel", "arbitrary"),
        name="moe_up",
    )(h, wg, wu, gates)


GLA_LEVELS = 6
GLA_MILD_DECAY = 50.0


def _gla_tables(reverse):
    n = GLA_CHUNK
    t = np.arange(n)[:, None]
    r = np.arange(n)[None, :]
    blocks = [r <= t, r > t]
    masks = []
    for lvl in range(GLA_LEVELS):
        bs = 1 << lvl
        blocks.append((r >= (t & ~(bs - 1))) & (r <= t))
        blocks.append((r > t) & (r <= (t | (bs - 1))))
        masks.append(((t >> (lvl + 1)) == (r >> (lvl + 1))) & (((t >> lvl) & 1) == 1) & (((r >> lvl) & 1) == 0))
    masks.append(t == r)
    masks.append(r <= t)
    e = np.stack(blocks).astype(np.float32)
    msk = np.stack(masks).astype(np.float32)
    if reverse:
        e, msk = e[:, ::-1, ::-1], msk[:, ::-1, ::-1]
    return jnp.asarray(e.reshape(-1, n)), jnp.asarray(msk)


def _gla_intra_kernel(k_ref, v_ref, q_ref, a_ref, wa_ref, ba_ref, e_ref, m_ref,
                      oi_ref, qd_ref, kd_ref, dl_ref, la_ref, tot_ref, *, nchunk, reverse):
    n = GLA_CHUNK
    scale = GLA_DK ** -0.5

    a = a_ref[...]
    z = _dot(a, wa_ref[0]) + _dot(a, wa_ref[1]) + _dot(a, wa_ref[2]) + ba_ref[...]
    la = (jnp.minimum(z, 0.0) - jnp.log1p(jnp.exp(-jnp.abs(z)))) * (1.0 / GLA_TEMP)
    la_ref[...] = la
    totals = jnp.sum(la.reshape(nchunk, n, GLA_DK), axis=1)
    tot_ref[...] = totals
    dl_ref[...] = jnp.exp(totals)
    mild = jnp.min(totals) > -GLA_MILD_DECAY
    rowid = lax.broadcasted_iota(jnp.int32, (n, GLA_DK), 0)

    def load(c):
        rows = pl.ds(pl.multiple_of(c * n, n), n)
        return rows, q_ref[rows, :].astype(F32) * scale, k_ref[rows, :].astype(F32)

    def mild_chunk(c, carry):
        rows, q, k = load(c)
        la_c = la_ref[rows, :]
        total = tot_ref[pl.ds(c, 1), :]
        cum = la_c
        shift = 1
        while shift < n:
            cum = cum + jnp.where(rowid >= shift, pltpu.roll(cum, shift, axis=0), 0.0)
            shift *= 2
        if reverse:
            cum = total - cum + la_c
        qd = (q * jnp.exp(cum)).astype(BF16)
        kinv = k * jnp.exp(-cum)
        qd_ref[rows, :] = qd
        kd_ref[rows, :] = (kinv * jnp.exp(total)).astype(BF16)
        att = m_ref[GLA_LEVELS + 1] * _dot_nt(qd, kinv.astype(BF16))
        oi_ref[rows, :] = _dot(att.astype(BF16), v_ref[rows, :])
        return carry

    def harsh_chunk(c, carry):
        rows, q, k = load(c)
        ex = _dot_f32(e_ref[...], la_ref[rows, :])
        qd_ref[rows, :] = (q * jnp.exp(ex[0:n])).astype(BF16)
        kd_ref[rows, :] = (k * jnp.exp(ex[n:2 * n])).astype(BF16)
        att = m_ref[GLA_LEVELS] * _dot_nt(q.astype(BF16), k.astype(BF16))
        for lvl in range(GLA_LEVELS):
            qs = (q * jnp.exp(ex[(2 + 2 * lvl) * n:(3 + 2 * lvl) * n])).astype(BF16)
            ks = (k * jnp.exp(ex[(3 + 2 * lvl) * n:(4 + 2 * lvl) * n])).astype(BF16)
            att += m_ref[lvl] * _dot_nt(qs, ks)
        oi_ref[rows, :] = _dot(att.astype(BF16), v_ref[rows, :])
        return carry

    @pl.when(mild)
    def _():
        lax.fori_loop(0, nchunk, mild_chunk, 0, unroll=2)

    @pl.when(jnp.logical_not(mild))
    def _():
        lax.fori_loop(0, nchunk, harsh_chunk, 0)


def _gla_blocks(m):
    tb = min(512, m)
    return tb, tb // GLA_CHUNK


def _gla_intra(px, cols, wa_pad, ba, reverse):
    m = px.shape[0]
    tb, nchunk = _gla_blocks(m)
    kb, vb, qb, ab = cols
    e, msk = _gla_tables(reverse)
    return pl.pallas_call(
        functools.partial(_gla_intra_kernel, nchunk=nchunk, reverse=reverse),
        out_shape=[jax.ShapeDtypeStruct((m, GLA_VAL), F32), jax.ShapeDtypeStruct((m, GLA_KEY), BF16),
                   jax.ShapeDtypeStruct((m, GLA_KEY), BF16), jax.ShapeDtypeStruct((m // GLA_CHUNK, GLA_KEY), F32)],
        grid=(m // tb, GLA_HEADS),
        in_specs=[pl.BlockSpec((tb, GLA_DK), lambda i, h: (i, kb + h)),
                  pl.BlockSpec((tb, GLA_DV), lambda i, h: (i, vb + h)),
                  pl.BlockSpec((tb, GLA_DK), lambda i, h: (i, qb + h)),
                  pl.BlockSpec((tb, LANE), lambda i, h: (i, ab)),
                  pl.BlockSpec((3, LANE, GLA_DK), lambda i, h: (0, 0, h)),
                  pl.BlockSpec((1, GLA_DK), lambda i, h: (0, h)),
                  pl.BlockSpec(e.shape, lambda i, h: (0, 0)),
                  pl.BlockSpec(msk.shape, lambda i, h: (0, 0, 0))],
        out_specs=[pl.BlockSpec((tb, GLA_DV), lambda i, h: (i, h)),
                   pl.BlockSpec((tb, GLA_DK), lambda i, h: (i, h)),
                   pl.BlockSpec((tb, GLA_DK), lambda i, h: (i, h)),
                   pl.BlockSpec((nchunk, GLA_DK), lambda i, h: (i, h))],
        scratch_shapes=[pltpu.VMEM((tb, GLA_DK), F32), pltpu.VMEM((nchunk, GLA_DK), F32)],
        compiler_params=_cp("parallel", "parallel"),
        name="gla_intra_bwd" if reverse else "gla_intra_fwd",
    )(px, px, px, px, wa_pad, ba, e, msk)


def _gla_scan_kernel(qd_ref, kd_ref, v_ref, dl_ref, oi_ref, s0_ref, o_ref, sf_ref, s_ref, *, nchunk, reverse):
    i = pl.program_id(1)
    n = GLA_CHUNK

    @pl.when(i == 0)
    def _():
        s_ref[...] = s0_ref[...]

    def chunk(cc, carry):
        c = (nchunk - 1 - cc) if reverse else cc
        rows = pl.ds(pl.multiple_of(c * n, n), n)
        st = s_ref[...]
        o_ref[rows, :] = oi_ref[rows, :] + _dot_nt(qd_ref[rows, :], st.astype(BF16))
        s_ref[...] = st * dl_ref[pl.ds(c, 1), :] + _dot_tn(v_ref[rows, :], kd_ref[rows, :])
        return carry

    lax.fori_loop(0, nchunk, chunk, 0, unroll=True)

    @pl.when(i == pl.num_programs(1) - 1)
    def _():
        sf_ref[...] = s_ref[...]


def _gla_scan(px, vb, qd, kd, dl, oi, s0, reverse):
    m = px.shape[0]
    tb, nchunk = _gla_blocks(m)
    nb = m // tb
    blk = (lambda i: nb - 1 - i) if reverse else (lambda i: i)
    return pl.pallas_call(
        functools.partial(_gla_scan_kernel, nchunk=nchunk, reverse=reverse),
        out_shape=[jax.ShapeDtypeStruct((m, GLA_VAL), F32),
                   jax.ShapeDtypeStruct((GLA_HEADS, GLA_DV, GLA_DK), F32)],
        grid=(GLA_HEADS, nb),
        in_specs=[pl.BlockSpec((tb, GLA_DK), lambda h, i: (blk(i), h)),
                  pl.BlockSpec((tb, GLA_DK), lambda h, i: (blk(i), h)),
                  pl.BlockSpec((tb, GLA_DV), lambda h, i: (blk(i), vb + h)),
                  pl.BlockSpec((nchunk, GLA_DK), lambda h, i: (blk(i), h)),
                  pl.BlockSpec((tb, GLA_DV), lambda h, i: (blk(i), h)),
                  pl.BlockSpec((None, GLA_DV, GLA_DK), lambda h, i: (h, 0, 0))],
        out_specs=[pl.BlockSpec((tb, GLA_DV), lambda h, i: (blk(i), h)),
                   pl.BlockSpec((None, GLA_DV, GLA_DK), lambda h, i: (h, 0, 0))],
        scratch_shapes=[pltpu.VMEM((GLA_DV, GLA_DK), F32)],
        compiler_params=_cp("parallel", "arbitrary"),
        name="gla_scan_bwd" if reverse else "gla_scan_fwd",
    )(qd, kd, px, dl, oi, s0)


def _gla_gate_kernel(of_ref, ob_ref, g_ref, gain_ref, o_ref):
    gain = gain_ref[...]
    for h in range(GLA_HEADS):
        sl = slice(h * GLA_DV, (h + 1) * GLA_DV)
        o = _rms(of_ref[:, sl] + ob_ref[:, sl]) * gain
        g = g_ref[:, sl].astype(F32)
        o_ref[:, sl] = (o * (g * jax.nn.sigmoid(g))).astype(BF16)


def _gla_gate(o_f, o_b, px, g_blk, gain):
    m = o_f.shape[0]
    tm = min(256, m)
    return pl.pallas_call(
        _gla_gate_kernel,
        out_shape=jax.ShapeDtypeStruct((m, GLA_VAL), BF16),
        grid=(m // tm,),
        in_specs=[pl.BlockSpec((tm, GLA_VAL), lambda i: (i, 0)), pl.BlockSpec((tm, GLA_VAL), lambda i: (i, 0)),
                  pl.BlockSpec((tm, GLA_VAL), lambda i: (i, g_blk)), pl.BlockSpec((1, GLA_DV), lambda i: (0, 0))],
        out_specs=pl.BlockSpec((tm, GLA_VAL), lambda i: (i, 0)),
        compiler_params=_cp("parallel"),
        name="gla_gate",
    )(o_f, o_b, px, gain)


def _pad_cols(w, mult):
    pad = (-w.shape[1]) % mult
    return jnp.pad(w, ((0, 0), (0, pad))) if pad else w


def _row(v, width=None):
    v = v.reshape(1, -1).astype(F32)
    return _pad_cols(v, width) if width else v


MM_TN = 512

AB_CKV_BLK = MLA_Q_RANK // MLA_KV_RANK
AB_CONV_COL = MLA_Q_RANK + MLA_KV_RANK
AB_KR_BLK = (AB_CONV_COL + 3 * CONV_WIDTH) // LANE
GLA_K_BLK = 0
GLA_V_BLK = GLA_KEY // GLA_DV
GLA_Q_BLK = (GLA_KEY + GLA_VAL) // GLA_DK
GLA_G_BLK = (2 * GLA_KEY + GLA_VAL) // GLA_VAL
GLA_A_BLK = (2 * GLA_KEY + 2 * GLA_VAL) // LANE


def _ab_w_in(w):
    ckv = w[:, :MLA_KV_RANK]
    kr = w[:, MLA_KV_RANK:MLA_KV_RANK + MLA_ROPE]
    cq = w[:, MLA_KV_RANK + MLA_ROPE:MLA_KV_RANK + MLA_ROPE + MLA_Q_RANK]
    conv = w[:, MLA_KV_RANK + MLA_ROPE + MLA_Q_RANK:]
    return _pad_cols(jnp.concatenate([cq, ckv, conv, kr], axis=1), MM_TN).astype(BF16)


def _ab_w_uq(w):
    w = w.reshape(MLA_Q_RANK, MLA_HEADS, MLA_QK)
    w = jnp.pad(w, ((0, 0), (0, 0), (0, MLA_HEAD_PAD - MLA_QK)))
    return w.reshape(MLA_Q_RANK, MLA_HEADS * MLA_HEAD_PAD).astype(BF16)


def _gla_w_in(w):
    o1, o2 = GLA_KEY, GLA_KEY + GLA_VAL
    o3 = o2 + 2 * GLA_RANK
    k, v, a, q, g = w[:, :o1], w[:, o1:o2], w[:, o2:o3], w[:, o3:o3 + GLA_KEY], w[:, o3 + GLA_KEY:]
    return _pad_cols(jnp.concatenate([k, v, q, g, a], axis=1), MM_TN).astype(BF16)


def _gla_wa_pad(w_a, offset):
    w = jnp.zeros((LANE, GLA_KEY), F32).at[offset:offset + GLA_RANK].set(w_a.astype(F32))
    hi = w.astype(BF16)
    r1 = w - hi.astype(F32)
    mid = r1.astype(BF16)
    lo = (r1 - mid.astype(F32)).astype(BF16)
    return jnp.stack([hi, mid, lo])


def _rope_tables(pos_rows, pos_cols):
    half = MLA_ROPE // 4
    inv_freq = ROPE_BASE ** (-jnp.arange(half, dtype=F32) / half)
    ar = pos_rows.astype(F32)[:, None] * inv_freq[None, :]
    ac = pos_cols.astype(F32)[:, None] * inv_freq[None, :]
    zeros = jnp.zeros((ar.shape[0], LANE - MLA_ROPE), F32)
    cos = jnp.concatenate([jnp.cos(ar), jnp.cos(ar), jnp.cos(ac), jnp.cos(ac), zeros], axis=1)
    sin = jnp.concatenate([-jnp.sin(ar), jnp.sin(ar), -jnp.sin(ac), jnp.sin(ac), zeros], axis=1)
    return cos, sin


def _mixer_mla_conv(hx, hc, p, rope_x, rope_c, need_ctx):
    w_in, w_uq, w_ukv, w_out = p["w_in"], p["w_uq"], p["w_ukv"], p["w_out"]

    def project(h, rope):
        px = _mm([(h, w_in)], out_dtype=BF16, name="ab_in_proj")
        k, v = _kv_up(px, AB_CKV_BLK, AB_KR_BLK, p["kv_a_gain"], w_ukv, p["k_gain_n"], p["k_gain_r"], *rope)
        return px, k, v

    def branch_out(px, k, v, rope):
        q = _q_up(px, p["q_a_gain"], w_uq, p["q_gain_n"], p["q_gain_r"], *rope)
        o = _attention(q, k, v)
        y = _conv(px, AB_CONV_COL, p["conv_w"], p["conv_b"])
        return o, y

    px_c, k_c, v_c = project(hc, rope_c)
    px_x, k_x, v_x = project(hx, rope_x)
    k_all = jnp.concatenate([k_c, k_x], axis=0)
    v_all = jnp.concatenate([v_c, v_x], axis=0)
    out_x = branch_out(px_x, k_all, v_all, rope_x)
    out_c = branch_out(px_c, k_c, v_c, rope_c) if need_ctx else None
    half = MLA_HEADS * MLA_V
    return out_x, out_c, (w_out[:half], w_out[half:])


def _mixer_gla(hx, hc, p, need_ctx):
    cols = (GLA_K_BLK, GLA_V_BLK, GLA_Q_BLK, GLA_A_BLK)
    zero_state = jnp.zeros((GLA_HEADS, GLA_DV, GLA_DK), F32)

    def run(h, s_f, s_b):
        px = _mm([(h, p["w_in"])], out_dtype=BF16, name="gla_in_proj")
        outs, states = [], []
        for reverse, wa, ba, s0 in ((False, p["wa_f"], p["ba_f"], s_f), (True, p["wa_b"], p["ba_b"], s_b)):
            oi, qd, kd, dl = _gla_intra(px, cols, wa, ba, reverse)
            o, s_fin = _gla_scan(px, GLA_V_BLK, qd, kd, dl, oi, s0, reverse)
            outs.append(o)
            states.append(s_fin)
        return px, outs, states

    px_c, o_c, (s_f, s_b) = run(hc, zero_state, zero_state)
    px_x, o_x, _ = run(hx, s_f, s_b)
    out_x = _gla_gate(o_x[0], o_x[1], px_x, GLA_G_BLK, p["o_gain"])
    out_c = _gla_gate(o_c[0], o_c[1], px_c, GLA_G_BLK, p["o_gain"]) if need_ctx else None
    return out_x, out_c


def _moe(xres, gain, shift, scale, gate, router, wg, wu, wd):
    h, gates = _modulate(xres, gain, shift, scale, router=router)
    hid = _moe_up(h, wg, wu, gates)
    return _mm([(hid, wd)], out_dtype=F32, residual=(xres, gate), name="moe_down")


def kernel(x, c, ctx, c_ctx, ada_w, ada_b, norm_mix_gain, norm_ffn_gain, ab_w_in, ab_q_a_gain, ab_w_uq, ab_q_gain, ab_kv_a_gain, ab_w_ukv, ab_k_gain, ab_conv_w, ab_conv_b, ab_w_out, gla_w_in, gla_w_af, gla_b_af, gla_w_ab, gla_b_ab, gla_o_gain, gla_w_out, router_w, router_b, moe_w_gate, moe_w_up, moe_w_down):
    b, s, d = x.shape
    assert b == 1 and s % GRID_W == 0
    depth = ada_w.shape[0]
    xs, cx = x[0], ctx[0]
    n_ctx = cx.shape[0]

    mod = _adaln(c.reshape(d, 1), c_ctx.reshape(d, 1), ada_w, ada_b).reshape(depth, 2, 6, 1, d)
    tok = jnp.arange(s, dtype=jnp.int32)
    rope_x = _rope_tables(tok // GRID_W, tok % GRID_W)
    rope_c = (jnp.concatenate([jnp.ones((n_ctx, MLA_ROPE), F32), jnp.zeros((n_ctx, LANE - MLA_ROPE), F32)], axis=1),
              jnp.zeros((n_ctx, LANE), F32))
    router = (_pad_cols(router_w.astype(F32), LANE), router_b.reshape(N_EXPERTS, 1).astype(F32))

    for i in range(depth):
        last = i == depth - 1
        j = i // 2
        mx, mc = mod[i, 0], mod[i, 1]
        gain_mix = _row(norm_mix_gain[i])
        hx = _modulate(xs, gain_mix, mx[0], mx[1])
        hc = _modulate(cx, gain_mix, mc[0], mc[1])
        if i % 2 == 0:
            p = dict(
                w_in=_ab_w_in(ab_w_in[j]), w_uq=_ab_w_uq(ab_w_uq[j]), w_ukv=ab_w_ukv[j].astype(BF16),
                w_out=ab_w_out[j].astype(BF16),
                q_a_gain=_row(ab_q_a_gain[j]), kv_a_gain=_row(ab_kv_a_gain[j]),
                q_gain_n=_row(ab_q_gain[j][:MLA_NOPE]), q_gain_r=_row(ab_q_gain[j][MLA_NOPE:], LANE),
                k_gain_n=_row(ab_k_gain[j][:MLA_NOPE]), k_gain_r=_row(ab_k_gain[j][MLA_NOPE:], LANE),
                conv_w=ab_conv_w[j].astype(F32), conv_b=_row(ab_conv_b[j]))
            (o_x, y_x), out_c, (w_o, w_y) = _mixer_mla_conv(hx, hc, p, rope_x, rope_c, not last)
            xs = _mm([(o_x, w_o), (y_x, w_y)], out_dtype=F32, residual=(xs, mx[2]), name="ab_out_proj")
            if not last:
                cx = _mm([(out_c[0], w_o), (out_c[1], w_y)], out_dtype=F32, residual=(cx, mc[2]), name="ab_out_proj")
        else:
            p = dict(
                w_in=_gla_w_in(gla_w_in[j]),
                wa_f=_gla_wa_pad(gla_w_af[j], 0), wa_b=_gla_wa_pad(gla_w_ab[j], GLA_RANK),
                ba_f=_row(gla_b_af[j]), ba_b=_row(gla_b_ab[j]), o_gain=_row(gla_o_gain[j]))
            w_out = gla_w_out[j].astype(BF16)
            y_x, y_c = _mixer_gla(hx, hc, p, not last)
            xs = _mm([(y_x, w_out)], out_dtype=F32, residual=(xs, mx[2]), name="gla_out_proj")
            if not last:
                cx = _mm([(y_c, w_out)], out_dtype=F32, residual=(cx, mc[2]), name="gla_out_proj")
        gain_ffn = _row(norm_ffn_gain[i])
        wg, wu = moe_w_gate[i].astype(BF16), moe_w_up[i].astype(BF16)
        wd = moe_w_down[i].reshape(N_EXPERTS * D_FF, d).astype(BF16)
        xs = _moe(xs, gain_ffn, mx[3], mx[4], mx[5], router, wg, wu, wd)
        if not last:
            cx = _moe(cx, gain_ffn, mc[3], mc[4], mc[5], router, wg, wu, wd)
    return xs[None]
```

```python
import functools

import numpy as np
import jax
import jax.numpy as jnp
from jax import lax
from jax.experimental import pallas as pl
from jax.experimental.pallas import tpu as pltpu

F32 = jnp.float32
BF16 = jnp.bfloat16
HIGHEST = lax.Precision.HIGHEST

LANE = 128
SUBLANE = 8
VMEM_LIMIT_BYTES = 56 * 1024 * 1024

GRID_W = 64
EPS = 1e-6
MLA_HEADS = 16
MLA_Q_RANK = 1024
MLA_KV_RANK = 512
MLA_NOPE = 128
MLA_ROPE = 64
MLA_V = 128
MLA_QK = MLA_NOPE + MLA_ROPE
MLA_HEAD_PAD = 2 * LANE
ROPE_BASE = 10000.0
LOG2E = 1.4426950408889634
CONV_WIDTH = 2048
GLA_HEADS = 8
GLA_DK = 256
GLA_DV = 512
GLA_KEY = GLA_HEADS * GLA_DK
GLA_VAL = GLA_HEADS * GLA_DV
GLA_RANK = 16
GLA_TEMP = 16.0
GLA_CHUNK = 64
N_EXPERTS = 16
N_GROUPS = 4
GROUP_SIZE = N_EXPERTS // N_GROUPS
D_FF = 256


def _cp(*sem):
    return pltpu.CompilerParams(dimension_semantics=sem, vmem_limit_bytes=VMEM_LIMIT_BYTES)


def _dot(a, b):
    return jnp.dot(a, b, preferred_element_type=F32)


def _dot_nt(a, b):
    return lax.dot_general(a, b, (((1,), (1,)), ((), ())), preferred_element_type=F32)


def _dot_tn(a, b):
    return lax.dot_general(a, b, (((0,), (0,)), ((), ())), preferred_element_type=F32)


def _dot_f32(a, b):
    return jnp.dot(a, b, precision=HIGHEST, preferred_element_type=F32)


def _rms(v, n=None):
    ss = jnp.sum(v * v, axis=-1, keepdims=True)
    return v * lax.rsqrt(ss * (1.0 / (n if n is not None else v.shape[-1])) + EPS)


def _adaln_kernel(c_ref, cc_ref, w_ref, b_ref, o_ref, acc_ref):
    k = pl.program_id(2)

    @pl.when(k == 0)
    def _():
        acc_ref[...] = jnp.zeros_like(acc_ref)

    w = w_ref[...]
    tk, tn = w.shape
    for r, ref in enumerate((c_ref, cc_ref)):
        cv = ref[...]
        cs = cv * jax.nn.sigmoid(cv)
        acc_ref[r] += jnp.sum((w * cs).reshape(tk // SUBLANE, SUBLANE, tn), axis=0)

    @pl.when(k == pl.num_programs(2) - 1)
    def _():
        o_ref[...] = jnp.sum(acc_ref[...], axis=1) + b_ref[...]


def _adaln(c_col, cc_col, ada_w, ada_b):
    depth, d, n = ada_w.shape
    tk, tn = 512, 2048
    return pl.pallas_call(
        _adaln_kernel,
        out_shape=jax.ShapeDtypeStruct((depth, 2, n), F32),
        grid=(depth, n // tn, d // tk),
        in_specs=[
            pl.BlockSpec((tk, 1), lambda l, j, k: (k, 0)),
            pl.BlockSpec((tk, 1), lambda l, j, k: (k, 0)),
            pl.BlockSpec((None, tk, tn), lambda l, j, k: (l, k, j)),
            pl.BlockSpec((None, 1, tn), lambda l, j, k: (l, 0, j)),
        ],
        out_specs=pl.BlockSpec((None, 2, tn), lambda l, j, k: (l, 0, j)),
        scratch_shapes=[pltpu.VMEM((2, SUBLANE, tn), F32)],
        compiler_params=_cp("parallel", "parallel", "arbitrary"),
        name="adaln_mod",
    )(c_col, cc_col, ada_w, ada_b.reshape(depth, 1, n))


def _route_topk(logits, bias_col):
    tm = logits.shape[0]
    s = jax.nn.sigmoid(logits.T[:N_EXPERTS])
    sel = s + bias_col
    row = [sel[e:e + 1] for e in range(N_EXPERTS)]
    score = []
    for g in range(N_GROUPS):
        a, b, c2, d2 = row[GROUP_SIZE * g:GROUP_SIZE * (g + 1)]
        hi1, lo1 = jnp.maximum(a, b), jnp.minimum(a, b)
        hi2, lo2 = jnp.maximum(c2, d2), jnp.minimum(c2, d2)
        score.append(jnp.maximum(hi1, hi2) + jnp.maximum(jnp.minimum(hi1, hi2), jnp.maximum(lo1, lo2)))
    best, gidx = score[0], jnp.zeros((1, tm), jnp.int32)
    for g in range(1, N_GROUPS):
        upd = score[g] > best
        gidx = jnp.where(upd, g, gidx)
        best = jnp.where(upd, score[g], best)
    neg = jnp.full((1, tm), -jnp.inf, F32)
    cand = [jnp.where(gidx == (e // GROUP_SIZE), row[e], neg) for e in range(N_EXPERTS)]

    def argmax_first(vals):
        bv, bi = vals[0], jnp.zeros((1, tm), jnp.int32)
        for e in range(1, N_EXPERTS):
            upd = vals[e] > bv
            bi = jnp.where(upd, e, bi)
            bv = jnp.where(upd, vals[e], bv)
        return bi

    i1 = argmax_first(cand)
    i2 = argmax_first([jnp.where(i1 == e, neg, cand[e]) for e in range(N_EXPERTS)])
    eidx = lax.broadcasted_iota(jnp.int32, (N_EXPERTS, tm), 0)
    w1 = jnp.sum(jnp.where(eidx == i1, s, 0.0), axis=0, keepdims=True)
    w2 = jnp.sum(jnp.where(eidx == i2, s, 0.0), axis=0, keepdims=True)
    den = w1 + w2
    return i1, i2, w1 / den, w2 / den


def _dense_gates(i1, i2, w1, w2):
    efull = lax.broadcasted_iota(jnp.int32, (LANE, i1.shape[1]), 0)
    return (jnp.where(efull == i1, w1, 0.0) + jnp.where(efull == i2, w2, 0.0)).T


INFO_E1, INFO_E2, INFO_W1, INFO_W2, INFO_R1, INFO_R2 = range(6)


def _routing_record(i1, i2, w1, w2, carry_ref):
    tm = i1.shape[1]
    efull = lax.broadcasted_iota(jnp.int32, (LANE, tm), 0)
    rec_t = (jnp.where(efull == INFO_E1, i1.astype(F32), 0.0) + jnp.where(efull == INFO_E2, i2.astype(F32), 0.0)
             + jnp.where(efull == INFO_W1, w1, 0.0) + jnp.where(efull == INFO_W2, w2, 0.0))
    rec = rec_t.T
    onehot = jnp.where((efull == i1) | (efull == i2), 1.0, 0.0).T
    r = lax.broadcasted_iota(jnp.int32, (tm, tm), 0)
    c = lax.broadcasted_iota(jnp.int32, (tm, tm), 1)
    incl = _dot(jnp.where(r >= c, 1.0, 0.0).astype(BF16), onehot.astype(BF16)) + carry_ref[...]
    carry_ref[...] = incl[tm - 1:tm, :]
    rank = incl - onehot
    lane = lax.broadcasted_iota(jnp.int32, (tm, LANE), 1)
    lane_f = lane.astype(F32)
    r1 = jnp.sum(jnp.where(lane_f == rec[:, INFO_E1:INFO_E1 + 1], rank, 0.0), axis=-1, keepdims=True)
    r2 = jnp.sum(jnp.where(lane_f == rec[:, INFO_E2:INFO_E2 + 1], rank, 0.0), axis=-1, keepdims=True)
    return jnp.where(lane == INFO_R1, r1, jnp.where(lane == INFO_R2, r2, rec))


def _pack_bf16_pairs(v):
    n = v.shape[1] // 2
    hi = lax.bitcast_convert_type(v[:, :n].astype(BF16).astype(F32), jnp.uint32)
    lo = lax.bitcast_convert_type(v[:, n:].astype(BF16).astype(F32), jnp.uint32)
    return hi | (lo >> 16)


def _unpack_bf16_pairs(w):
    first = lax.bitcast_convert_type(w & jnp.uint32(0xFFFF0000), F32)
    second = lax.bitcast_convert_type(w << 16, F32)
    return first, second


def _modulate_kernel(x_ref, g_ref, sh_ref, sc_ref, *rest, route):
    x = x_ref[...]
    h = _rms(x) * g_ref[...] * (1.0 + sc_ref[...]) + sh_ref[...]
    if route is None:
        (h_ref,) = rest
        h_ref[...] = h.astype(h_ref.dtype)
        return
    wr_ref, br_ref = rest[:2]
    topk = _route_topk(_dot_f32(h, wr_ref[...]), br_ref[...])
    if route == "dense":
        h_ref, gates_ref = rest[2:]
        h_ref[...] = h.astype(h_ref.dtype)
        gates_ref[...] = _dense_gates(*topk)
    else:
        h_ref, rec_ref, cnt_ref, carry_ref = rest[2:]

        @pl.when(pl.program_id(0) == 0)
        def _():
            carry_ref[...] = jnp.zeros_like(carry_ref)

        h_ref[...] = _pack_bf16_pairs(h)
        rec_ref[...] = _routing_record(*topk, carry_ref)
        cnt_ref[...] = carry_ref[...]


def _modulate(x, gain, shift, scale, router=None, route=None):
    m, d = x.shape
    tm = min(256, m)
    vec = pl.BlockSpec((1, d), lambda i: (0, 0))
    in_specs = [pl.BlockSpec((tm, d), lambda i: (i, 0)), vec, vec, vec]
    args = [x, gain, shift, scale]
    out_shape = [jax.ShapeDtypeStruct((m, d), BF16)]
    out_specs = [pl.BlockSpec((tm, d), lambda i: (i, 0))]
    scratch = []
    if route is not None:
        wr, br = router
        in_specs += [pl.BlockSpec((d, LANE), lambda i: (0, 0)), pl.BlockSpec((N_EXPERTS, 1), lambda i: (0, 0))]
        args += [wr, br]
        out_shape.append(jax.ShapeDtypeStruct((m, LANE), F32))
        out_specs.append(pl.BlockSpec((tm, LANE), lambda i: (i, 0)))
    if route == "routed":
        out_shape[0] = jax.ShapeDtypeStruct((m, d // 2), jnp.uint32)
        out_specs[0] = pl.BlockSpec((tm, d // 2), lambda i: (i, 0))
        out_shape.append(jax.ShapeDtypeStruct((1, LANE), F32))
        out_specs.append(pl.BlockSpec((1, LANE), lambda i: (0, 0)))
        scratch = [pltpu.VMEM((1, LANE), F32)]
    out = pl.pallas_call(
        functools.partial(_modulate_kernel, route=route),
        out_shape=out_shape, grid=(m // tm,), in_specs=in_specs, out_specs=out_specs, scratch_shapes=scratch,
        compiler_params=_cp("arbitrary" if route == "routed" else "parallel"),
        name="modulate" if route is None else "modulate_route_" + route,
    )(*args)
    return out[0] if route is None else out


def _mm_kernel(*refs, n_pairs, residual):
    a_refs, w_refs = refs[:n_pairs], refs[n_pairs:2 * n_pairs]
    o_ref = refs[-1]
    acc = _dot(a_refs[0][...], w_refs[0][...])
    for a_ref, w_ref in zip(a_refs[1:], w_refs[1:]):
        acc += _dot(a_ref[...], w_ref[...])
    if residual:
        x_ref, g_ref = refs[2 * n_pairs:2 * n_pairs + 2]
        acc = x_ref[...] + g_ref[...] * acc
    o_ref[...] = acc.astype(o_ref.dtype)


def _mm(pairs, *, out_dtype, residual=None, tm=1024, tn=512, name="mm"):
    m = pairs[0][0].shape[0]
    n = pairs[0][1].shape[1]
    tm, tn = min(tm, m), min(tn, n)
    assert m % tm == 0 and n % tn == 0, (m, n, tm, tn)
    in_specs = [pl.BlockSpec((tm, a.shape[1]), lambda i, j: (i, 0)) for a, _ in pairs]
    in_specs += [pl.BlockSpec((w.shape[0], tn), lambda i, j: (0, j)) for _, w in pairs]
    args = [a for a, _ in pairs] + [w for _, w in pairs]
    if residual is not None:
        in_specs += [pl.BlockSpec((tm, tn), lambda i, j: (i, j)), pl.BlockSpec((1, tn), lambda i, j: (0, j))]
        args += list(residual)
    return pl.pallas_call(
        functools.partial(_mm_kernel, n_pairs=len(pairs), residual=residual is not None),
        out_shape=jax.ShapeDtypeStruct((m, n), out_dtype),
        grid=(m // tm, n // tn), in_specs=in_specs,
        out_specs=pl.BlockSpec((tm, tn), lambda i, j: (i, j)),
        compiler_params=_cp("parallel", "arbitrary"),
        name=name,
    )(*args)


def _rope_rotate(v, cos, sin_signed):
    lane = lax.broadcasted_iota(jnp.int32, (1, LANE), 1)
    first_half = (lane % 32) < 16
    partner = jnp.where(first_half, pltpu.roll(v, LANE - 16, axis=1), pltpu.roll(v, 16, axis=1))
    return v * cos + partner * sin_signed


def _q_up_kernel(cq_ref, ga_ref, w_ref, gn_ref, gr_ref, cos_ref, sin_ref, o_ref, *, scale):
    cqn = (_rms(cq_ref[...].astype(F32)) * ga_ref[...]).astype(BF16)
    q = _dot(cqn, w_ref[...])
    cos, sin = cos_ref[...], sin_ref[...]
    gn, gr = gn_ref[...] * scale, gr_ref[...] * scale
    for h in range(MLA_HEADS):
        o = h * MLA_HEAD_PAD
        o_ref[:, o:o + LANE] = (_rms(q[:, o:o + LANE]) * gn).astype(BF16)
        qr = _rms(q[:, o + LANE:o + 2 * LANE], MLA_ROPE) * gr
        o_ref[:, o + LANE:o + 2 * LANE] = _rope_rotate(qr, cos, sin).astype(BF16)


def _q_up(px, ga, w, gn, gr, cos, sin):
    m = px.shape[0]
    tm = min(512, m)
    n = MLA_HEADS * MLA_HEAD_PAD
    vec = lambda width: pl.BlockSpec((1, width), lambda i: (0, 0))
    return pl.pallas_call(
        functools.partial(_q_up_kernel, scale=MLA_QK ** -0.5 * LOG2E),
        out_shape=jax.ShapeDtypeStruct((m, n), BF16),
        grid=(m // tm,),
        in_specs=[pl.BlockSpec((tm, MLA_Q_RANK), lambda i: (i, 0)), vec(MLA_Q_RANK),
                  pl.BlockSpec((MLA_Q_RANK, n), lambda i: (0, 0)), vec(LANE), vec(LANE),
                  pl.BlockSpec((tm, LANE), lambda i: (i, 0)), pl.BlockSpec((tm, LANE), lambda i: (i, 0))],
        out_specs=pl.BlockSpec((tm, n), lambda i: (i, 0)),
        compiler_params=_cp("parallel"),
        name="mla_q_up",
    )(px, ga, w, gn, gr, cos, sin)


def _kv_up_kernel(ckv_ref, kr_ref, ga_ref, w_ref, gn_ref, gr_ref, cos_ref, sin_ref, k_ref, v_ref):
    ckvn = (_rms(ckv_ref[...].astype(F32)) * ga_ref[...]).astype(BF16)
    kv = _dot(ckvn, w_ref[...])
    kr = _rms(kr_ref[...].astype(F32), MLA_ROPE) * gr_ref[...]
    kr = _rope_rotate(kr, cos_ref[...], sin_ref[...]).astype(BF16)
    gn = gn_ref[...]
    for h in range(MLA_HEADS):
        o = h * MLA_HEAD_PAD
        k_ref[:, o:o + LANE] = (_rms(kv[:, o:o + LANE]) * gn).astype(BF16)
        k_ref[:, o + LANE:o + 2 * LANE] = kr
        v_ref[:, h * MLA_V:(h + 1) * MLA_V] = kv[:, o + LANE:o + 2 * LANE].astype(BF16)


def _kv_up(px, ckv_blk, kr_blk, ga, w, gn, gr, cos, sin):
    m = px.shape[0]
    tm = min(512, m)
    vec = lambda width: pl.BlockSpec((1, width), lambda i: (0, 0))
    return pl.pallas_call(
        _kv_up_kernel,
        out_shape=[jax.ShapeDtypeStruct((m, MLA_HEADS * MLA_HEAD_PAD), BF16),
                   jax.ShapeDtypeStruct((m, MLA_HEADS * MLA_V), BF16)],
        grid=(m // tm,),
        in_specs=[pl.BlockSpec((tm, MLA_KV_RANK), lambda i: (i, ckv_blk)),
                  pl.BlockSpec((tm, LANE), lambda i: (i, kr_blk)), vec(MLA_KV_RANK),
                  pl.BlockSpec((MLA_KV_RANK, MLA_HEADS * MLA_HEAD_PAD), lambda i: (0, 0)), vec(LANE), vec(LANE),
                  pl.BlockSpec((tm, LANE), lambda i: (i, 0)), pl.BlockSpec((tm, LANE), lambda i: (i, 0))],
        out_specs=[pl.BlockSpec((tm, MLA_HEADS * MLA_HEAD_PAD), lambda i: (i, 0)),
                   pl.BlockSpec((tm, MLA_HEADS * MLA_V), lambda i: (i, 0))],
        compiler_params=_cp("parallel"),
        name="mla_kv_up",
    )(px, px, ga, w, gn, gr, cos, sin)


def _attn_kernel(q_ref, k_ref, v_ref, o_ref, sa_ref, sb_ref, m_ref, l_ref, acc_ref, *, ck, n_chunks):
    m_ref[...] = jnp.full_like(m_ref, -jnp.inf)
    l_ref[...] = jnp.zeros_like(l_ref)
    acc_ref[...] = jnp.zeros_like(acc_ref)

    def rows(c):
        return pl.ds(pl.multiple_of(c * ck, ck), ck)

    def scores(c, s_ref):
        s_ref[...] = _dot_nt(q_ref[...], k_ref[rows(c), :])

    def update(c, s_ref):
        m_prev = m_ref[...]
        m_new = jnp.maximum(m_prev, jnp.max(s_ref[...], axis=-1, keepdims=True))
        alpha = jnp.exp2(m_prev - m_new)
        m_ref[...] = m_new
        lsum = alpha * l_ref[...]
        acc = alpha * acc_ref[...]
        for j in range(ck // ATTN_SUB):
            p = jnp.exp2(s_ref[:, j * ATTN_SUB:(j + 1) * ATTN_SUB] - m_new)
            for t in range(ATTN_SUB // LANE):
                lsum += p[:, t * LANE:(t + 1) * LANE]
            acc += _dot(p.astype(BF16), v_ref[pl.ds(pl.multiple_of(c * ck + j * ATTN_SUB, ATTN_SUB), ATTN_SUB), :])
        l_ref[...] = lsum
        acc_ref[...] = acc

    bufs = (sa_ref, sb_ref)
    scores(0, sa_ref)
    group = 4 if (n_chunks - 1) % 4 == 0 else 2
    n_groups = (n_chunks - 1) // group

    def body(t, carry):
        c = group * t
        for u in range(group):
            scores(c + u + 1, bufs[(u + 1) % 2])
            update(c + u, bufs[u % 2])
        return carry

    if n_groups:
        lax.fori_loop(0, n_groups, body, 0)
    done = group * n_groups
    for c in range(done, n_chunks):
        if c + 1 < n_chunks:
            scores(c + 1, bufs[(c + 1) % 2])
        update(c, bufs[c % 2])
    o_ref[...] = (acc_ref[...] / jnp.sum(l_ref[...], axis=-1, keepdims=True)).astype(o_ref.dtype)


ATTN_SUB = 2 * LANE


def _attn_tiles(tq_total, tk_total):
    tq = min(1024, tq_total)
    ck = tk_total
    for cand in (1280, 1024, 512, 256):
        if tk_total % cand == 0:
            ck = cand
            break
    return tq, ck


def _attention(q, k, v):
    tq_total, tk_total = q.shape[0], k.shape[0]
    tq, ck = _attn_tiles(tq_total, tk_total)
    return pl.pallas_call(
        functools.partial(_attn_kernel, ck=ck, n_chunks=tk_total // ck),
        out_shape=jax.ShapeDtypeStruct((tq_total, MLA_HEADS * MLA_V), BF16),
        grid=(MLA_HEADS, tq_total // tq),
        in_specs=[pl.BlockSpec((tq, MLA_HEAD_PAD), lambda h, i: (i, h)),
                  pl.BlockSpec((tk_total, MLA_HEAD_PAD), lambda h, i: (0, h), pipeline_mode=pl.Buffered(1)),
                  pl.BlockSpec((tk_total, MLA_V), lambda h, i: (0, h), pipeline_mode=pl.Buffered(1))],
        out_specs=pl.BlockSpec((tq, MLA_V), lambda h, i: (i, h)),
        scratch_shapes=[pltpu.VMEM((tq, ck), F32), pltpu.VMEM((tq, ck), F32),
                        pltpu.VMEM((tq, 1), F32), pltpu.VMEM((tq, LANE), F32), pltpu.VMEM((tq, MLA_V), F32)],
        compiler_params=_cp("parallel", "arbitrary"),
        name="mla_attention",
    )(q, k, v)


CONV_TILE = 512
HALO = 16


def _conv_kernel(gb_ref, gc_ref, u_ref, gcp_ref, up_ref, gcn_ref, un_ref, w_ref, b_ref, o_ref):
    i = pl.program_id(0)
    z = gc_ref[...].astype(F32) * u_ref[...].astype(F32)
    tm = z.shape[0]
    z_before = gcp_ref[HALO - 1:HALO, :].astype(F32) * up_ref[HALO - 1:HALO, :].astype(F32)
    z_after = gcn_ref[0:1, :].astype(F32) * un_ref[0:1, :].astype(F32)
    z_before = jnp.where(i > 0, z_before, 0.0)
    z_after = jnp.where(i < pl.num_programs(0) - 1, z_after, 0.0)
    rowid = lax.broadcasted_iota(jnp.int32, z.shape, 0)
    z_prev = jnp.where(rowid == 0, z_before, pltpu.roll(z, 1, axis=0))
    z_next = jnp.where(rowid == tm - 1, z_after, pltpu.roll(z, tm - 1, axis=0))
    w = w_ref[...]
    y = z_prev * w[0:1] + z * w[1:2] + z_next * w[2:3] + b_ref[...]
    o_ref[...] = (gb_ref[...].astype(F32) * y).astype(o_ref.dtype)


def _conv(px, col0, conv_w, conv_b):
    m = px.shape[0]
    tm = min(512, m)
    cb = CONV_WIDTH // CONV_TILE
    b0 = col0 // CONV_TILE
    nhalo = m // HALO
    main = lambda off: pl.BlockSpec((tm, CONV_TILE), lambda i, j: (i, b0 + off * cb + j))
    prev = lambda off: pl.BlockSpec(
        (HALO, CONV_TILE), lambda i, j: (jnp.maximum(i * (tm // HALO) - 1, 0), b0 + off * cb + j))
    nxt = lambda off: pl.BlockSpec(
        (HALO, CONV_TILE), lambda i, j: (jnp.minimum((i + 1) * (tm // HALO), nhalo - 1), b0 + off * cb + j))
    return pl.pallas_call(
        _conv_kernel,
        out_shape=jax.ShapeDtypeStruct((m, CONV_WIDTH), BF16),
        grid=(m // tm, cb),
        in_specs=[main(0), main(1), main(2), prev(1), prev(2), nxt(1), nxt(2),
                  pl.BlockSpec((3, CONV_TILE), lambda i, j: (0, j)),
                  pl.BlockSpec((1, CONV_TILE), lambda i, j: (0, j))],
        out_specs=pl.BlockSpec((tm, CONV_TILE), lambda i, j: (i, j)),
        compiler_params=_cp("parallel", "parallel"),
        name="gated_conv3",
    )(px, px, px, px, px, px, px, conv_w, conv_b)


def _moe_up_kernel(h_ref, wg_ref, wu_ref, gates_ref, o_ref):
    e = pl.program_id(1)
    h = h_ref[...]
    a = _dot(h, wg_ref[...])
    u = _dot(h, wu_ref[...])
    gates = gates_ref[...]
    lane = lax.broadcasted_iota(jnp.int32, gates.shape, 1)
    g = jnp.sum(jnp.where(lane == e, gates, 0.0), axis=-1, keepdims=True)
    o_ref[...] = (a * jax.nn.sigmoid(a) * u * g).astype(o_ref.dtype)


def _moe_up(h, wg, wu, gates):
    m, d = h.shape
    tm = min(1024, m)
    return pl.pallas_call(
        _moe_up_kernel,
        out_shape=jax.ShapeDtypeStruct((m, N_EXPERTS * D_FF), BF16),
        grid=(m // tm, N_EXPERTS),
        in_specs=[pl.BlockSpec((tm, d), lambda i, e: (i, 0)),
                  pl.BlockSpec((None, d, D_FF), lambda i, e: (e, 0, 0)),
                  pl.BlockSpec((None, d, D_FF), lambda i, e: (e, 0, 0)),
                  pl.BlockSpec((tm, LANE), lambda i, e: (i, 0))],
        out_specs=pl.BlockSpec((tm, D_FF), lambda i, e: (i, e)),
        compiler_params=_cp("parallel", "arbitrary"),
        name="moe_up",
    )(h, wg, wu, gates)


MOE_TOK_TILE = 256
MOE_ROW_TILE = 256


def _row_copy(src, src_row, dst, dst_row, sem):
    return pltpu.make_async_copy(src.at[pl.ds(src_row, 1)], dst.at[pl.ds(dst_row, 1)], sem)


def _moe_dispatch_kernel(pos_ref, h_ref, init_ref, xs_ref, sem):
    del init_ref
    tm = h_ref.shape[0]

    def issue(r, carry):
        for k in range(2):
            _row_copy(h_ref, r, xs_ref, pos_ref[0, 2 * r + k], sem).start()
        return carry

    def drain(r, carry):
        for k in range(2):
            _row_copy(h_ref, r, xs_ref, pos_ref[0, 2 * r + k], sem).wait()
        return carry

    lax.fori_loop(0, tm, issue, 0, unroll=8)
    lax.fori_loop(0, tm, drain, 0, unroll=8)


def _moe_dispatch(h_packed, pos, n_rows):
    m, w = h_packed.shape
    tm = MOE_TOK_TILE
    return pl.pallas_call(
        _moe_dispatch_kernel,
        out_shape=jax.ShapeDtypeStruct((n_rows, w), jnp.uint32),
        grid=(m // tm,),
        in_specs=[pl.BlockSpec((None, 1, 2 * tm), lambda i: (i, 0, 0), memory_space=pltpu.SMEM),
                  pl.BlockSpec((tm, w), lambda i: (i, 0)),
                  pl.BlockSpec(memory_space=pl.ANY)],
        out_specs=pl.BlockSpec(memory_space=pl.ANY),
        scratch_shapes=[pltpu.SemaphoreType.DMA],
        input_output_aliases={2: 0},
        compiler_params=_cp("arbitrary"),
        name="moe_dispatch",
    )(pos, h_packed, jnp.zeros((n_rows, w), jnp.uint32))


def _moe_experts_kernel(te_ref, xs_ref, wg_ref, wu_ref, wd_ref, ys_ref):
    del te_ref
    half = xs_ref.shape[1]
    x1, x2 = _unpack_bf16_pairs(xs_ref[...])
    x1, x2 = x1.astype(BF16), x2.astype(BF16)
    a = _dot(x1, wg_ref[:half, :]) + _dot(x2, wg_ref[half:, :])
    u = _dot(x1, wu_ref[:half, :]) + _dot(x2, wu_ref[half:, :])
    hid = (a * jax.nn.sigmoid(a) * u).astype(BF16)
    ys_ref[...] = _pack_bf16_pairs(_dot(hid, wd_ref[...]))


def _moe_experts(tile_expert, xs, wg, wu, wd):
    n_rows, w = xs.shape
    d = 2 * w
    t = MOE_ROW_TILE
    return pl.pallas_call(
        _moe_experts_kernel,
        out_shape=jax.ShapeDtypeStruct((n_rows, w), jnp.uint32),
        grid_spec=pltpu.PrefetchScalarGridSpec(
            num_scalar_prefetch=1, grid=(n_rows // t,),
            in_specs=[pl.BlockSpec((t, w), lambda i, te: (i, 0)),
                      pl.BlockSpec((None, d, D_FF), lambda i, te: (te[i], 0, 0)),
                      pl.BlockSpec((None, d, D_FF), lambda i, te: (te[i], 0, 0)),
                      pl.BlockSpec((None, D_FF, d), lambda i, te: (te[i], 0, 0))],
            out_specs=pl.BlockSpec((t, w), lambda i, te: (i, 0))),
        compiler_params=_cp("arbitrary"),
        name="moe_experts",
    )(tile_expert, xs, wg, wu, wd)


def _moe_combine_kernel(pos_ref, x_ref, gate_ref, rec_ref, ys_ref, o_ref, g1_ref, g2_ref, sem):
    tm, half = g1_ref.shape
    bufs = (g1_ref, g2_ref)

    def issue(r, carry):
        for k in range(2):
            _row_copy(ys_ref, pos_ref[0, 2 * r + k], bufs[k], r, sem).start()
        return carry

    def drain(r, carry):
        for k in range(2):
            _row_copy(ys_ref, pos_ref[0, 2 * r + k], bufs[k], r, sem).wait()
        return carry

    lax.fori_loop(0, tm, issue, 0, unroll=8)
    lax.fori_loop(0, tm, drain, 0, unroll=8)
    rb = 2 * SUBLANE

    def combine(b, carry):
        rs = pl.ds(pl.multiple_of(b * rb, rb), rb)
        rec = rec_ref[rs, :]
        w1, w2 = rec[:, INFO_W1:INFO_W1 + 1], rec[:, INFO_W2:INFO_W2 + 1]
        y1 = _unpack_bf16_pairs(g1_ref[rs, :])
        y2 = _unpack_bf16_pairs(g2_ref[rs, :])
        for part in range(2):
            cols = slice(part * half, (part + 1) * half)
            o_ref[rs, cols] = x_ref[rs, cols] + gate_ref[:, cols] * (w1 * y1[part] + w2 * y2[part])
        return carry

    lax.fori_loop(0, tm // rb, combine, 0)


def _moe_combine(pos, xres, gate, rec, ys):
    m, d = xres.shape
    tm = MOE_TOK_TILE
    return pl.pallas_call(
        _moe_combine_kernel,
        out_shape=jax.ShapeDtypeStruct((m, d), F32),
        grid=(m // tm,),
        in_specs=[pl.BlockSpec((None, 1, 2 * tm), lambda i: (i, 0, 0), memory_space=pltpu.SMEM),
                  pl.BlockSpec((tm, d), lambda i: (i, 0)),
                  pl.BlockSpec((1, d), lambda i: (0, 0)),
                  pl.BlockSpec((tm, LANE), lambda i: (i, 0)),
                  pl.BlockSpec(memory_space=pl.ANY)],
        out_specs=pl.BlockSpec((tm, d), lambda i: (i, 0)),
        scratch_shapes=[pltpu.VMEM((tm, d // 2), jnp.uint32), pltpu.VMEM((tm, d // 2), jnp.uint32),
                        pltpu.SemaphoreType.DMA],
        compiler_params=_cp("arbitrary"),
        name="moe_combine",
    )(pos, xres, gate, rec, ys)


def _moe_routed(xres, gain, shift, scale, gate, router, wg, wu, wd3):
    m = xres.shape[0]
    tm, t = MOE_TOK_TILE, MOE_ROW_TILE
    h_packed, rec, counts = _modulate(xres, gain, shift, scale, router=router, route="routed")
    n_rows = 2 * m + N_EXPERTS * t
    cnt = counts[0, :N_EXPERTS].astype(jnp.int32)
    seg = ((cnt + t - 1) // t) * t
    seg_end = jnp.cumsum(seg)
    seg_start = seg_end - seg
    e12 = rec[:, INFO_E1:INFO_E2 + 1].astype(jnp.int32)
    r12 = rec[:, INFO_R1:INFO_R2 + 1].astype(jnp.int32)
    pos = (seg_start[e12] + r12).reshape(m // tm, 1, 2 * tm)
    tile_start = jnp.arange(n_rows // t, dtype=jnp.int32) * t
    tile_expert = jnp.minimum(jnp.sum(tile_start[:, None] >= seg_end[None, :], axis=1), N_EXPERTS - 1).astype(jnp.int32)
    xs = _moe_dispatch(h_packed, pos, n_rows)
    ys = _moe_experts(tile_expert, xs, wg, wu, wd3)
    return _moe_combine(pos, xres, gate, rec, ys)


GLA_LEVELS = 6
GLA_MILD_DECAY = 50.0


def _gla_tables(reverse):
    n = GLA_CHUNK
    t = np.arange(n)[:, None]
    r = np.arange(n)[None, :]
    blocks = [r <= t, r > t]
    masks = []
    for lvl in range(GLA_LEVELS):
        bs = 1 << lvl
        blocks.append((r >= (t & ~(bs - 1))) & (r <= t))
        blocks.append((r > t) & (r <= (t | (bs - 1))))
        masks.append(((t >> (lvl + 1)) == (r >> (lvl + 1))) & (((t >> lvl) & 1) == 1) & (((r >> lvl) & 1) == 0))
    masks.append(t == r)
    masks.append(r <= t)
    e = np.stack(blocks).astype(np.float32)
    msk = np.stack(masks).astype(np.float32)
    if reverse:
        e, msk = e[:, ::-1, ::-1], msk[:, ::-1, ::-1]
    return jnp.asarray(e.reshape(-1, n)), jnp.asarray(msk)


def _gla_intra_kernel(k_ref, v_ref, q_ref, a_ref, wa_ref, ba_ref, e_ref, m_ref,
                      oi_ref, qd_ref, kd_ref, dl_ref, la_ref, tot_ref, *, nchunk, reverse):
    n = GLA_CHUNK
    scale = GLA_DK ** -0.5

    a = a_ref[...]
    z = _dot(a, wa_ref[0]) + _dot(a, wa_ref[1]) + _dot(a, wa_ref[2]) + ba_ref[...]
    la = (jnp.minimum(z, 0.0) - jnp.log1p(jnp.exp(-jnp.abs(z)))) * (1.0 / GLA_TEMP)
    la_ref[...] = la
    totals = jnp.sum(la.reshape(nchunk, n, GLA_DK), axis=1)
    tot_ref[...] = totals
    dl_ref[...] = jnp.exp(totals)
    mild = jnp.min(totals) > -GLA_MILD_DECAY
    rowid = lax.broadcasted_iota(jnp.int32, (n, GLA_DK), 0)

    def load(c):
        rows = pl.ds(pl.multiple_of(c * n, n), n)
        return rows, q_ref[rows, :].astype(F32) * scale, k_ref[rows, :].astype(F32)

    def mild_chunk(c, carry):
        rows, q, k = load(c)
        la_c = la_ref[rows, :]
        total = tot_ref[pl.ds(c, 1), :]
        cum = la_c
        shift = 1
        while shift < n:
            cum = cum + jnp.where(rowid >= shift, pltpu.roll(cum, shift, axis=0), 0.0)
            shift *= 2
        if reverse:
            cum = total - cum + la_c
        qd = (q * jnp.exp(cum)).astype(BF16)
        kinv = k * jnp.exp(-cum)
        qd_ref[rows, :] = qd
        kd_ref[rows, :] = (kinv * jnp.exp(total)).astype(BF16)
        att = m_ref[GLA_LEVELS + 1] * _dot_nt(qd, kinv.astype(BF16))
        oi_ref[rows, :] = _dot(att.astype(BF16), v_ref[rows, :])
        return carry

    def harsh_chunk(c, carry):
        rows, q, k = load(c)
        ex = _dot_f32(e_ref[...], la_ref[rows, :])
        qd_ref[rows, :] = (q * jnp.exp(ex[0:n])).astype(BF16)
        kd_ref[rows, :] = (k * jnp.exp(ex[n:2 * n])).astype(BF16)
        att = m_ref[GLA_LEVELS] * _dot_nt(q.astype(BF16), k.astype(BF16))
        for lvl in range(GLA_LEVELS):
            qs = (q * jnp.exp(ex[(2 + 2 * lvl) * n:(3 + 2 * lvl) * n])).astype(BF16)
            ks = (k * jnp.exp(ex[(3 + 2 * lvl) * n:(4 + 2 * lvl) * n])).astype(BF16)
            att += m_ref[lvl] * _dot_nt(qs, ks)
        oi_ref[rows, :] = _dot(att.astype(BF16), v_ref[rows, :])
        return carry

    @pl.when(mild)
    def _():
        lax.fori_loop(0, nchunk, mild_chunk, 0, unroll=2)

    @pl.when(jnp.logical_not(mild))
    def _():
        lax.fori_loop(0, nchunk, harsh_chunk, 0)


def _gla_blocks(m):
    tb = min(512, m)
    return tb, tb // GLA_CHUNK


def _gla_intra(px, cols, wa_pad, ba, reverse):
    m = px.shape[0]
    tb, nchunk = _gla_blocks(m)
    kb, vb, qb, ab = cols
    e, msk = _gla_tables(reverse)
    return pl.pallas_call(
        functools.partial(_gla_intra_kernel, nchunk=nchunk, reverse=reverse),
        out_shape=[jax.ShapeDtypeStruct((m, GLA_VAL), F32), jax.ShapeDtypeStruct((m, GLA_KEY), BF16),
                   jax.ShapeDtypeStruct((m, GLA_KEY), BF16), jax.ShapeDtypeStruct((m // GLA_CHUNK, GLA_KEY), F32)],
        grid=(m // tb, GLA_HEADS),
        in_specs=[pl.BlockSpec((tb, GLA_DK), lambda i, h: (i, kb + h)),
                  pl.BlockSpec((tb, GLA_DV), lambda i, h: (i, vb + h)),
                  pl.BlockSpec((tb, GLA_DK), lambda i, h: (i, qb + h)),
                  pl.BlockSpec((tb, LANE), lambda i, h: (i, ab)),
                  pl.BlockSpec((3, LANE, GLA_DK), lambda i, h: (0, 0, h)),
                  pl.BlockSpec((1, GLA_DK), lambda i, h: (0, h)),
                  pl.BlockSpec(e.shape, lambda i, h: (0, 0)),
                  pl.BlockSpec(msk.shape, lambda i, h: (0, 0, 0))],
        out_specs=[pl.BlockSpec((tb, GLA_DV), lambda i, h: (i, h)),
                   pl.BlockSpec((tb, GLA_DK), lambda i, h: (i, h)),
                   pl.BlockSpec((tb, GLA_DK), lambda i, h: (i, h)),
                   pl.BlockSpec((nchunk, GLA_DK), lambda i, h: (i, h))],
        scratch_shapes=[pltpu.VMEM((tb, GLA_DK), F32), pltpu.VMEM((nchunk, GLA_DK), F32)],
        compiler_params=_cp("parallel", "parallel"),
        name="gla_intra_bwd" if reverse else "gla_intra_fwd",
    )(px, px, px, px, wa_pad, ba, e, msk)


def _gla_scan_kernel(qd_ref, kd_ref, v_ref, dl_ref, oi_ref, s0_ref, o_ref, sf_ref, s_ref, *, nchunk, reverse):
    i = pl.program_id(1)
    n = GLA_CHUNK

    @pl.when(i == 0)
    def _():
        s_ref[...] = s0_ref[...]

    def chunk(cc, carry):
        c = (nchunk - 1 - cc) if reverse else cc
        rows = pl.ds(pl.multiple_of(c * n, n), n)
        st = s_ref[...]
        o_ref[rows, :] = oi_ref[rows, :] + _dot_nt(qd_ref[rows, :], st.astype(BF16))
        s_ref[...] = st * dl_ref[pl.ds(c, 1), :] + _dot_tn(v_ref[rows, :], kd_ref[rows, :])
        return carry

    lax.fori_loop(0, nchunk, chunk, 0, unroll=True)

    @pl.when(i == pl.num_programs(1) - 1)
    def _():
        sf_ref[...] = s_ref[...]


def _gla_scan(px, vb, qd, kd, dl, oi, s0, reverse):
    m = px.shape[0]
    tb, nchunk = _gla_blocks(m)
    nb = m // tb
    blk = (lambda i: nb - 1 - i) if reverse else (lambda i: i)
    return pl.pallas_call(
        functools.partial(_gla_scan_kernel, nchunk=nchunk, reverse=reverse),
        out_shape=[jax.ShapeDtypeStruct((m, GLA_VAL), F32),
                   jax.ShapeDtypeStruct((GLA_HEADS, GLA_DV, GLA_DK), F32)],
        grid=(GLA_HEADS, nb),
        in_specs=[pl.BlockSpec((tb, GLA_DK), lambda h, i: (blk(i), h)),
                  pl.BlockSpec((tb, GLA_DK), lambda h, i: (blk(i), h)),
                  pl.BlockSpec((tb, GLA_DV), lambda h, i: (blk(i), vb + h)),
                  pl.BlockSpec((nchunk, GLA_DK), lambda h, i: (blk(i), h)),
                  pl.BlockSpec((tb, GLA_DV), lambda h, i: (blk(i), h)),
                  pl.BlockSpec((None, GLA_DV, GLA_DK), lambda h, i: (h, 0, 0))],
        out_specs=[pl.BlockSpec((tb, GLA_DV), lambda h, i: (blk(i), h)),
                   pl.BlockSpec((None, GLA_DV, GLA_DK), lambda h, i: (h, 0, 0))],
        scratch_shapes=[pltpu.VMEM((GLA_DV, GLA_DK), F32)],
        compiler_params=_cp("parallel", "arbitrary"),
        name="gla_scan_bwd" if reverse else "gla_scan_fwd",
    )(qd, kd, px, dl, oi, s0)


def _gla_gate_kernel(of_ref, ob_ref, g_ref, gain_ref, o_ref):
    gain = gain_ref[...]
    for h in range(GLA_HEADS):
        sl = slice(h * GLA_DV, (h + 1) * GLA_DV)
        o = _rms(of_ref[:, sl] + ob_ref[:, sl]) * gain
        g = g_ref[:, sl].astype(F32)
        o_ref[:, sl] = (o * (g * jax.nn.sigmoid(g))).astype(BF16)


def _gla_gate(o_f, o_b, px, g_blk, gain):
    m = o_f.shape[0]
    tm = min(256, m)
    return pl.pallas_call(
        _gla_gate_kernel,
        out_shape=jax.ShapeDtypeStruct((m, GLA_VAL), BF16),
        grid=(m // tm,),
        in_specs=[pl.BlockSpec((tm, GLA_VAL), lambda i: (i, 0)), pl.BlockSpec((tm, GLA_VAL), lambda i: (i, 0)),
                  pl.BlockSpec((tm, GLA_VAL), lambda i: (i, g_blk)), pl.BlockSpec((1, GLA_DV), lambda i: (0, 0))],
        out_specs=pl.BlockSpec((tm, GLA_VAL), lambda i: (i, 0)),
        compiler_params=_cp("parallel"),
        name="gla_gate",
    )(o_f, o_b, px, gain)


def _pad_cols(w, mult):
    pad = (-w.shape[1]) % mult
    return jnp.pad(w, ((0, 0), (0, pad))) if pad else w


def _row(v, width=None):
    v = v.reshape(1, -1).astype(F32)
    return _pad_cols(v, width) if width else v


MM_TN = 512

AB_CKV_BLK = MLA_Q_RANK // MLA_KV_RANK
AB_CONV_COL = MLA_Q_RANK + MLA_KV_RANK
AB_KR_BLK = (AB_CONV_COL + 3 * CONV_WIDTH) // LANE
GLA_K_BLK = 0
GLA_V_BLK = GLA_KEY // GLA_DV
GLA_Q_BLK = (GLA_KEY + GLA_VAL) // GLA_DK
GLA_G_BLK = (2 * GLA_KEY + GLA_VAL) // GLA_VAL
GLA_A_BLK = (2 * GLA_KEY + 2 * GLA_VAL) // LANE


def _ab_w_in(w):
    ckv = w[:, :MLA_KV_RANK]
    kr = w[:, MLA_KV_RANK:MLA_KV_RANK + MLA_ROPE]
    cq = w[:, MLA_KV_RANK + MLA_ROPE:MLA_KV_RANK + MLA_ROPE + MLA_Q_RANK]
    conv = w[:, MLA_KV_RANK + MLA_ROPE + MLA_Q_RANK:]
    return _pad_cols(jnp.concatenate([cq, ckv, conv, kr], axis=1), MM_TN).astype(BF16)


def _ab_w_uq(w):
    w = w.reshape(MLA_Q_RANK, MLA_HEADS, MLA_QK)
    w = jnp.pad(w, ((0, 0), (0, 0), (0, MLA_HEAD_PAD - MLA_QK)))
    return w.reshape(MLA_Q_RANK, MLA_HEADS * MLA_HEAD_PAD).astype(BF16)


def _gla_w_in(w):
    o1, o2 = GLA_KEY, GLA_KEY + GLA_VAL
    o3 = o2 + 2 * GLA_RANK
    k, v, a, q, g = w[:, :o1], w[:, o1:o2], w[:, o2:o3], w[:, o3:o3 + GLA_KEY], w[:, o3 + GLA_KEY:]
    return _pad_cols(jnp.concatenate([k, v, q, g, a], axis=1), MM_TN).astype(BF16)


def _gla_wa_pad(w_a, offset):
    w = jnp.zeros((LANE, GLA_KEY), F32).at[offset:offset + GLA_RANK].set(w_a.astype(F32))
    hi = w.astype(BF16)
    r1 = w - hi.astype(F32)
    mid = r1.astype(BF16)
    lo = (r1 - mid.astype(F32)).astype(BF16)
    return jnp.stack([hi, mid, lo])


def _rope_tables(pos_rows, pos_cols):
    half = MLA_ROPE // 4
    inv_freq = ROPE_BASE ** (-jnp.arange(half, dtype=F32) / half)
    ar = pos_rows.astype(F32)[:, None] * inv_freq[None, :]
    ac = pos_cols.astype(F32)[:, None] * inv_freq[None, :]
    zeros = jnp.zeros((ar.shape[0], LANE - MLA_ROPE), F32)
    cos = jnp.concatenate([jnp.cos(ar), jnp.cos(ar), jnp.cos(ac), jnp.cos(ac), zeros], axis=1)
    sin = jnp.concatenate([-jnp.sin(ar), jnp.sin(ar), -jnp.sin(ac), jnp.sin(ac), zeros], axis=1)
    return cos, sin


def _mixer_mla_conv(hx, hc, p, rope_x, rope_c, need_ctx):
    w_in, w_uq, w_ukv, w_out = p["w_in"], p["w_uq"], p["w_ukv"], p["w_out"]

    def project(h, rope):
        px = _mm([(h, w_in)], out_dtype=BF16, name="ab_in_proj")
        k, v = _kv_up(px, AB_CKV_BLK, AB_KR_BLK, p["kv_a_gain"], w_ukv, p["k_gain_n"], p["k_gain_r"], *rope)
        return px, k, v

    def branch_out(px, k, v, rope):
        q = _q_up(px, p["q_a_gain"], w_uq, p["q_gain_n"], p["q_gain_r"], *rope)
        o = _attention(q, k, v)
        y = _conv(px, AB_CONV_COL, p["conv_w"], p["conv_b"])
        return o, y

    px_c, k_c, v_c = project(hc, rope_c)
    px_x, k_x, v_x = project(hx, rope_x)
    k_all = jnp.concatenate([k_c, k_x], axis=0)
    v_all = jnp.concatenate([v_c, v_x], axis=0)
    out_x = branch_out(px_x, k_all, v_all, rope_x)
    out_c = branch_out(px_c, k_c, v_c, rope_c) if need_ctx else None
    half = MLA_HEADS * MLA_V
    return out_x, out_c, (w_out[:half], w_out[half:])


def _mixer_gla(hx, hc, p, need_ctx):
    cols = (GLA_K_BLK, GLA_V_BLK, GLA_Q_BLK, GLA_A_BLK)
    zero_state = jnp.zeros((GLA_HEADS, GLA_DV, GLA_DK), F32)

    def run(h, s_f, s_b):
        px = _mm([(h, p["w_in"])], out_dtype=BF16, name="gla_in_proj")
        outs, states = [], []
        for reverse, wa, ba, s0 in ((False, p["wa_f"], p["ba_f"], s_f), (True, p["wa_b"], p["ba_b"], s_b)):
            oi, qd, kd, dl = _gla_intra(px, cols, wa, ba, reverse)
            o, s_fin = _gla_scan(px, GLA_V_BLK, qd, kd, dl, oi, s0, reverse)
            outs.append(o)
            states.append(s_fin)
        return px, outs, states

    px_c, o_c, (s_f, s_b) = run(hc, zero_state, zero_state)
    px_x, o_x, _ = run(hx, s_f, s_b)
    out_x = _gla_gate(o_x[0], o_x[1], px_x, GLA_G_BLK, p["o_gain"])
    out_c = _gla_gate(o_c[0], o_c[1], px_c, GLA_G_BLK, p["o_gain"]) if need_ctx else None
    return out_x, out_c


def _moe_dense(xres, gain, shift, scale, gate, router, wg, wu, wd3):
    h, gates = _modulate(xres, gain, shift, scale, router=router, route="dense")
    hid = _moe_up(h, wg, wu, gates)
    wd = wd3.reshape(N_EXPERTS * D_FF, wd3.shape[-1])
    return _mm([(hid, wd)], out_dtype=F32, residual=(xres, gate), name="moe_down")


def _moe(xres, *args):
    routed = xres.shape[0] >= 4 * MOE_ROW_TILE and xres.shape[0] % MOE_TOK_TILE == 0
    return (_moe_routed if routed else _moe_dense)(xres, *args)


def kernel(x, c, ctx, c_ctx, ada_w, ada_b, norm_mix_gain, norm_ffn_gain, ab_w_in, ab_q_a_gain, ab_w_uq, ab_q_gain, ab_kv_a_gain, ab_w_ukv, ab_k_gain, ab_conv_w, ab_conv_b, ab_w_out, gla_w_in, gla_w_af, gla_b_af, gla_w_ab, gla_b_ab, gla_o_gain, gla_w_out, router_w, router_b, moe_w_gate, moe_w_up, moe_w_down):
    b, s, d = x.shape
    assert b == 1 and s % GRID_W == 0
    depth = ada_w.shape[0]
    xs, cx = x[0], ctx[0]
    n_ctx = cx.shape[0]

    mod = _adaln(c.reshape(d, 1), c_ctx.reshape(d, 1), ada_w, ada_b).reshape(depth, 2, 6, 1, d)
    tok = jnp.arange(s, dtype=jnp.int32)
    rope_x = _rope_tables(tok // GRID_W, tok % GRID_W)
    rope_c = (jnp.concatenate([jnp.ones((n_ctx, MLA_ROPE), F32), jnp.zeros((n_ctx, LANE - MLA_ROPE), F32)], axis=1),
              jnp.zeros((n_ctx, LANE), F32))
    router = (_pad_cols(router_w.astype(F32), LANE), router_b.reshape(N_EXPERTS, 1).astype(F32))

    for i in range(depth):
        last = i == depth - 1
        j = i // 2
        mx, mc = mod[i, 0], mod[i, 1]
        gain_mix = _row(norm_mix_gain[i])
        hx = _modulate(xs, gain_mix, mx[0], mx[1])
        hc = _modulate(cx, gain_mix, mc[0], mc[1])
        if i % 2 == 0:
            p = dict(
                w_in=_ab_w_in(ab_w_in[j]), w_uq=_ab_w_uq(ab_w_uq[j]), w_ukv=ab_w_ukv[j].astype(BF16),
                w_out=ab_w_out[j].astype(BF16),
                q_a_gain=_row(ab_q_a_gain[j]), kv_a_gain=_row(ab_kv_a_gain[j]),
                q_gain_n=_row(ab_q_gain[j][:MLA_NOPE]), q_gain_r=_row(ab_q_gain[j][MLA_NOPE:], LANE),
                k_gain_n=_row(ab_k_gain[j][:MLA_NOPE]), k_gain_r=_row(ab_k_gain[j][MLA_NOPE:], LANE),
                conv_w=ab_conv_w[j].astype(F32), conv_b=_row(ab_conv_b[j]))
            (o_x, y_x), out_c, (w_o, w_y) = _mixer_mla_conv(hx, hc, p, rope_x, rope_c, not last)
            xs = _mm([(o_x, w_o), (y_x, w_y)], out_dtype=F32, residual=(xs, mx[2]), name="ab_out_proj")
            if not last:
                cx = _mm([(out_c[0], w_o), (out_c[1], w_y)], out_dtype=F32, residual=(cx, mc[2]), name="ab_out_proj")
        else:
            p = dict(
                w_in=_gla_w_in(gla_w_in[j]),
                wa_f=_gla_wa_pad(gla_w_af[j], 0), wa_b=_gla_wa_pad(gla_w_ab[j], GLA_RANK),
                ba_f=_row(gla_b_af[j]), ba_b=_row(gla_b_ab[j]), o_gain=_row(gla_o_gain[j]))
            w_out = gla_w_out[j].astype(BF16)
            y_x, y_c = _mixer_gla(hx, hc, p, not last)
            xs = _mm([(y_x, w_out)], out_dtype=F32, residual=(xs, mx[2]), name="gla_out_proj")
            if not last:
                cx = _mm([(y_c, w_out)], out_dtype=F32, residual=(cx, mc[2]), name="gla_out_proj")
        gain_ffn = _row(norm_ffn_gain[i])
        wg, wu = moe_w_gate[i].astype(BF16), moe_w_up[i].astype(BF16)
        wd = moe_w_down[i].astype(BF16)
        xs = _moe(xs, gain_ffn, mx[3], mx[4], mx[5], router, wg, wu, wd)
        if not last:
            cx = _moe(cx, gain_ffn, mc[3], mc[4], mc[5], router, wg, wu, wd)
    return xs[None]
```

```python
import functools

import numpy as np
import jax
import jax.numpy as jnp
from jax import lax
from jax.experimental import pallas as pl
from jax.experimental.pallas import tpu as pltpu

F32 = jnp.float32
BF16 = jnp.bfloat16
HIGHEST = lax.Precision.HIGHEST

LANE = 128
SUBLANE = 8
VMEM_LIMIT_BYTES = 56 * 1024 * 1024

GRID_W = 64
EPS = 1e-6
MLA_HEADS = 16
MLA_Q_RANK = 1024
MLA_KV_RANK = 512
MLA_NOPE = 128
MLA_ROPE = 64
MLA_V = 128
MLA_QK = MLA_NOPE + MLA_ROPE
MLA_HEAD_PAD = 2 * LANE
ROPE_BASE = 10000.0
LOG2E = 1.4426950408889634
CONV_WIDTH = 2048
GLA_HEADS = 8
GLA_DK = 256
GLA_DV = 512
GLA_KEY = GLA_HEADS * GLA_DK
GLA_VAL = GLA_HEADS * GLA_DV
GLA_RANK = 16
GLA_TEMP = 16.0
GLA_CHUNK = 64
N_EXPERTS = 16
N_GROUPS = 4
GROUP_SIZE = N_EXPERTS // N_GROUPS
D_FF = 256


def _cp(*sem):
    return pltpu.CompilerParams(dimension_semantics=sem, vmem_limit_bytes=VMEM_LIMIT_BYTES)


def _dot(a, b):
    return jnp.dot(a, b, preferred_element_type=F32)


def _dot_nt(a, b):
    return lax.dot_general(a, b, (((1,), (1,)), ((), ())), preferred_element_type=F32)


def _dot_tn(a, b):
    return lax.dot_general(a, b, (((0,), (0,)), ((), ())), preferred_element_type=F32)


def _dot_f32(a, b):
    return jnp.dot(a, b, precision=HIGHEST, preferred_element_type=F32)


def _rms(v, n=None):
    ss = jnp.sum(v * v, axis=-1, keepdims=True)
    return v * lax.rsqrt(ss * (1.0 / (n if n is not None else v.shape[-1])) + EPS)


def _adaln_kernel(c_ref, cc_ref, w_ref, b_ref, o_ref, acc_ref):
    k = pl.program_id(2)

    @pl.when(k == 0)
    def _():
        acc_ref[...] = jnp.zeros_like(acc_ref)

    w = w_ref[...]
    tk, tn = w.shape
    for r, ref in enumerate((c_ref, cc_ref)):
        cv = ref[...]
        cs = cv * jax.nn.sigmoid(cv)
        acc_ref[r] += jnp.sum((w * cs).reshape(tk // SUBLANE, SUBLANE, tn), axis=0)

    @pl.when(k == pl.num_programs(2) - 1)
    def _():
        o_ref[...] = jnp.sum(acc_ref[...], axis=1) + b_ref[...]


def _adaln(c_col, cc_col, ada_w, ada_b):
    depth, d, n = ada_w.shape
    tk, tn = 512, 2048
    return pl.pallas_call(
        _adaln_kernel,
        out_shape=jax.ShapeDtypeStruct((depth, 2, n), F32),
        grid=(depth, n // tn, d // tk),
        in_specs=[
            pl.BlockSpec((tk, 1), lambda l, j, k: (k, 0)),
            pl.BlockSpec((tk, 1), lambda l, j, k: (k, 0)),
            pl.BlockSpec((None, tk, tn), lambda l, j, k: (l, k, j)),
            pl.BlockSpec((None, 1, tn), lambda l, j, k: (l, 0, j)),
        ],
        out_specs=pl.BlockSpec((None, 2, tn), lambda l, j, k: (l, 0, j)),
        scratch_shapes=[pltpu.VMEM((2, SUBLANE, tn), F32)],
        compiler_params=_cp("parallel", "parallel", "arbitrary"),
        name="adaln_mod",
    )(c_col, cc_col, ada_w, ada_b.reshape(depth, 1, n))


def _route_topk(logits, bias_col):
    tm = logits.shape[0]
    s = jax.nn.sigmoid(logits.T[:N_EXPERTS])
    sel = s + bias_col
    row = [sel[e:e + 1] for e in range(N_EXPERTS)]
    score = []
    for g in range(N_GROUPS):
        a, b, c2, d2 = row[GROUP_SIZE * g:GROUP_SIZE * (g + 1)]
        hi1, lo1 = jnp.maximum(a, b), jnp.minimum(a, b)
        hi2, lo2 = jnp.maximum(c2, d2), jnp.minimum(c2, d2)
        score.append(jnp.maximum(hi1, hi2) + jnp.maximum(jnp.minimum(hi1, hi2), jnp.maximum(lo1, lo2)))
    best, gidx = score[0], jnp.zeros((1, tm), jnp.int32)
    for g in range(1, N_GROUPS):
        upd = score[g] > best
        gidx = jnp.where(upd, g, gidx)
        best = jnp.where(upd, score[g], best)
    neg = jnp.full((1, tm), -jnp.inf, F32)
    cand = [jnp.where(gidx == (e // GROUP_SIZE), row[e], neg) for e in range(N_EXPERTS)]

    def argmax_first(vals):
        bv, bi = vals[0], jnp.zeros((1, tm), jnp.int32)
        for e in range(1, N_EXPERTS):
            upd = vals[e] > bv
            bi = jnp.where(upd, e, bi)
            bv = jnp.where(upd, vals[e], bv)
        return bi

    i1 = argmax_first(cand)
    i2 = argmax_first([jnp.where(i1 == e, neg, cand[e]) for e in range(N_EXPERTS)])
    eidx = lax.broadcasted_iota(jnp.int32, (N_EXPERTS, tm), 0)
    w1 = jnp.sum(jnp.where(eidx == i1, s, 0.0), axis=0, keepdims=True)
    w2 = jnp.sum(jnp.where(eidx == i2, s, 0.0), axis=0, keepdims=True)
    den = w1 + w2
    return i1, i2, w1 / den, w2 / den


def _dense_gates(i1, i2, w1, w2):
    efull = lax.broadcasted_iota(jnp.int32, (LANE, i1.shape[1]), 0)
    return (jnp.where(efull == i1, w1, 0.0) + jnp.where(efull == i2, w2, 0.0)).T


INFO_E1, INFO_E2, INFO_W1, INFO_W2, INFO_R1, INFO_R2 = range(6)


def _routing_record(i1, i2, w1, w2, carry_ref):
    tm = i1.shape[1]
    efull = lax.broadcasted_iota(jnp.int32, (LANE, tm), 0)
    rec_t = (jnp.where(efull == INFO_E1, i1.astype(F32), 0.0) + jnp.where(efull == INFO_E2, i2.astype(F32), 0.0)
             + jnp.where(efull == INFO_W1, w1, 0.0) + jnp.where(efull == INFO_W2, w2, 0.0))
    rec = rec_t.T
    onehot = jnp.where((efull == i1) | (efull == i2), 1.0, 0.0).T
    r = lax.broadcasted_iota(jnp.int32, (tm, tm), 0)
    c = lax.broadcasted_iota(jnp.int32, (tm, tm), 1)
    incl = _dot(jnp.where(r >= c, 1.0, 0.0).astype(BF16), onehot.astype(BF16)) + carry_ref[...]
    carry_ref[...] = incl[tm - 1:tm, :]
    rank = incl - onehot
    lane = lax.broadcasted_iota(jnp.int32, (tm, LANE), 1)
    lane_f = lane.astype(F32)
    r1 = jnp.sum(jnp.where(lane_f == rec[:, INFO_E1:INFO_E1 + 1], rank, 0.0), axis=-1, keepdims=True)
    r2 = jnp.sum(jnp.where(lane_f == rec[:, INFO_E2:INFO_E2 + 1], rank, 0.0), axis=-1, keepdims=True)
    return jnp.where(lane == INFO_R1, r1, jnp.where(lane == INFO_R2, r2, rec))


def _pack_bf16_pairs(v):
    n = v.shape[1] // 2
    hi = lax.bitcast_convert_type(v[:, :n].astype(BF16).astype(F32), jnp.uint32)
    lo = lax.bitcast_convert_type(v[:, n:].astype(BF16).astype(F32), jnp.uint32)
    return hi | (lo >> 16)


def _unpack_bf16_pairs(w):
    first = lax.bitcast_convert_type(w & jnp.uint32(0xFFFF0000), F32)
    second = lax.bitcast_convert_type(w << 16, F32)
    return first, second


def _modulate_kernel(x_ref, g_ref, sh_ref, sc_ref, *rest, route):
    x = x_ref[...]
    h = _rms(x) * g_ref[...] * (1.0 + sc_ref[...]) + sh_ref[...]
    if route is None:
        (h_ref,) = rest
        h_ref[...] = h.astype(h_ref.dtype)
        return
    wr_ref, br_ref = rest[:2]
    topk = _route_topk(_dot_f32(h, wr_ref[...]), br_ref[...])
    if route == "dense":
        h_ref, gates_ref = rest[2:]
        h_ref[...] = h.astype(h_ref.dtype)
        gates_ref[...] = _dense_gates(*topk)
    else:
        h_ref, rec_ref, cnt_ref, carry_ref = rest[2:]

        @pl.when(pl.program_id(0) == 0)
        def _():
            carry_ref[...] = jnp.zeros_like(carry_ref)

        h_ref[...] = _pack_bf16_pairs(h)
        rec_ref[...] = _routing_record(*topk, carry_ref)
        cnt_ref[...] = carry_ref[...]


def _modulate(x, gain, shift, scale, router=None, route=None):
    m, d = x.shape
    tm = min(256, m)
    vec = pl.BlockSpec((1, d), lambda i: (0, 0))
    in_specs = [pl.BlockSpec((tm, d), lambda i: (i, 0)), vec, vec, vec]
    args = [x, gain, shift, scale]
    out_shape = [jax.ShapeDtypeStruct((m, d), BF16)]
    out_specs = [pl.BlockSpec((tm, d), lambda i: (i, 0))]
    scratch = []
    if route is not None:
        wr, br = router
        in_specs += [pl.BlockSpec((d, LANE), lambda i: (0, 0)), pl.BlockSpec((N_EXPERTS, 1), lambda i: (0, 0))]
        args += [wr, br]
        out_shape.append(jax.ShapeDtypeStruct((m, LANE), F32))
        out_specs.append(pl.BlockSpec((tm, LANE), lambda i: (i, 0)))
    if route == "routed":
        out_shape[0] = jax.ShapeDtypeStruct((m, d // 2), jnp.uint32)
        out_specs[0] = pl.BlockSpec((tm, d // 2), lambda i: (i, 0))
        out_shape.append(jax.ShapeDtypeStruct((1, LANE), F32))
        out_specs.append(pl.BlockSpec((1, LANE), lambda i: (0, 0)))
        scratch = [pltpu.VMEM((1, LANE), F32)]
    out = pl.pallas_call(
        functools.partial(_modulate_kernel, route=route),
        out_shape=out_shape, grid=(m // tm,), in_specs=in_specs, out_specs=out_specs, scratch_shapes=scratch,
        compiler_params=_cp("arbitrary" if route == "routed" else "parallel"),
        name="modulate" if route is None else "modulate_route_" + route,
    )(*args)
    return out[0] if route is None else out


def _mm_kernel(*refs, n_pairs, residual):
    a_refs, w_refs = refs[:n_pairs], refs[n_pairs:2 * n_pairs]
    o_ref = refs[-1]
    acc = _dot(a_refs[0][...], w_refs[0][...])
    for a_ref, w_ref in zip(a_refs[1:], w_refs[1:]):
        acc += _dot(a_ref[...], w_ref[...])
    if residual:
        x_ref, g_ref = refs[2 * n_pairs:2 * n_pairs + 2]
        acc = x_ref[...] + g_ref[...] * acc
    o_ref[...] = acc.astype(o_ref.dtype)


def _mm(pairs, *, out_dtype, residual=None, tm=1024, tn=512, name="mm"):
    m = pairs[0][0].shape[0]
    n = pairs[0][1].shape[1]
    tm, tn = min(tm, m), min(tn, n)
    assert m % tm == 0 and n % tn == 0, (m, n, tm, tn)
    in_specs = [pl.BlockSpec((tm, a.shape[1]), lambda i, j: (i, 0)) for a, _ in pairs]
    in_specs += [pl.BlockSpec((w.shape[0], tn), lambda i, j: (0, j)) for _, w in pairs]
    args = [a for a, _ in pairs] + [w for _, w in pairs]
    if residual is not None:
        in_specs += [pl.BlockSpec((tm, tn), lambda i, j: (i, j)), pl.BlockSpec((1, tn), lambda i, j: (0, j))]
        args += list(residual)
    return pl.pallas_call(
        functools.partial(_mm_kernel, n_pairs=len(pairs), residual=residual is not None),
        out_shape=jax.ShapeDtypeStruct((m, n), out_dtype),
        grid=(m // tm, n // tn), in_specs=in_specs,
        out_specs=pl.BlockSpec((tm, tn), lambda i, j: (i, j)),
        compiler_params=_cp("parallel", "arbitrary"),
        name=name,
    )(*args)


def _rope_rotate(v, cos, sin_signed):
    lane = lax.broadcasted_iota(jnp.int32, (1, LANE), 1)
    first_half = (lane % 32) < 16
    partner = jnp.where(first_half, pltpu.roll(v, LANE - 16, axis=1), pltpu.roll(v, 16, axis=1))
    return v * cos + partner * sin_signed


def _q_up_kernel(cq_ref, ga_ref, w_ref, gn_ref, gr_ref, cos_ref, sin_ref, kmax_ref, o_ref, *, scale):
    cqn = (_rms(cq_ref[...].astype(F32)) * ga_ref[...]).astype(BF16)
    q = _dot(cqn, w_ref[...])
    cos, sin = cos_ref[...], sin_ref[...]
    gn, gr = gn_ref[...] * scale, gr_ref[...] * scale
    lane = lax.broadcasted_iota(jnp.int32, (1, LANE), 1)
    kmax = kmax_ref[...]
    for h in range(MLA_HEADS):
        o = h * MLA_HEAD_PAD
        qn = _rms(q[:, o:o + LANE]) * gn
        qr = _rope_rotate(_rms(q[:, o + LANE:o + 2 * LANE], MLA_ROPE) * gr, cos, sin)
        qq = jnp.sum(qn * qn, axis=-1, keepdims=True) + jnp.sum(qr * qr, axis=-1, keepdims=True)
        offset = SCORE_CENTER - jnp.sqrt(qq * kmax[:, h:h + 1])
        o_ref[:, o:o + LANE] = qn.astype(BF16)
        o_ref[:, o + LANE:o + 2 * LANE] = jnp.where(lane == OFFSET_LANE, offset, qr).astype(BF16)


def _q_up(px, ga, w, gn, gr, cos, sin, kmax):
    m = px.shape[0]
    tm = min(512, m)
    n = MLA_HEADS * MLA_HEAD_PAD
    vec = lambda width: pl.BlockSpec((1, width), lambda i: (0, 0))
    return pl.pallas_call(
        functools.partial(_q_up_kernel, scale=MLA_QK ** -0.5 * LOG2E),
        out_shape=jax.ShapeDtypeStruct((m, n), BF16),
        grid=(m // tm,),
        in_specs=[pl.BlockSpec((tm, MLA_Q_RANK), lambda i: (i, 0)), vec(MLA_Q_RANK),
                  pl.BlockSpec((MLA_Q_RANK, n), lambda i: (0, 0)), vec(LANE), vec(LANE),
                  pl.BlockSpec((tm, LANE), lambda i: (i, 0)), pl.BlockSpec((tm, LANE), lambda i: (i, 0)),
                  vec(LANE)],
        out_specs=pl.BlockSpec((tm, n), lambda i: (i, 0)),
        compiler_params=_cp("parallel"),
        name="mla_q_up",
    )(px, ga, w, gn, gr, cos, sin, kmax)


def _kv_up_kernel(ckv_ref, kr_ref, ga_ref, w_ref, gn_ref, gr_ref, cos_ref, sin_ref, k_ref, v_ref, kk_ref):
    ckvn = (_rms(ckv_ref[...].astype(F32)) * ga_ref[...]).astype(BF16)
    kv = _dot(ckvn, w_ref[...])
    kr = _rms(kr_ref[...].astype(F32), MLA_ROPE) * gr_ref[...]
    kr = _rope_rotate(kr, cos_ref[...], sin_ref[...])
    kr2 = jnp.sum(kr * kr, axis=-1, keepdims=True)
    lane = lax.broadcasted_iota(jnp.int32, (1, LANE), 1)
    krb = jnp.where(lane == OFFSET_LANE, 1.0, kr).astype(BF16)
    gn = gn_ref[...]
    kk = jnp.zeros(kk_ref.shape, F32)
    for h in range(MLA_HEADS):
        o = h * MLA_HEAD_PAD
        kn = _rms(kv[:, o:o + LANE]) * gn
        kk = jnp.where(lane == h, jnp.sum(kn * kn, axis=-1, keepdims=True) + kr2, kk)
        k_ref[:, o:o + LANE] = kn.astype(BF16)
        k_ref[:, o + LANE:o + 2 * LANE] = krb
        v_ref[:, h * MLA_V:(h + 1) * MLA_V] = kv[:, o + LANE:o + 2 * LANE].astype(BF16)
    kk_ref[...] = kk


def _colmax_kernel(x_ref, o_ref):
    @pl.when(pl.program_id(0) == 0)
    def _():
        o_ref[...] = jnp.full_like(o_ref, -jnp.inf)

    o_ref[...] = jnp.maximum(o_ref[...], jnp.max(x_ref[...], axis=0, keepdims=True))


def _colmax(x):
    m = x.shape[0]
    tm = min(2048, m)
    assert m % tm == 0, (m, tm)
    return pl.pallas_call(
        _colmax_kernel,
        out_shape=jax.ShapeDtypeStruct((1, LANE), F32),
        grid=(m // tm,),
        in_specs=[pl.BlockSpec((tm, LANE), lambda i: (i, 0))],
        out_specs=pl.BlockSpec((1, LANE), lambda i: (0, 0)),
        compiler_params=_cp("arbitrary"),
        name="colmax",
    )(x)


def _kv_up(px, ckv_blk, kr_blk, ga, w, gn, gr, cos, sin):
    m = px.shape[0]
    tm = min(512, m)
    vec = lambda width: pl.BlockSpec((1, width), lambda i: (0, 0))
    return pl.pallas_call(
        _kv_up_kernel,
        out_shape=[jax.ShapeDtypeStruct((m, MLA_HEADS * MLA_HEAD_PAD), BF16),
                   jax.ShapeDtypeStruct((m, MLA_HEADS * MLA_V), BF16),
                   jax.ShapeDtypeStruct((m, LANE), F32)],
        grid=(m // tm,),
        in_specs=[pl.BlockSpec((tm, MLA_KV_RANK), lambda i: (i, ckv_blk)),
                  pl.BlockSpec((tm, LANE), lambda i: (i, kr_blk)), vec(MLA_KV_RANK),
                  pl.BlockSpec((MLA_KV_RANK, MLA_HEADS * MLA_HEAD_PAD), lambda i: (0, 0)), vec(LANE), vec(LANE),
                  pl.BlockSpec((tm, LANE), lambda i: (i, 0)), pl.BlockSpec((tm, LANE), lambda i: (i, 0))],
        out_specs=[pl.BlockSpec((tm, MLA_HEADS * MLA_HEAD_PAD), lambda i: (i, 0)),
                   pl.BlockSpec((tm, MLA_HEADS * MLA_V), lambda i: (i, 0)),
                   pl.BlockSpec((tm, LANE), lambda i: (i, 0))],
        compiler_params=_cp("parallel"),
        name="mla_kv_up",
    )(px, px, ga, w, gn, gr, cos, sin)


def _attn_kernel(q_ref, k_ref, v_ref, o_ref, sa_ref, sb_ref, m_ref, l_ref, acc_ref, *, ck, n_chunks):
    def sub(i):
        return pl.ds(pl.multiple_of(i * ATTN_SUB, ATTN_SUB), ATTN_SUB)

    l_ref[...] = jnp.zeros_like(l_ref)
    acc_ref[...] = jnp.zeros_like(acc_ref)

    n_sub = n_chunks * ck // ATTN_SUB
    unroll = next(u for u in (13, 8, 5, 4, 3, 2, 1) if n_sub % u == 0)

    def stream(t, carry):
        lsum = l_ref[...]
        acc = acc_ref[...]
        for u in range(unroll):
            i = t * unroll + u
            p = jnp.exp2(_dot_nt(q_ref[...], k_ref[sub(i), :]))
            for part in range(ATTN_SUB // LANE):
                lsum += p[:, part * LANE:(part + 1) * LANE]
            acc += _dot(p.astype(BF16), v_ref[sub(i), :])
        l_ref[...] = lsum
        acc_ref[...] = acc
        return carry

    lax.fori_loop(0, n_sub // unroll, stream, 0)
    row_sum = jnp.sum(l_ref[...], axis=-1, keepdims=True)
    in_range = jnp.min(row_sum) >= SCORE_SUM_FLOOR

    @pl.when(in_range)
    def _():
        o_ref[...] = (acc_ref[...] / row_sum).astype(o_ref.dtype)

    @pl.when(jnp.logical_not(in_range))
    def _():
        _attn_online(q_ref, k_ref, v_ref, o_ref, sa_ref, sb_ref, m_ref, l_ref, acc_ref, ck=ck, n_chunks=n_chunks)


def _attn_online(q_ref, k_ref, v_ref, o_ref, sa_ref, sb_ref, m_ref, l_ref, acc_ref, *, ck, n_chunks):
    m_ref[...] = jnp.full_like(m_ref, -jnp.inf)
    l_ref[...] = jnp.zeros_like(l_ref)
    acc_ref[...] = jnp.zeros_like(acc_ref)

    def rows(c):
        return pl.ds(pl.multiple_of(c * ck, ck), ck)

    def scores(c, s_ref):
        s_ref[...] = _dot_nt(q_ref[...], k_ref[rows(c), :])

    def update(c, s_ref):
        m_prev = m_ref[...]
        m_new = jnp.maximum(m_prev, jnp.max(s_ref[...], axis=-1, keepdims=True))
        alpha = jnp.exp2(m_prev - m_new)
        m_ref[...] = m_new
        lsum = alpha * l_ref[...]
        acc = alpha * acc_ref[...]
        for j in range(ck // ATTN_SUB):
            p = jnp.exp2(s_ref[:, j * ATTN_SUB:(j + 1) * ATTN_SUB] - m_new)
            for t in range(ATTN_SUB // LANE):
                lsum += p[:, t * LANE:(t + 1) * LANE]
            acc += _dot(p.astype(BF16), v_ref[pl.ds(pl.multiple_of(c * ck + j * ATTN_SUB, ATTN_SUB), ATTN_SUB), :])
        l_ref[...] = lsum
        acc_ref[...] = acc

    bufs = (sa_ref, sb_ref)
    scores(0, sa_ref)
    group = 4 if (n_chunks - 1) % 4 == 0 else 2
    n_groups = (n_chunks - 1) // group

    def body(t, carry):
        c = group * t
        for u in range(group):
            scores(c + u + 1, bufs[(u + 1) % 2])
            update(c + u, bufs[u % 2])
        return carry

    if n_groups:
        lax.fori_loop(0, n_groups, body, 0)
    done = group * n_groups
    for c in range(done, n_chunks):
        if c + 1 < n_chunks:
            scores(c + 1, bufs[(c + 1) % 2])
        update(c, bufs[c % 2])
    o_ref[...] = (acc_ref[...] / jnp.sum(l_ref[...], axis=-1, keepdims=True)).astype(o_ref.dtype)


ATTN_SUB = 2 * LANE
OFFSET_LANE = MLA_ROPE
SCORE_CENTER = 64.0
SCORE_SUM_FLOOR = 2.0 ** -60


def _attn_tiles(tq_total, tk_total):
    tq = min(1024, tq_total)
    ck = tk_total
    for cand in (1280, 1024, 512, 256):
        if tk_total % cand == 0:
            ck = cand
            break
    return tq, ck


def _attention(q, k, v):
    tq_total, tk_total = q.shape[0], k.shape[0]
    tq, ck = _attn_tiles(tq_total, tk_total)
    return pl.pallas_call(
        functools.partial(_attn_kernel, ck=ck, n_chunks=tk_total // ck),
        out_shape=jax.ShapeDtypeStruct((tq_total, MLA_HEADS * MLA_V), BF16),
        grid=(MLA_HEADS, tq_total // tq),
        in_specs=[pl.BlockSpec((tq, MLA_HEAD_PAD), lambda h, i: (i, h)),
                  pl.BlockSpec((tk_total, MLA_HEAD_PAD), lambda h, i: (0, h), pipeline_mode=pl.Buffered(1)),
                  pl.BlockSpec((tk_total, MLA_V), lambda h, i: (0, h), pipeline_mode=pl.Buffered(1))],
        out_specs=pl.BlockSpec((tq, MLA_V), lambda h, i: (i, h)),
        scratch_shapes=[pltpu.VMEM((tq, ck), F32), pltpu.VMEM((tq, ck), F32),
                        pltpu.VMEM((tq, 1), F32), pltpu.VMEM((tq, LANE), F32), pltpu.VMEM((tq, MLA_V), F32)],
        compiler_params=_cp("parallel", "arbitrary"),
        name="mla_attention",
    )(q, k, v)


CONV_TILE = 512
HALO = 16


def _conv_kernel(gb_ref, gc_ref, u_ref, gcp_ref, up_ref, gcn_ref, un_ref, w_ref, b_ref, o_ref):
    i = pl.program_id(0)
    z = gc_ref[...].astype(F32) * u_ref[...].astype(F32)
    tm = z.shape[0]
    z_before = gcp_ref[HALO - 1:HALO, :].astype(F32) * up_ref[HALO - 1:HALO, :].astype(F32)
    z_after = gcn_ref[0:1, :].astype(F32) * un_ref[0:1, :].astype(F32)
    z_before = jnp.where(i > 0, z_before, 0.0)
    z_after = jnp.where(i < pl.num_programs(0) - 1, z_after, 0.0)
    rowid = lax.broadcasted_iota(jnp.int32, z.shape, 0)
    z_prev = jnp.where(rowid == 0, z_before, pltpu.roll(z, 1, axis=0))
    z_next = jnp.where(rowid == tm - 1, z_after, pltpu.roll(z, tm - 1, axis=0))
    w = w_ref[...]
    y = z_prev * w[0:1] + z * w[1:2] + z_next * w[2:3] + b_ref[...]
    o_ref[...] = (gb_ref[...].astype(F32) * y).astype(o_ref.dtype)


def _conv(px, col0, conv_w, conv_b):
    m = px.shape[0]
    tm = min(512, m)
    cb = CONV_WIDTH // CONV_TILE
    b0 = col0 // CONV_TILE
    nhalo = m // HALO
    main = lambda off: pl.BlockSpec((tm, CONV_TILE), lambda i, j: (i, b0 + off * cb + j))
    prev = lambda off: pl.BlockSpec(
        (HALO, CONV_TILE), lambda i, j: (jnp.maximum(i * (tm // HALO) - 1, 0), b0 + off * cb + j))
    nxt = lambda off: pl.BlockSpec(
        (HALO, CONV_TILE), lambda i, j: (jnp.minimum((i + 1) * (tm // HALO), nhalo - 1), b0 + off * cb + j))
    return pl.pallas_call(
        _conv_kernel,
        out_shape=jax.ShapeDtypeStruct((m, CONV_WIDTH), BF16),
        grid=(m // tm, cb),
        in_specs=[main(0), main(1), main(2), prev(1), prev(2), nxt(1), nxt(2),
                  pl.BlockSpec((3, CONV_TILE), lambda i, j: (0, j)),
                  pl.BlockSpec((1, CONV_TILE), lambda i, j: (0, j))],
        out_specs=pl.BlockSpec((tm, CONV_TILE), lambda i, j: (i, j)),
        compiler_params=_cp("parallel", "parallel"),
        name="gated_conv3",
    )(px, px, px, px, px, px, px, conv_w, conv_b)


def _moe_up_kernel(h_ref, wg_ref, wu_ref, gates_ref, o_ref):
    e = pl.program_id(1)
    h = h_ref[...]
    a = _dot(h, wg_ref[...])
    u = _dot(h, wu_ref[...])
    gates = gates_ref[...]
    lane = lax.broadcasted_iota(jnp.int32, gates.shape, 1)
    g = jnp.sum(jnp.where(lane == e, gates, 0.0), axis=-1, keepdims=True)
    o_ref[...] = (a * jax.nn.sigmoid(a) * u * g).astype(o_ref.dtype)


def _moe_up(h, wg, wu, gates):
    m, d = h.shape
    tm = min(1024, m)
    return pl.pallas_call(
        _moe_up_kernel,
        out_shape=jax.ShapeDtypeStruct((m, N_EXPERTS * D_FF), BF16),
        grid=(m // tm, N_EXPERTS),
        in_specs=[pl.BlockSpec((tm, d), lambda i, e: (i, 0)),
                  pl.BlockSpec((None, d, D_FF), lambda i, e: (e, 0, 0)),
                  pl.BlockSpec((None, d, D_FF), lambda i, e: (e, 0, 0)),
                  pl.BlockSpec((tm, LANE), lambda i, e: (i, 0))],
        out_specs=pl.BlockSpec((tm, D_FF), lambda i, e: (i, e)),
        compiler_params=_cp("parallel", "arbitrary"),
        name="moe_up",
    )(h, wg, wu, gates)


MOE_TOK_TILE = 256
MOE_ROW_TILE = 256


def _row_copy(src, src_row, dst, dst_row, sem):
    return pltpu.make_async_copy(src.at[pl.ds(src_row, 1)], dst.at[pl.ds(dst_row, 1)], sem)


def _moe_dispatch_kernel(pos_ref, h_ref, init_ref, xs_ref, sem):
    del init_ref
    tm = h_ref.shape[0]

    def issue(r, carry):
        for k in range(2):
            _row_copy(h_ref, r, xs_ref, pos_ref[0, 2 * r + k], sem).start()
        return carry

    def drain(r, carry):
        for k in range(2):
            _row_copy(h_ref, r, xs_ref, pos_ref[0, 2 * r + k], sem).wait()
        return carry

    lax.fori_loop(0, tm, issue, 0, unroll=8)
    lax.fori_loop(0, tm, drain, 0, unroll=8)


def _moe_dispatch(h_packed, pos, n_rows):
    m, w = h_packed.shape
    tm = MOE_TOK_TILE
    return pl.pallas_call(
        _moe_dispatch_kernel,
        out_shape=jax.ShapeDtypeStruct((n_rows, w), jnp.uint32),
        grid=(m // tm,),
        in_specs=[pl.BlockSpec((None, 1, 2 * tm), lambda i: (i, 0, 0), memory_space=pltpu.SMEM),
                  pl.BlockSpec((tm, w), lambda i: (i, 0)),
                  pl.BlockSpec(memory_space=pl.ANY)],
        out_specs=pl.BlockSpec(memory_space=pl.ANY),
        scratch_shapes=[pltpu.SemaphoreType.DMA],
        input_output_aliases={2: 0},
        compiler_params=_cp("arbitrary"),
        name="moe_dispatch",
    )(pos, h_packed, jnp.zeros((n_rows, w), jnp.uint32))


def _moe_experts_kernel(te_ref, xs_ref, wg_ref, wu_ref, wd_ref, ys_ref):
    del te_ref
    half = xs_ref.shape[1]
    x1, x2 = _unpack_bf16_pairs(xs_ref[...])
    x1, x2 = x1.astype(BF16), x2.astype(BF16)
    a = _dot(x1, wg_ref[:half, :]) + _dot(x2, wg_ref[half:, :])
    u = _dot(x1, wu_ref[:half, :]) + _dot(x2, wu_ref[half:, :])
    hid = (a * jax.nn.sigmoid(a) * u).astype(BF16)
    ys_ref[...] = _pack_bf16_pairs(_dot(hid, wd_ref[...]))


def _moe_experts(tile_expert, xs, wg, wu, wd):
    n_rows, w = xs.shape
    d = 2 * w
    t = MOE_ROW_TILE
    return pl.pallas_call(
        _moe_experts_kernel,
        out_shape=jax.ShapeDtypeStruct((n_rows, w), jnp.uint32),
        grid_spec=pltpu.PrefetchScalarGridSpec(
            num_scalar_prefetch=1, grid=(n_rows // t,),
            in_specs=[pl.BlockSpec((t, w), lambda i, te: (i, 0)),
                      pl.BlockSpec((None, d, D_FF), lambda i, te: (te[i], 0, 0)),
                      pl.BlockSpec((None, d, D_FF), lambda i, te: (te[i], 0, 0)),
                      pl.BlockSpec((None, D_FF, d), lambda i, te: (te[i], 0, 0))],
            out_specs=pl.BlockSpec((t, w), lambda i, te: (i, 0))),
        compiler_params=_cp("arbitrary"),
        name="moe_experts",
    )(tile_expert, xs, wg, wu, wd)


def _moe_combine_kernel(pos_ref, posn_ref, x_ref, gate_ref, rec_ref, ys_ref, o_ref, g1_ref, g2_ref, sem):
    _, tm, half = g1_ref.shape
    bufs = (g1_ref, g2_ref)
    i = pl.program_id(0)
    slot = i % 2

    def gather(p_ref, s, start):
        def one(r, carry):
            for k in range(2):
                cp = _row_copy(ys_ref, p_ref[0, 2 * r + k], bufs[k].at[s], r, sem.at[s])
                cp.start() if start else cp.wait()
            return carry

        lax.fori_loop(0, tm, one, 0, unroll=8)

    @pl.when(i == 0)
    def _():
        gather(pos_ref, slot, True)

    @pl.when(i + 1 < pl.num_programs(0))
    def _():
        gather(posn_ref, 1 - slot, True)

    gather(pos_ref, slot, False)
    rb = 2 * SUBLANE

    def combine(b, carry):
        rs = pl.ds(pl.multiple_of(b * rb, rb), rb)
        rec = rec_ref[rs, :]
        w1, w2 = rec[:, INFO_W1:INFO_W1 + 1], rec[:, INFO_W2:INFO_W2 + 1]
        y1 = _unpack_bf16_pairs(g1_ref[slot, rs, :])
        y2 = _unpack_bf16_pairs(g2_ref[slot, rs, :])
        for part in range(2):
            cols = slice(part * half, (part + 1) * half)
            o_ref[rs, cols] = x_ref[rs, cols] + gate_ref[:, cols] * (w1 * y1[part] + w2 * y2[part])
        return carry

    lax.fori_loop(0, tm // rb, combine, 0)


def _moe_combine(pos, xres, gate, rec, ys):
    m, d = xres.shape
    tm = MOE_TOK_TILE
    last = m // tm - 1
    return pl.pallas_call(
        _moe_combine_kernel,
        out_shape=jax.ShapeDtypeStruct((m, d), F32),
        grid=(m // tm,),
        in_specs=[pl.BlockSpec((None, 1, 2 * tm), lambda i: (i, 0, 0), memory_space=pltpu.SMEM),
                  pl.BlockSpec((None, 1, 2 * tm), lambda i: (jnp.minimum(i + 1, last), 0, 0),
                               memory_space=pltpu.SMEM),
                  pl.BlockSpec((tm, d), lambda i: (i, 0)),
                  pl.BlockSpec((1, d), lambda i: (0, 0)),
                  pl.BlockSpec((tm, LANE), lambda i: (i, 0)),
                  pl.BlockSpec(memory_space=pl.ANY)],
        out_specs=pl.BlockSpec((tm, d), lambda i: (i, 0)),
        scratch_shapes=[pltpu.VMEM((2, tm, d // 2), jnp.uint32), pltpu.VMEM((2, tm, d // 2), jnp.uint32),
                        pltpu.SemaphoreType.DMA((2,))],
        compiler_params=_cp("arbitrary"),
        name="moe_combine",
    )(pos, pos, xres, gate, rec, ys)


def _moe_routed(xres, gain, shift, scale, gate, router, wg, wu, wd3):
    m = xres.shape[0]
    tm, t = MOE_TOK_TILE, MOE_ROW_TILE
    h_packed, rec, counts = _modulate(xres, gain, shift, scale, router=router, route="routed")
    n_rows = 2 * m + N_EXPERTS * t
    cnt = counts[0, :N_EXPERTS].astype(jnp.int32)
    seg = ((cnt + t - 1) // t) * t
    seg_end = jnp.cumsum(seg)
    seg_start = seg_end - seg
    e12 = rec[:, INFO_E1:INFO_E2 + 1].astype(jnp.int32)
    r12 = rec[:, INFO_R1:INFO_R2 + 1].astype(jnp.int32)
    pos = (seg_start[e12] + r12).reshape(m // tm, 1, 2 * tm)
    tile_start = jnp.arange(n_rows // t, dtype=jnp.int32) * t
    tile_expert = jnp.minimum(jnp.sum(tile_start[:, None] >= seg_end[None, :], axis=1), N_EXPERTS - 1).astype(jnp.int32)
    xs = _moe_dispatch(h_packed, pos, n_rows)
    ys = _moe_experts(tile_expert, xs, wg, wu, wd3)
    return _moe_combine(pos, xres, gate, rec, ys)


GLA_LEVELS = 6
GLA_MILD_DECAY = 50.0


def _gla_tables(reverse):
    n = GLA_CHUNK
    t = np.arange(n)[:, None]
    r = np.arange(n)[None, :]
    blocks = [r <= t, r > t]
    masks = []
    for lvl in range(GLA_LEVELS):
        bs = 1 << lvl
        blocks.append((r >= (t & ~(bs - 1))) & (r <= t))
        blocks.append((r > t) & (r <= (t | (bs - 1))))
        masks.append(((t >> (lvl + 1)) == (r >> (lvl + 1))) & (((t >> lvl) & 1) == 1) & (((r >> lvl) & 1) == 0))
    masks.append(t == r)
    masks.append(r <= t)
    e = np.stack(blocks).astype(np.float32)
    msk = np.stack(masks).astype(np.float32)
    if reverse:
        e, msk = e[:, ::-1, ::-1], msk[:, ::-1, ::-1]
    return jnp.asarray(e.reshape(-1, n)), jnp.asarray(msk)


def _gla_intra_kernel(k_ref, v_ref, q_ref, a_ref, wa_ref, ba_ref, e_ref, m_ref,
                      oi_ref, qd_ref, kd_ref, dl_ref, la_ref, tot_ref, *, nchunk, reverse):
    n = GLA_CHUNK
    scale = GLA_DK ** -0.5

    a = a_ref[...]
    z = _dot(a, wa_ref[0]) + _dot(a, wa_ref[1]) + _dot(a, wa_ref[2]) + ba_ref[...]
    la = (jnp.minimum(z, 0.0) - jnp.log1p(jnp.exp(-jnp.abs(z)))) * (1.0 / GLA_TEMP)
    la_ref[...] = la
    totals = jnp.sum(la.reshape(nchunk, n, GLA_DK), axis=1)
    tot_ref[...] = totals
    dl_ref[...] = jnp.exp(totals)
    mild = jnp.min(totals) > -GLA_MILD_DECAY
    rowid = lax.broadcasted_iota(jnp.int32, (n, GLA_DK), 0)

    def load(c):
        rows = pl.ds(pl.multiple_of(c * n, n), n)
        return rows, q_ref[rows, :].astype(F32) * scale, k_ref[rows, :].astype(F32)

    def mild_chunk(c, carry):
        rows, q, k = load(c)
        la_c = la_ref[rows, :]
        total = tot_ref[pl.ds(c, 1), :]
        cum = la_c
        shift = 1
        while shift < n:
            cum = cum + jnp.where(rowid >= shift, pltpu.roll(cum, shift, axis=0), 0.0)
            shift *= 2
        if reverse:
            cum = total - cum + la_c
        qd = (q * jnp.exp(cum)).astype(BF16)
        kinv = k * jnp.exp(-cum)
        qd_ref[rows, :] = qd
        kd_ref[rows, :] = (kinv * jnp.exp(total)).astype(BF16)
        att = m_ref[GLA_LEVELS + 1] * _dot_nt(qd, kinv.astype(BF16))
        oi_ref[rows, :] = _dot(att.astype(BF16), v_ref[rows, :])
        return carry

    def harsh_chunk(c, carry):
        rows, q, k = load(c)
        ex = _dot_f32(e_ref[...], la_ref[rows, :])
        qd_ref[rows, :] = (q * jnp.exp(ex[0:n])).astype(BF16)
        kd_ref[rows, :] = (k * jnp.exp(ex[n:2 * n])).astype(BF16)
        att = m_ref[GLA_LEVELS] * _dot_nt(q.astype(BF16), k.astype(BF16))
        for lvl in range(GLA_LEVELS):
            qs = (q * jnp.exp(ex[(2 + 2 * lvl) * n:(3 + 2 * lvl) * n])).astype(BF16)
            ks = (k * jnp.exp(ex[(3 + 2 * lvl) * n:(4 + 2 * lvl) * n])).astype(BF16)
            att += m_ref[lvl] * _dot_nt(qs, ks)
        oi_ref[rows, :] = _dot(att.astype(BF16), v_ref[rows, :])
        return carry

    @pl.when(mild)
    def _():
        lax.fori_loop(0, nchunk, mild_chunk, 0, unroll=2)

    @pl.when(jnp.logical_not(mild))
    def _():
        lax.fori_loop(0, nchunk, harsh_chunk, 0)


def _gla_blocks(m):
    tb = min(512, m)
    return tb, tb // GLA_CHUNK


def _gla_intra(px, cols, wa_pad, ba, reverse):
    m = px.shape[0]
    tb, nchunk = _gla_blocks(m)
    kb, vb, qb, ab = cols
    e, msk = _gla_tables(reverse)
    return pl.pallas_call(
        functools.partial(_gla_intra_kernel, nchunk=nchunk, reverse=reverse),
        out_shape=[jax.ShapeDtypeStruct((m, GLA_VAL), F32), jax.ShapeDtypeStruct((m, GLA_KEY), BF16),
                   jax.ShapeDtypeStruct((m, GLA_KEY), BF16), jax.ShapeDtypeStruct((m // GLA_CHUNK, GLA_KEY), F32)],
        grid=(m // tb, GLA_HEADS),
        in_specs=[pl.BlockSpec((tb, GLA_DK), lambda i, h: (i, kb + h)),
                  pl.BlockSpec((tb, GLA_DV), lambda i, h: (i, vb + h)),
                  pl.BlockSpec((tb, GLA_DK), lambda i, h: (i, qb + h)),
                  pl.BlockSpec((tb, LANE), lambda i, h: (i, ab)),
                  pl.BlockSpec((3, LANE, GLA_DK), lambda i, h: (0, 0, h)),
                  pl.BlockSpec((1, GLA_DK), lambda i, h: (0, h)),
                  pl.BlockSpec(e.shape, lambda i, h: (0, 0)),
                  pl.BlockSpec(msk.shape, lambda i, h: (0, 0, 0))],
        out_specs=[pl.BlockSpec((tb, GLA_DV), lambda i, h: (i, h)),
                   pl.BlockSpec((tb, GLA_DK), lambda i, h: (i, h)),
                   pl.BlockSpec((tb, GLA_DK), lambda i, h: (i, h)),
                   pl.BlockSpec((nchunk, GLA_DK), lambda i, h: (i, h))],
        scratch_shapes=[pltpu.VMEM((tb, GLA_DK), F32), pltpu.VMEM((nchunk, GLA_DK), F32)],
        compiler_params=_cp("parallel", "parallel"),
        name="gla_intra_bwd" if reverse else "gla_intra_fwd",
    )(px, px, px, px, wa_pad, ba, e, msk)


def _gla_scan_kernel(qd_ref, kd_ref, v_ref, dl_ref, oi_ref, s0_ref, o_ref, sf_ref, s_ref, *, nchunk, reverse):
    i = pl.program_id(1)
    n = GLA_CHUNK

    @pl.when(i == 0)
    def _():
        s_ref[...] = s0_ref[...]

    def chunk(cc, carry):
        c = (nchunk - 1 - cc) if reverse else cc
        rows = pl.ds(pl.multiple_of(c * n, n), n)
        st = s_ref[...]
        o_ref[rows, :] = oi_ref[rows, :] + _dot_nt(qd_ref[rows, :], st.astype(BF16))
        s_ref[...] = st * dl_ref[pl.ds(c, 1), :] + _dot_tn(v_ref[rows, :], kd_ref[rows, :])
        return carry

    lax.fori_loop(0, nchunk, chunk, 0, unroll=True)

    @pl.when(i == pl.num_programs(1) - 1)
    def _():
        sf_ref[...] = s_ref[...]


def _gla_scan(px, vb, qd, kd, dl, oi, s0, reverse):
    m = px.shape[0]
    tb, nchunk = _gla_blocks(m)
    nb = m // tb
    blk = (lambda i: nb - 1 - i) if reverse else (lambda i: i)
    return pl.pallas_call(
        functools.partial(_gla_scan_kernel, nchunk=nchunk, reverse=reverse),
        out_shape=[jax.ShapeDtypeStruct((m, GLA_VAL), F32),
                   jax.ShapeDtypeStruct((GLA_HEADS, GLA_DV, GLA_DK), F32)],
        grid=(GLA_HEADS, nb),
        in_specs=[pl.BlockSpec((tb, GLA_DK), lambda h, i: (blk(i), h)),
                  pl.BlockSpec((tb, GLA_DK), lambda h, i: (blk(i), h)),
                  pl.BlockSpec((tb, GLA_DV), lambda h, i: (blk(i), vb + h)),
                  pl.BlockSpec((nchunk, GLA_DK), lambda h, i: (blk(i), h)),
                  pl.BlockSpec((tb, GLA_DV), lambda h, i: (blk(i), h)),
                  pl.BlockSpec((None, GLA_DV, GLA_DK), lambda h, i: (h, 0, 0))],
        out_specs=[pl.BlockSpec((tb, GLA_DV), lambda h, i: (blk(i), h)),
                   pl.BlockSpec((None, GLA_DV, GLA_DK), lambda h, i: (h, 0, 0))],
        scratch_shapes=[pltpu.VMEM((GLA_DV, GLA_DK), F32)],
        compiler_params=_cp("parallel", "arbitrary"),
        name="gla_scan_bwd" if reverse else "gla_scan_fwd",
    )(qd, kd, px, dl, oi, s0)


def _gla_gate_kernel(of_ref, ob_ref, g_ref, gain_ref, o_ref):
    gain = gain_ref[...]
    for h in range(GLA_HEADS):
        sl = slice(h * GLA_DV, (h + 1) * GLA_DV)
        o = _rms(of_ref[:, sl] + ob_ref[:, sl]) * gain
        g = g_ref[:, sl].astype(F32)
        o_ref[:, sl] = (o * (g * jax.nn.sigmoid(g))).astype(BF16)


def _gla_gate(o_f, o_b, px, g_blk, gain):
    m = o_f.shape[0]
    tm = min(256, m)
    return pl.pallas_call(
        _gla_gate_kernel,
        out_shape=jax.ShapeDtypeStruct((m, GLA_VAL), BF16),
        grid=(m // tm,),
        in_specs=[pl.BlockSpec((tm, GLA_VAL), lambda i: (i, 0)), pl.BlockSpec((tm, GLA_VAL), lambda i: (i, 0)),
                  pl.BlockSpec((tm, GLA_VAL), lambda i: (i, g_blk)), pl.BlockSpec((1, GLA_DV), lambda i: (0, 0))],
        out_specs=pl.BlockSpec((tm, GLA_VAL), lambda i: (i, 0)),
        compiler_params=_cp("parallel"),
        name="gla_gate",
    )(o_f, o_b, px, gain)


def _pad_cols(w, mult):
    pad = (-w.shape[1]) % mult
    return jnp.pad(w, ((0, 0), (0, pad))) if pad else w


def _row(v, width=None):
    v = v.reshape(1, -1).astype(F32)
    return _pad_cols(v, width) if width else v


MM_TN = 512

AB_CKV_BLK = MLA_Q_RANK // MLA_KV_RANK
AB_CONV_COL = MLA_Q_RANK + MLA_KV_RANK
AB_KR_BLK = (AB_CONV_COL + 3 * CONV_WIDTH) // LANE
GLA_K_BLK = 0
GLA_V_BLK = GLA_KEY // GLA_DV
GLA_Q_BLK = (GLA_KEY + GLA_VAL) // GLA_DK
GLA_G_BLK = (2 * GLA_KEY + GLA_VAL) // GLA_VAL
GLA_A_BLK = (2 * GLA_KEY + 2 * GLA_VAL) // LANE


def _ab_w_in(w):
    ckv = w[:, :MLA_KV_RANK]
    kr = w[:, MLA_KV_RANK:MLA_KV_RANK + MLA_ROPE]
    cq = w[:, MLA_KV_RANK + MLA_ROPE:MLA_KV_RANK + MLA_ROPE + MLA_Q_RANK]
    conv = w[:, MLA_KV_RANK + MLA_ROPE + MLA_Q_RANK:]
    return _pad_cols(jnp.concatenate([cq, ckv, conv, kr], axis=1), MM_TN).astype(BF16)


def _ab_w_uq(w):
    w = w.reshape(MLA_Q_RANK, MLA_HEADS, MLA_QK)
    w = jnp.pad(w, ((0, 0), (0, 0), (0, MLA_HEAD_PAD - MLA_QK)))
    return w.reshape(MLA_Q_RANK, MLA_HEADS * MLA_HEAD_PAD).astype(BF16)


def _gla_w_in(w):
    o1, o2 = GLA_KEY, GLA_KEY + GLA_VAL
    o3 = o2 + 2 * GLA_RANK
    k, v, a, q, g = w[:, :o1], w[:, o1:o2], w[:, o2:o3], w[:, o3:o3 + GLA_KEY], w[:, o3 + GLA_KEY:]
    return _pad_cols(jnp.concatenate([k, v, q, g, a], axis=1), MM_TN).astype(BF16)


def _gla_wa_pad(w_a, offset):
    w = jnp.zeros((LANE, GLA_KEY), F32).at[offset:offset + GLA_RANK].set(w_a.astype(F32))
    hi = w.astype(BF16)
    r1 = w - hi.astype(F32)
    mid = r1.astype(BF16)
    lo = (r1 - mid.astype(F32)).astype(BF16)
    return jnp.stack([hi, mid, lo])


def _rope_tables(pos_rows, pos_cols):
    half = MLA_ROPE // 4
    inv_freq = ROPE_BASE ** (-jnp.arange(half, dtype=F32) / half)
    ar = pos_rows.astype(F32)[:, None] * inv_freq[None, :]
    ac = pos_cols.astype(F32)[:, None] * inv_freq[None, :]
    zeros = jnp.zeros((ar.shape[0], LANE - MLA_ROPE), F32)
    cos = jnp.concatenate([jnp.cos(ar), jnp.cos(ar), jnp.cos(ac), jnp.cos(ac), zeros], axis=1)
    sin = jnp.concatenate([-jnp.sin(ar), jnp.sin(ar), -jnp.sin(ac), jnp.sin(ac), zeros], axis=1)
    return cos, sin


def _mixer_mla_conv(hx, hc, p, rope_x, rope_c, need_ctx):
    w_in, w_uq, w_ukv, w_out = p["w_in"], p["w_uq"], p["w_ukv"], p["w_out"]

    def project(h, rope):
        px = _mm([(h, w_in)], out_dtype=BF16, name="ab_in_proj")
        k, v, kk = _kv_up(px, AB_CKV_BLK, AB_KR_BLK, p["kv_a_gain"], w_ukv, p["k_gain_n"], p["k_gain_r"], *rope)
        return px, k, v, _colmax(kk)

    def branch_out(px, k, v, kmax, rope):
        q = _q_up(px, p["q_a_gain"], w_uq, p["q_gain_n"], p["q_gain_r"], *rope, kmax)
        o = _attention(q, k, v)
        y = _conv(px, AB_CONV_COL, p["conv_w"], p["conv_b"])
        return o, y

    px_c, k_c, v_c, kmax_c = project(hc, rope_c)
    px_x, k_x, v_x, kmax_x = project(hx, rope_x)
    k_all = jnp.concatenate([k_c, k_x], axis=0)
    v_all = jnp.concatenate([v_c, v_x], axis=0)
    out_x = branch_out(px_x, k_all, v_all, jnp.maximum(kmax_c, kmax_x), rope_x)
    out_c = branch_out(px_c, k_c, v_c, kmax_c, rope_c) if need_ctx else None
    half = MLA_HEADS * MLA_V
    return out_x, out_c, (w_out[:half], w_out[half:])


def _mixer_gla(hx, hc, p, need_ctx):
    cols = (GLA_K_BLK, GLA_V_BLK, GLA_Q_BLK, GLA_A_BLK)
    zero_state = jnp.zeros((GLA_HEADS, GLA_DV, GLA_DK), F32)

    def run(h, s_f, s_b):
        px = _mm([(h, p["w_in"])], out_dtype=BF16, name="gla_in_proj")
        outs, states = [], []
        for reverse, wa, ba, s0 in ((False, p["wa_f"], p["ba_f"], s_f), (True, p["wa_b"], p["ba_b"], s_b)):
            oi, qd, kd, dl = _gla_intra(px, cols, wa, ba, reverse)
            o, s_fin = _gla_scan(px, GLA_V_BLK, qd, kd, dl, oi, s0, reverse)
            outs.append(o)
            states.append(s_fin)
        return px, outs, states

    px_c, o_c, (s_f, s_b) = run(hc, zero_state, zero_state)
    px_x, o_x, _ = run(hx, s_f, s_b)
    out_x = _gla_gate(o_x[0], o_x[1], px_x, GLA_G_BLK, p["o_gain"])
    out_c = _gla_gate(o_c[0], o_c[1], px_c, GLA_G_BLK, p["o_gain"]) if need_ctx else None
    return out_x, out_c


def _moe_dense(xres, gain, shift, scale, gate, router, wg, wu, wd3):
    h, gates = _modulate(xres, gain, shift, scale, router=router, route="dense")
    hid = _moe_up(h, wg, wu, gates)
    wd = wd3.reshape(N_EXPERTS * D_FF, wd3.shape[-1])
    return _mm([(hid, wd)], out_dtype=F32, residual=(xres, gate), name="moe_down")


def _moe(xres, *args):
    routed = xres.shape[0] >= 4 * MOE_ROW_TILE and xres.shape[0] % MOE_TOK_TILE == 0
    return (_moe_routed if routed else _moe_dense)(xres, *args)


def kernel(x, c, ctx, c_ctx, ada_w, ada_b, norm_mix_gain, norm_ffn_gain, ab_w_in, ab_q_a_gain, ab_w_uq, ab_q_gain, ab_kv_a_gain, ab_w_ukv, ab_k_gain, ab_conv_w, ab_conv_b, ab_w_out, gla_w_in, gla_w_af, gla_b_af, gla_w_ab, gla_b_ab, gla_o_gain, gla_w_out, router_w, router_b, moe_w_gate, moe_w_up, moe_w_down):
    b, s, d = x.shape
    assert b == 1 and s % GRID_W == 0
    depth = ada_w.shape[0]
    xs, cx = x[0], ctx[0]
    n_ctx = cx.shape[0]

    mod = _adaln(c.reshape(d, 1), c_ctx.reshape(d, 1), ada_w, ada_b).reshape(depth, 2, 6, 1, d)
    tok = jnp.arange(s, dtype=jnp.int32)
    rope_x = _rope_tables(tok // GRID_W, tok % GRID_W)
    rope_c = (jnp.concatenate([jnp.ones((n_ctx, MLA_ROPE), F32), jnp.zeros((n_ctx, LANE - MLA_ROPE), F32)], axis=1),
              jnp.zeros((n_ctx, LANE), F32))
    router = (_pad_cols(router_w.astype(F32), LANE), router_b.reshape(N_EXPERTS, 1).astype(F32))

    for i in range(depth):
        last = i == depth - 1
        j = i // 2
        mx, mc = mod[i, 0], mod[i, 1]
        gain_mix = _row(norm_mix_gain[i])
        hx = _modulate(xs, gain_mix, mx[0], mx[1])
        hc = _modulate(cx, gain_mix, mc[0], mc[1])
        if i % 2 == 0:
            p = dict(
                w_in=_ab_w_in(ab_w_in[j]), w_uq=_ab_w_uq(ab_w_uq[j]), w_ukv=ab_w_ukv[j].astype(BF16),
                w_out=ab_w_out[j].astype(BF16),
                q_a_gain=_row(ab_q_a_gain[j]), kv_a_gain=_row(ab_kv_a_gain[j]),
                q_gain_n=_row(ab_q_gain[j][:MLA_NOPE]), q_gain_r=_row(ab_q_gain[j][MLA_NOPE:], LANE),
                k_gain_n=_row(ab_k_gain[j][:MLA_NOPE]), k_gain_r=_row(ab_k_gain[j][MLA_NOPE:], LANE),
                conv_w=ab_conv_w[j].astype(F32), conv_b=_row(ab_conv_b[j]))
            (o_x, y_x), out_c, (w_o, w_y) = _mixer_mla_conv(hx, hc, p, rope_x, rope_c, not last)
            xs = _mm([(o_x, w_o), (y_x, w_y)], out_dtype=F32, residual=(xs, mx[2]), name="ab_out_proj")
            if not last:
                cx = _mm([(out_c[0], w_o), (out_c[1], w_y)], out_dtype=F32, residual=(cx, mc[2]), name="ab_out_proj")
        else:
            p = dict(
                w_in=_gla_w_in(gla_w_in[j]),
                wa_f=_gla_wa_pad(gla_w_af[j], 0), wa_b=_gla_wa_pad(gla_w_ab[j], GLA_RANK),
                ba_f=_row(gla_b_af[j]), ba_b=_row(gla_b_ab[j]), o_gain=_row(gla_o_gain[j]))
            w_out = gla_w_out[j].astype(BF16)
            y_x, y_c = _mixer_gla(hx, hc, p, not last)
            xs = _mm([(y_x, w_out)], out_dtype=F32, residual=(xs, mx[2]), name="gla_out_proj")
            if not last:
                cx = _mm([(y_c, w_out)], out_dtype=F32, residual=(cx, mc[2]), name="gla_out_proj")
        gain_ffn = _row(norm_ffn_gain[i])
        wg, wu = moe_w_gate[i].astype(BF16), moe_w_up[i].astype(BF16)
        wd = moe_w_down[i].astype(BF16)
        xs = _moe(xs, gain_ffn, mx[3], mx[4], mx[5], router, wg, wu, wd)
        if not last:
            cx = _moe(cx, gain_ffn, mc[3], mc[4], mc[5], router, wg, wu, wd)
    return xs[None]
```

```python
import functools

import numpy as np
import jax
import jax.numpy as jnp
from jax import lax
from jax.experimental import pallas as pl
from jax.experimental.pallas import tpu as pltpu

F32 = jnp.float32
BF16 = jnp.bfloat16
HIGHEST = lax.Precision.HIGHEST

LANE = 128
SUBLANE = 8
VMEM_LIMIT_BYTES = 56 * 1024 * 1024

GRID_W = 64
EPS = 1e-6
MLA_HEADS = 16
MLA_Q_RANK = 1024
MLA_KV_RANK = 512
MLA_NOPE = 128
MLA_ROPE = 64
MLA_V = 128
MLA_QK = MLA_NOPE + MLA_ROPE
MLA_HEAD_PAD = 2 * LANE
ROPE_BASE = 10000.0
LOG2E = 1.4426950408889634
CONV_WIDTH = 2048
GLA_HEADS = 8
GLA_DK = 256
GLA_DV = 512
GLA_KEY = GLA_HEADS * GLA_DK
GLA_VAL = GLA_HEADS * GLA_DV
GLA_RANK = 16
GLA_TEMP = 16.0
GLA_CHUNK = 64
N_EXPERTS = 16
N_GROUPS = 4
GROUP_SIZE = N_EXPERTS // N_GROUPS
D_FF = 256


def _cp(*sem):
    return pltpu.CompilerParams(dimension_semantics=sem, vmem_limit_bytes=VMEM_LIMIT_BYTES)


def _dot(a, b):
    return jnp.dot(a, b, preferred_element_type=F32)


def _dot_nt(a, b):
    return lax.dot_general(a, b, (((1,), (1,)), ((), ())), preferred_element_type=F32)


def _dot_tn(a, b):
    return lax.dot_general(a, b, (((0,), (0,)), ((), ())), preferred_element_type=F32)


def _dot_f32(a, b):
    return jnp.dot(a, b, precision=HIGHEST, preferred_element_type=F32)


def _rms(v, n=None):
    ss = jnp.sum(v * v, axis=-1, keepdims=True)
    return v * lax.rsqrt(ss * (1.0 / (n if n is not None else v.shape[-1])) + EPS)


def _adaln_kernel(c_ref, cc_ref, w_ref, b_ref, o_ref, acc_ref):
    k = pl.program_id(2)

    @pl.when(k == 0)
    def _():
        acc_ref[...] = jnp.zeros_like(acc_ref)

    w = w_ref[...]
    tk, tn = w.shape
    for r, ref in enumerate((c_ref, cc_ref)):
        cv = ref[...]
        cs = cv * jax.nn.sigmoid(cv)
        acc_ref[r] += jnp.sum((w * cs).reshape(tk // SUBLANE, SUBLANE, tn), axis=0)

    @pl.when(k == pl.num_programs(2) - 1)
    def _():
        o_ref[...] = jnp.sum(acc_ref[...], axis=1) + b_ref[...]


def _adaln(c_col, cc_col, ada_w, ada_b):
    depth, d, n = ada_w.shape
    tk, tn = 1024, 2048
    return pl.pallas_call(
        _adaln_kernel,
        out_shape=jax.ShapeDtypeStruct((depth, 2, n), F32),
        grid=(depth, n // tn, d // tk),
        in_specs=[
            pl.BlockSpec((tk, 1), lambda l, j, k: (k, 0)),
            pl.BlockSpec((tk, 1), lambda l, j, k: (k, 0)),
            pl.BlockSpec((None, tk, tn), lambda l, j, k: (l, k, j)),
            pl.BlockSpec((None, 1, tn), lambda l, j, k: (l, 0, j)),
        ],
        out_specs=pl.BlockSpec((None, 2, tn), lambda l, j, k: (l, 0, j)),
        scratch_shapes=[pltpu.VMEM((2, SUBLANE, tn), F32)],
        compiler_params=_cp("parallel", "parallel", "arbitrary"),
        name="adaln_mod",
    )(c_col, cc_col, ada_w, ada_b.reshape(depth, 1, n))


def _route_topk(logits, bias_col):
    tm = logits.shape[0]
    s = jax.nn.sigmoid(logits.T[:N_EXPERTS])
    sel = s + bias_col
    row = [sel[e:e + 1] for e in range(N_EXPERTS)]
    score = []
    for g in range(N_GROUPS):
        a, b, c2, d2 = row[GROUP_SIZE * g:GROUP_SIZE * (g + 1)]
        hi1, lo1 = jnp.maximum(a, b), jnp.minimum(a, b)
        hi2, lo2 = jnp.maximum(c2, d2), jnp.minimum(c2, d2)
        score.append(jnp.maximum(hi1, hi2) + jnp.maximum(jnp.minimum(hi1, hi2), jnp.maximum(lo1, lo2)))
    best, gidx = score[0], jnp.zeros((1, tm), jnp.int32)
    for g in range(1, N_GROUPS):
        upd = score[g] > best
        gidx = jnp.where(upd, g, gidx)
        best = jnp.where(upd, score[g], best)
    neg = jnp.full((1, tm), -jnp.inf, F32)
    cand = [jnp.where(gidx == (e // GROUP_SIZE), row[e], neg) for e in range(N_EXPERTS)]

    def argmax_first(vals):
        bv, bi = vals[0], jnp.zeros((1, tm), jnp.int32)
        for e in range(1, N_EXPERTS):
            upd = vals[e] > bv
            bi = jnp.where(upd, e, bi)
            bv = jnp.where(upd, vals[e], bv)
        return bi

    i1 = argmax_first(cand)
    i2 = argmax_first([jnp.where(i1 == e, neg, cand[e]) for e in range(N_EXPERTS)])
    eidx = lax.broadcasted_iota(jnp.int32, (N_EXPERTS, tm), 0)
    w1 = jnp.sum(jnp.where(eidx == i1, s, 0.0), axis=0, keepdims=True)
    w2 = jnp.sum(jnp.where(eidx == i2, s, 0.0), axis=0, keepdims=True)
    den = w1 + w2
    return i1, i2, w1 / den, w2 / den


def _dense_gates(i1, i2, w1, w2):
    efull = lax.broadcasted_iota(jnp.int32, (LANE, i1.shape[1]), 0)
    return (jnp.where(efull == i1, w1, 0.0) + jnp.where(efull == i2, w2, 0.0)).T


INFO_E1, INFO_E2, INFO_W1, INFO_W2, INFO_R1, INFO_R2 = range(6)


def _routing_record(i1, i2, w1, w2, carry_ref):
    tm = i1.shape[1]
    efull = lax.broadcasted_iota(jnp.int32, (LANE, tm), 0)
    rec_t = (jnp.where(efull == INFO_E1, i1.astype(F32), 0.0) + jnp.where(efull == INFO_E2, i2.astype(F32), 0.0)
             + jnp.where(efull == INFO_W1, w1, 0.0) + jnp.where(efull == INFO_W2, w2, 0.0))
    rec = rec_t.T
    onehot = jnp.where((efull == i1) | (efull == i2), 1.0, 0.0).T
    r = lax.broadcasted_iota(jnp.int32, (tm, tm), 0)
    c = lax.broadcasted_iota(jnp.int32, (tm, tm), 1)
    incl = _dot(jnp.where(r >= c, 1.0, 0.0).astype(BF16), onehot.astype(BF16)) + carry_ref[...]
    carry_ref[...] = incl[tm - 1:tm, :]
    rank = incl - onehot
    lane = lax.broadcasted_iota(jnp.int32, (tm, LANE), 1)
    lane_f = lane.astype(F32)
    r1 = jnp.sum(jnp.where(lane_f == rec[:, INFO_E1:INFO_E1 + 1], rank, 0.0), axis=-1, keepdims=True)
    r2 = jnp.sum(jnp.where(lane_f == rec[:, INFO_E2:INFO_E2 + 1], rank, 0.0), axis=-1, keepdims=True)
    return jnp.where(lane == INFO_R1, r1, jnp.where(lane == INFO_R2, r2, rec))


def _pack_bf16_pairs(v):
    n = v.shape[1] // 2
    hi = lax.bitcast_convert_type(v[:, :n].astype(BF16).astype(F32), jnp.uint32)
    lo = lax.bitcast_convert_type(v[:, n:].astype(BF16).astype(F32), jnp.uint32)
    return hi | (lo >> 16)


def _unpack_bf16_pairs(w):
    first = lax.bitcast_convert_type(w & jnp.uint32(0xFFFF0000), F32)
    second = lax.bitcast_convert_type(w << 16, F32)
    return first, second


def _modulate_kernel(x_ref, g_ref, sh_ref, sc_ref, *rest, route):
    x = x_ref[...]
    h = _rms(x) * g_ref[...] * (1.0 + sc_ref[...]) + sh_ref[...]
    if route is None:
        (h_ref,) = rest
        h_ref[...] = h.astype(h_ref.dtype)
        return
    wr_ref, br_ref = rest[:2]
    h_hi = h.astype(BF16)
    h_lo = (h - h_hi.astype(F32)).astype(BF16)
    logits = _dot(h_hi, wr_ref[0]) + (_dot(h_lo, wr_ref[0]) + _dot(h_hi, wr_ref[1]))
    topk = _route_topk(logits, br_ref[...])
    if route == "dense":
        h_ref, gates_ref = rest[2:]
        h_ref[...] = h.astype(h_ref.dtype)
        gates_ref[...] = _dense_gates(*topk)
    else:
        h_ref, rec_ref, cnt_ref, carry_ref = rest[2:]

        @pl.when(pl.program_id(0) == 0)
        def _():
            carry_ref[...] = jnp.zeros_like(carry_ref)

        h_ref[...] = _pack_bf16_pairs(h)
        rec_ref[...] = _routing_record(*topk, carry_ref)
        cnt_ref[...] = carry_ref[...]


def _modulate(x, gain, shift, scale, router=None, route=None):
    m, d = x.shape
    tm = min(256, m)
    vec = pl.BlockSpec((1, d), lambda i: (0, 0))
    in_specs = [pl.BlockSpec((tm, d), lambda i: (i, 0)), vec, vec, vec]
    args = [x, gain, shift, scale]
    out_shape = [jax.ShapeDtypeStruct((m, d), BF16)]
    out_specs = [pl.BlockSpec((tm, d), lambda i: (i, 0))]
    scratch = []
    if route is not None:
        wr, br = router
        in_specs += [pl.BlockSpec((2, d, LANE), lambda i: (0, 0, 0)), pl.BlockSpec((N_EXPERTS, 1), lambda i: (0, 0))]
        args += [wr, br]
        out_shape.append(jax.ShapeDtypeStruct((m, LANE), F32))
        out_specs.append(pl.BlockSpec((tm, LANE), lambda i: (i, 0)))
    if route == "routed":
        out_shape[0] = jax.ShapeDtypeStruct((m, d // 2), jnp.uint32)
        out_specs[0] = pl.BlockSpec((tm, d // 2), lambda i: (i, 0))
        out_shape.append(jax.ShapeDtypeStruct((1, LANE), F32))
        out_specs.append(pl.BlockSpec((1, LANE), lambda i: (0, 0)))
        scratch = [pltpu.VMEM((1, LANE), F32)]
    out = pl.pallas_call(
        functools.partial(_modulate_kernel, route=route),
        out_shape=out_shape, grid=(m // tm,), in_specs=in_specs, out_specs=out_specs, scratch_shapes=scratch,
        compiler_params=_cp("arbitrary" if route == "routed" else "parallel"),
        name="modulate" if route is None else "modulate_route_" + route,
    )(*args)
    return out[0] if route is None else out


def _mm_kernel(*refs, n_pairs, residual):
    a_refs, w_refs = refs[:n_pairs], refs[n_pairs:2 * n_pairs]
    o_ref = refs[-1]
    acc = _dot(a_refs[0][...], w_refs[0][...])
    for a_ref, w_ref in zip(a_refs[1:], w_refs[1:]):
        acc += _dot(a_ref[...], w_ref[...])
    if residual:
        x_ref, g_ref = refs[2 * n_pairs:2 * n_pairs + 2]
        acc = x_ref[...] + g_ref[...] * acc
    o_ref[...] = acc.astype(o_ref.dtype)


def _mm(pairs, *, out_dtype, residual=None, tm=1024, tn=512, name="mm"):
    m = pairs[0][0].shape[0]
    n = pairs[0][1].shape[1]
    tm, tn = min(tm, m), min(tn, n)
    assert m % tm == 0 and n % tn == 0, (m, n, tm, tn)
    in_specs = [pl.BlockSpec((tm, a.shape[1]), lambda i, j: (i, 0)) for a, _ in pairs]
    in_specs += [pl.BlockSpec((w.shape[0], tn), lambda i, j: (0, j)) for _, w in pairs]
    args = [a for a, _ in pairs] + [w for _, w in pairs]
    if residual is not None:
        in_specs += [pl.BlockSpec((tm, tn), lambda i, j: (i, j)), pl.BlockSpec((1, tn), lambda i, j: (0, j))]
        args += list(residual)
    return pl.pallas_call(
        functools.partial(_mm_kernel, n_pairs=len(pairs), residual=residual is not None),
        out_shape=jax.ShapeDtypeStruct((m, n), out_dtype),
        grid=(m // tm, n // tn), in_specs=in_specs,
        out_specs=pl.BlockSpec((tm, tn), lambda i, j: (i, j)),
        compiler_params=_cp("parallel", "arbitrary"),
        name=name,
    )(*args)


def _rope_rotate(v, cos, sin_signed):
    lane = lax.broadcasted_iota(jnp.int32, (1, LANE), 1)
    first_half = (lane % 32) < 16
    partner = jnp.where(first_half, pltpu.roll(v, LANE - 16, axis=1), pltpu.roll(v, 16, axis=1))
    return v * cos + partner * sin_signed


def _q_up_kernel(cq_ref, ga_ref, w_ref, gn_ref, gr_ref, cos_ref, sin_ref, kmax_ref, o_ref, *, scale):
    cqn = (_rms(cq_ref[...].astype(F32)) * ga_ref[...]).astype(BF16)
    q = _dot(cqn, w_ref[...])
    cos, sin = cos_ref[...], sin_ref[...]
    gn, gr = gn_ref[...] * scale, gr_ref[...] * scale
    lane = lax.broadcasted_iota(jnp.int32, (1, LANE), 1)
    kmax = kmax_ref[...]
    for h in range(MLA_HEADS):
        o = h * MLA_HEAD_PAD
        qn = _rms(q[:, o:o + LANE]) * gn
        qr = _rope_rotate(_rms(q[:, o + LANE:o + 2 * LANE], MLA_ROPE) * gr, cos, sin)
        qq = jnp.sum(qn * qn, axis=-1, keepdims=True) + jnp.sum(qr * qr, axis=-1, keepdims=True)
        offset = SCORE_CENTER - jnp.sqrt(qq * kmax[:, h:h + 1])
        o_ref[:, o:o + LANE] = qn.astype(BF16)
        o_ref[:, o + LANE:o + 2 * LANE] = jnp.where(lane == OFFSET_LANE, offset, qr).astype(BF16)


def _q_up(px, ga, w, gn, gr, cos, sin, kmax):
    m = px.shape[0]
    tm = min(512, m)
    n = MLA_HEADS * MLA_HEAD_PAD
    vec = lambda width: pl.BlockSpec((1, width), lambda i: (0, 0))
    return pl.pallas_call(
        functools.partial(_q_up_kernel, scale=MLA_QK ** -0.5 * LOG2E),
        out_shape=jax.ShapeDtypeStruct((m, n), BF16),
        grid=(m // tm,),
        in_specs=[pl.BlockSpec((tm, MLA_Q_RANK), lambda i: (i, 0)), vec(MLA_Q_RANK),
                  pl.BlockSpec((MLA_Q_RANK, n), lambda i: (0, 0)), vec(LANE), vec(LANE),
                  pl.BlockSpec((tm, LANE), lambda i: (i, 0)), pl.BlockSpec((tm, LANE), lambda i: (i, 0)),
                  vec(LANE)],
        out_specs=pl.BlockSpec((tm, n), lambda i: (i, 0)),
        compiler_params=_cp("parallel"),
        name="mla_q_up",
    )(px, ga, w, gn, gr, cos, sin, kmax)


def _kv_up_kernel(ckv_ref, kr_ref, ga_ref, w_ref, gn_ref, gr_ref, cos_ref, sin_ref, k_ref, v_ref, kk_ref):
    ckvn = (_rms(ckv_ref[...].astype(F32)) * ga_ref[...]).astype(BF16)
    kv = _dot(ckvn, w_ref[...])
    kr = _rms(kr_ref[...].astype(F32), MLA_ROPE) * gr_ref[...]
    kr = _rope_rotate(kr, cos_ref[...], sin_ref[...])
    kr2 = jnp.sum(kr * kr, axis=-1, keepdims=True)
    lane = lax.broadcasted_iota(jnp.int32, (1, LANE), 1)
    krb = jnp.where(lane == OFFSET_LANE, 1.0, kr).astype(BF16)
    gn = gn_ref[...]
    kk = jnp.zeros(kk_ref.shape, F32)
    for h in range(MLA_HEADS):
        o = h * MLA_HEAD_PAD
        kn = _rms(kv[:, o:o + LANE]) * gn
        kk = jnp.where(lane == h, jnp.sum(kn * kn, axis=-1, keepdims=True) + kr2, kk)
        k_ref[:, o:o + LANE] = kn.astype(BF16)
        k_ref[:, o + LANE:o + 2 * LANE] = krb
        v_ref[:, h * MLA_V:(h + 1) * MLA_V] = kv[:, o + LANE:o + 2 * LANE].astype(BF16)
    kk_ref[...] = kk


def _colmax_kernel(x_ref, o_ref):
    @pl.when(pl.program_id(0) == 0)
    def _():
        o_ref[...] = jnp.full_like(o_ref, -jnp.inf)

    o_ref[...] = jnp.maximum(o_ref[...], jnp.max(x_ref[...], axis=0, keepdims=True))


def _colmax(x):
    m = x.shape[0]
    tm = min(2048, m)
    assert m % tm == 0, (m, tm)
    return pl.pallas_call(
        _colmax_kernel,
        out_shape=jax.ShapeDtypeStruct((1, LANE), F32),
        grid=(m // tm,),
        in_specs=[pl.BlockSpec((tm, LANE), lambda i: (i, 0))],
        out_specs=pl.BlockSpec((1, LANE), lambda i: (0, 0)),
        compiler_params=_cp("arbitrary"),
        name="colmax",
    )(x)


def _kv_up(px, ckv_blk, kr_blk, ga, w, gn, gr, cos, sin):
    m = px.shape[0]
    tm = min(512, m)
    vec = lambda width: pl.BlockSpec((1, width), lambda i: (0, 0))
    return pl.pallas_call(
        _kv_up_kernel,
        out_shape=[jax.ShapeDtypeStruct((m, MLA_HEADS * MLA_HEAD_PAD), BF16),
                   jax.ShapeDtypeStruct((m, MLA_HEADS * MLA_V), BF16),
                   jax.ShapeDtypeStruct((m, LANE), F32)],
        grid=(m // tm,),
        in_specs=[pl.BlockSpec((tm, MLA_KV_RANK), lambda i: (i, ckv_blk)),
                  pl.BlockSpec((tm, LANE), lambda i: (i, kr_blk)), vec(MLA_KV_RANK),
                  pl.BlockSpec((MLA_KV_RANK, MLA_HEADS * MLA_HEAD_PAD), lambda i: (0, 0)), vec(LANE), vec(LANE),
                  pl.BlockSpec((tm, LANE), lambda i: (i, 0)), pl.BlockSpec((tm, LANE), lambda i: (i, 0))],
        out_specs=[pl.BlockSpec((tm, MLA_HEADS * MLA_HEAD_PAD), lambda i: (i, 0)),
                   pl.BlockSpec((tm, MLA_HEADS * MLA_V), lambda i: (i, 0)),
                   pl.BlockSpec((tm, LANE), lambda i: (i, 0))],
        compiler_params=_cp("parallel"),
        name="mla_kv_up",
    )(px, px, ga, w, gn, gr, cos, sin)


def _attn_kernel(q_ref, k_ref, v_ref, o_ref, sa_ref, sb_ref, m_ref, l_ref, acc_ref, *, ck, n_chunks):
    def sub(i):
        return pl.ds(pl.multiple_of(i * ATTN_SUB, ATTN_SUB), ATTN_SUB)

    l_ref[...] = jnp.zeros_like(l_ref)
    acc_ref[...] = jnp.zeros_like(acc_ref)

    n_sub = n_chunks * ck // ATTN_SUB
    unroll = next(u for u in (13, 8, 5, 4, 3, 2, 1) if n_sub % u == 0)

    def stream(t, carry):
        lsum = l_ref[...]
        acc = acc_ref[...]
        for u in range(unroll):
            i = t * unroll + u
            p = jnp.exp2(_dot_nt(q_ref[...], k_ref[sub(i), :]))
            for part in range(ATTN_SUB // LANE):
                lsum += p[:, part * LANE:(part + 1) * LANE]
            acc += _dot(p.astype(BF16), v_ref[sub(i), :])
        l_ref[...] = lsum
        acc_ref[...] = acc
        return carry

    lax.fori_loop(0, n_sub // unroll, stream, 0)
    row_sum = jnp.sum(l_ref[...], axis=-1, keepdims=True)
    in_range = jnp.min(row_sum) >= SCORE_SUM_FLOOR

    @pl.when(in_range)
    def _():
        o_ref[...] = (acc_ref[...] / row_sum).astype(o_ref.dtype)

    @pl.when(jnp.logical_not(in_range))
    def _():
        _attn_online(q_ref, k_ref, v_ref, o_ref, sa_ref, sb_ref, m_ref, l_ref, acc_ref, ck=ck, n_chunks=n_chunks)


def _attn_online(q_ref, k_ref, v_ref, o_ref, sa_ref, sb_ref, m_ref, l_ref, acc_ref, *, ck, n_chunks):
    m_ref[...] = jnp.full_like(m_ref, -jnp.inf)
    l_ref[...] = jnp.zeros_like(l_ref)
    acc_ref[...] = jnp.zeros_like(acc_ref)

    def rows(c):
        return pl.ds(pl.multiple_of(c * ck, ck), ck)

    def scores(c, s_ref):
        s_ref[...] = _dot_nt(q_ref[...], k_ref[rows(c), :])

    def update(c, s_ref):
        m_prev = m_ref[...]
        m_new = jnp.maximum(m_prev, jnp.max(s_ref[...], axis=-1, keepdims=True))
        alpha = jnp.exp2(m_prev - m_new)
        m_ref[...] = m_new
        lsum = alpha * l_ref[...]
        acc = alpha * acc_ref[...]
        for j in range(ck // ATTN_SUB):
            p = jnp.exp2(s_ref[:, j * ATTN_SUB:(j + 1) * ATTN_SUB] - m_new)
            for t in range(ATTN_SUB // LANE):
                lsum += p[:, t * LANE:(t + 1) * LANE]
            acc += _dot(p.astype(BF16), v_ref[pl.ds(pl.multiple_of(c * ck + j * ATTN_SUB, ATTN_SUB), ATTN_SUB), :])
        l_ref[...] = lsum
        acc_ref[...] = acc

    bufs = (sa_ref, sb_ref)
    scores(0, sa_ref)
    group = 4 if (n_chunks - 1) % 4 == 0 else 2
    n_groups = (n_chunks - 1) // group

    def body(t, carry):
        c = group * t
        for u in range(group):
            scores(c + u + 1, bufs[(u + 1) % 2])
            update(c + u, bufs[u % 2])
        return carry

    if n_groups:
        lax.fori_loop(0, n_groups, body, 0)
    done = group * n_groups
    for c in range(done, n_chunks):
        if c + 1 < n_chunks:
            scores(c + 1, bufs[(c + 1) % 2])
        update(c, bufs[c % 2])
    o_ref[...] = (acc_ref[...] / jnp.sum(l_ref[...], axis=-1, keepdims=True)).astype(o_ref.dtype)


ATTN_SUB = 2 * LANE
OFFSET_LANE = MLA_ROPE
SCORE_CENTER = 64.0
SCORE_SUM_FLOOR = 2.0 ** -60


def _attn_tiles(tq_total, tk_total):
    tq = min(1024, tq_total)
    ck = tk_total
    for cand in (1280, 1024, 512, 256):
        if tk_total % cand == 0:
            ck = cand
            break
    return tq, ck


def _attention(q, k, v):
    tq_total, tk_total = q.shape[0], k.shape[0]
    tq, ck = _attn_tiles(tq_total, tk_total)
    return pl.pallas_call(
        functools.partial(_attn_kernel, ck=ck, n_chunks=tk_total // ck),
        out_shape=jax.ShapeDtypeStruct((tq_total, MLA_HEADS * MLA_V), BF16),
        grid=(MLA_HEADS, tq_total // tq),
        in_specs=[pl.BlockSpec((tq, MLA_HEAD_PAD), lambda h, i: (i, h)),
                  pl.BlockSpec((tk_total, MLA_HEAD_PAD), lambda h, i: (0, h), pipeline_mode=pl.Buffered(1)),
                  pl.BlockSpec((tk_total, MLA_V), lambda h, i: (0, h), pipeline_mode=pl.Buffered(1))],
        out_specs=pl.BlockSpec((tq, MLA_V), lambda h, i: (i, h)),
        scratch_shapes=[pltpu.VMEM((tq, ck), F32), pltpu.VMEM((tq, ck), F32),
                        pltpu.VMEM((tq, 1), F32), pltpu.VMEM((tq, LANE), F32), pltpu.VMEM((tq, MLA_V), F32)],
        compiler_params=_cp("parallel", "arbitrary"),
        name="mla_attention",
    )(q, k, v)


CONV_TILE = 512
HALO = 16


def _conv_kernel(gb_ref, gc_ref, u_ref, gcp_ref, up_ref, gcn_ref, un_ref, w_ref, b_ref, o_ref):
    i = pl.program_id(0)
    z = gc_ref[...].astype(F32) * u_ref[...].astype(F32)
    tm = z.shape[0]
    z_before = gcp_ref[HALO - 1:HALO, :].astype(F32) * up_ref[HALO - 1:HALO, :].astype(F32)
    z_after = gcn_ref[0:1, :].astype(F32) * un_ref[0:1, :].astype(F32)
    z_before = jnp.where(i > 0, z_before, 0.0)
    z_after = jnp.where(i < pl.num_programs(0) - 1, z_after, 0.0)
    rowid = lax.broadcasted_iota(jnp.int32, z.shape, 0)
    z_prev = jnp.where(rowid == 0, z_before, pltpu.roll(z, 1, axis=0))
    z_next = jnp.where(rowid == tm - 1, z_after, pltpu.roll(z, tm - 1, axis=0))
    w = w_ref[...]
    y = z_prev * w[0:1] + z * w[1:2] + z_next * w[2:3] + b_ref[...]
    o_ref[...] = (gb_ref[...].astype(F32) * y).astype(o_ref.dtype)


def _conv(px, col0, conv_w, conv_b):
    m = px.shape[0]
    tm = min(512, m)
    cb = CONV_WIDTH // CONV_TILE
    b0 = col0 // CONV_TILE
    nhalo = m // HALO
    main = lambda off: pl.BlockSpec((tm, CONV_TILE), lambda i, j: (i, b0 + off * cb + j))
    prev = lambda off: pl.BlockSpec(
        (HALO, CONV_TILE), lambda i, j: (jnp.maximum(i * (tm // HALO) - 1, 0), b0 + off * cb + j))
    nxt = lambda off: pl.BlockSpec(
        (HALO, CONV_TILE), lambda i, j: (jnp.minimum((i + 1) * (tm // HALO), nhalo - 1), b0 + off * cb + j))
    return pl.pallas_call(
        _conv_kernel,
        out_shape=jax.ShapeDtypeStruct((m, CONV_WIDTH), BF16),
        grid=(m // tm, cb),
        in_specs=[main(0), main(1), main(2), prev(1), prev(2), nxt(1), nxt(2),
                  pl.BlockSpec((3, CONV_TILE), lambda i, j: (0, j)),
                  pl.BlockSpec((1, CONV_TILE), lambda i, j: (0, j))],
        out_specs=pl.BlockSpec((tm, CONV_TILE), lambda i, j: (i, j)),
        compiler_params=_cp("parallel", "parallel"),
        name="gated_conv3",
    )(px, px, px, px, px, px, px, conv_w, conv_b)


def _moe_up_kernel(h_ref, wg_ref, wu_ref, gates_ref, o_ref):
    e = pl.program_id(1)
    h = h_ref[...]
    a = _dot(h, wg_ref[...])
    u = _dot(h, wu_ref[...])
    gates = gates_ref[...]
    lane = lax.broadcasted_iota(jnp.int32, gates.shape, 1)
    g = jnp.sum(jnp.where(lane == e, gates, 0.0), axis=-1, keepdims=True)
    o_ref[...] = (a * jax.nn.sigmoid(a) * u * g).astype(o_ref.dtype)


def _moe_up(h, wg, wu, gates):
    m, d = h.shape
    tm = min(1024, m)
    return pl.pallas_call(
        _moe_up_kernel,
        out_shape=jax.ShapeDtypeStruct((m, N_EXPERTS * D_FF), BF16),
        grid=(m // tm, N_EXPERTS),
        in_specs=[pl.BlockSpec((tm, d), lambda i, e: (i, 0)),
                  pl.BlockSpec((None, d, D_FF), lambda i, e: (e, 0, 0)),
                  pl.BlockSpec((None, d, D_FF), lambda i, e: (e, 0, 0)),
                  pl.BlockSpec((tm, LANE), lambda i, e: (i, 0))],
        out_specs=pl.BlockSpec((tm, D_FF), lambda i, e: (i, e)),
        compiler_params=_cp("parallel", "arbitrary"),
        name="moe_up",
    )(h, wg, wu, gates)


MOE_TOK_TILE = 256
MOE_ROW_TILE = 256


def _row_copy(src, src_row, dst, dst_row, sem):
    return pltpu.make_async_copy(src.at[pl.ds(src_row, 1)], dst.at[pl.ds(dst_row, 1)], sem)


def _moe_dispatch_kernel(pos_ref, pad_ref, h_ref, xs_ref, sem):
    tm = h_ref.shape[0]
    n_pad = pad_ref.shape[1]

    def token_rows(start):
        def one(r, carry):
            for k in range(2):
                cp = _row_copy(h_ref, r, xs_ref, pos_ref[0, 2 * r + k], sem)
                cp.start() if start else cp.wait()
            return carry

        lax.fori_loop(0, tm, one, 0, unroll=8)

    def pad_rows(start):
        def one(j, carry):
            cp = _row_copy(h_ref, 0, xs_ref, pad_ref[0, j], sem)
            cp.start() if start else cp.wait()
            return carry

        lax.fori_loop(0, n_pad, one, 0, unroll=8)

    token_rows(True)
    pad_rows(True)
    token_rows(False)
    pad_rows(False)


def _moe_dispatch(h_packed, pos, pad_pos, n_rows):
    m, w = h_packed.shape
    tm = MOE_TOK_TILE
    n_pad = pad_pos.shape[-1]
    return pl.pallas_call(
        _moe_dispatch_kernel,
        out_shape=jax.ShapeDtypeStruct((n_rows, w), jnp.uint32),
        grid=(m // tm,),
        in_specs=[pl.BlockSpec((None, 1, 2 * tm), lambda i: (i, 0, 0), memory_space=pltpu.SMEM),
                  pl.BlockSpec((None, 1, n_pad), lambda i: (i, 0, 0), memory_space=pltpu.SMEM),
                  pl.BlockSpec((tm, w), lambda i: (i, 0))],
        out_specs=pl.BlockSpec(memory_space=pl.ANY),
        scratch_shapes=[pltpu.SemaphoreType.DMA],
        compiler_params=_cp("arbitrary"),
        name="moe_dispatch",
    )(pos, pad_pos, h_packed)


def _moe_experts_kernel(te_ref, xs_ref, wg_ref, wu_ref, wd_ref, ys_ref):
    del te_ref
    half = xs_ref.shape[1]
    x1, x2 = _unpack_bf16_pairs(xs_ref[...])
    x1, x2 = x1.astype(BF16), x2.astype(BF16)
    a = _dot(x1, wg_ref[:half, :]) + _dot(x2, wg_ref[half:, :])
    u = _dot(x1, wu_ref[:half, :]) + _dot(x2, wu_ref[half:, :])
    hid = (a * jax.nn.sigmoid(a) * u).astype(BF16)
    ys_ref[...] = _pack_bf16_pairs(_dot(hid, wd_ref[...]))


def _moe_experts(tile_expert, xs, wg, wu, wd):
    n_rows, w = xs.shape
    d = 2 * w
    t = MOE_ROW_TILE
    return pl.pallas_call(
        _moe_experts_kernel,
        out_shape=jax.ShapeDtypeStruct((n_rows, w), jnp.uint32),
        grid_spec=pltpu.PrefetchScalarGridSpec(
            num_scalar_prefetch=1, grid=(n_rows // t,),
            in_specs=[pl.BlockSpec((t, w), lambda i, te: (i, 0)),
                      pl.BlockSpec((None, d, D_FF), lambda i, te: (te[i], 0, 0)),
                      pl.BlockSpec((None, d, D_FF), lambda i, te: (te[i], 0, 0)),
                      pl.BlockSpec((None, D_FF, d), lambda i, te: (te[i], 0, 0))],
            out_specs=pl.BlockSpec((t, w), lambda i, te: (i, 0))),
        compiler_params=_cp("arbitrary"),
        name="moe_experts",
    )(tile_expert, xs, wg, wu, wd)


def _moe_combine_kernel(pos_ref, posn_ref, x_ref, gate_ref, rec_ref, ys_ref, o_ref, g1_ref, g2_ref, sem):
    _, tm, half = g1_ref.shape
    bufs = (g1_ref, g2_ref)
    i = pl.program_id(0)
    slot = i % 2

    def gather(p_ref, s, start):
        def one(r, carry):
            for k in range(2):
                cp = _row_copy(ys_ref, p_ref[0, 2 * r + k], bufs[k].at[s], r, sem.at[s])
                cp.start() if start else cp.wait()
            return carry

        lax.fori_loop(0, tm, one, 0, unroll=8)

    @pl.when(i == 0)
    def _():
        gather(pos_ref, slot, True)

    @pl.when(i + 1 < pl.num_programs(0))
    def _():
        gather(posn_ref, 1 - slot, True)

    gather(pos_ref, slot, False)
    rb = 2 * SUBLANE

    def combine(b, carry):
        rs = pl.ds(pl.multiple_of(b * rb, rb), rb)
        rec = rec_ref[rs, :]
        w1, w2 = rec[:, INFO_W1:INFO_W1 + 1], rec[:, INFO_W2:INFO_W2 + 1]
        y1 = _unpack_bf16_pairs(g1_ref[slot, rs, :])
        y2 = _unpack_bf16_pairs(g2_ref[slot, rs, :])
        for part in range(2):
            cols = slice(part * half, (part + 1) * half)
            o_ref[rs, cols] = x_ref[rs, cols] + gate_ref[:, cols] * (w1 * y1[part] + w2 * y2[part])
        return carry

    lax.fori_loop(0, tm // rb, combine, 0)


def _moe_combine(pos, xres, gate, rec, ys):
    m, d = xres.shape
    tm = MOE_TOK_TILE
    last = m // tm - 1
    return pl.pallas_call(
        _moe_combine_kernel,
        out_shape=jax.ShapeDtypeStruct((m, d), F32),
        grid=(m // tm,),
        in_specs=[pl.BlockSpec((None, 1, 2 * tm), lambda i: (i, 0, 0), memory_space=pltpu.SMEM),
                  pl.BlockSpec((None, 1, 2 * tm), lambda i: (jnp.minimum(i + 1, last), 0, 0),
                               memory_space=pltpu.SMEM),
                  pl.BlockSpec((tm, d), lambda i: (i, 0)),
                  pl.BlockSpec((1, d), lambda i: (0, 0)),
                  pl.BlockSpec((tm, LANE), lambda i: (i, 0)),
                  pl.BlockSpec(memory_space=pl.ANY)],
        out_specs=pl.BlockSpec((tm, d), lambda i: (i, 0)),
        scratch_shapes=[pltpu.VMEM((2, tm, d // 2), jnp.uint32), pltpu.VMEM((2, tm, d // 2), jnp.uint32),
                        pltpu.SemaphoreType.DMA((2,))],
        compiler_params=_cp("arbitrary"),
        name="moe_combine",
    )(pos, pos, xres, gate, rec, ys)


def _moe_routed(xres, gain, shift, scale, gate, router, wg, wu, wd3):
    m = xres.shape[0]
    tm, t = MOE_TOK_TILE, MOE_ROW_TILE
    h_packed, rec, counts = _modulate(xres, gain, shift, scale, router=router, route="routed")
    n_rows = 2 * m + N_EXPERTS * t
    cnt = counts[0, :N_EXPERTS].astype(jnp.int32)
    seg = ((cnt + t - 1) // t) * t
    seg_end = jnp.cumsum(seg)
    seg_start = seg_end - seg
    e12 = rec[:, INFO_E1:INFO_E2 + 1].astype(jnp.int32)
    r12 = rec[:, INFO_R1:INFO_R2 + 1].astype(jnp.int32)
    pos = (seg_start[e12] + r12).reshape(m // tm, 1, 2 * tm)
    tile_start = jnp.arange(n_rows // t, dtype=jnp.int32) * t
    tile_expert = jnp.minimum(jnp.sum(tile_start[:, None] >= seg_end[None, :], axis=1), N_EXPERTS - 1).astype(jnp.int32)
    n_steps = m // tm
    assert (n_rows - 2 * m) % n_steps == 0
    pad_cnt = jnp.concatenate([seg - cnt, n_rows - seg_end[-1:]])
    pad_end = jnp.cumsum(pad_cnt)
    pad_first = jnp.concatenate([seg_start + cnt, seg_end[-1:]])
    j = jnp.arange(n_rows - 2 * m, dtype=jnp.int32)
    rng = jnp.sum(j[:, None] >= pad_end[None, :], axis=1)
    pad_pos = (pad_first[rng] + (j - (pad_end - pad_cnt)[rng])).astype(jnp.int32).reshape(n_steps, 1, -1)
    xs = _moe_dispatch(h_packed, pos, pad_pos, n_rows)
    ys = _moe_experts(tile_expert, xs, wg, wu, wd3)
    return _moe_combine(pos, xres, gate, rec, ys)


GLA_LEVELS = 6
GLA_MILD_DECAY = 50.0


def _gla_tables(reverse):
    n = GLA_CHUNK
    t = np.arange(n)[:, None]
    r = np.arange(n)[None, :]
    blocks = [r <= t, r > t]
    masks = []
    for lvl in range(GLA_LEVELS):
        bs = 1 << lvl
        blocks.append((r >= (t & ~(bs - 1))) & (r <= t))
        blocks.append((r > t) & (r <= (t | (bs - 1))))
        masks.append(((t >> (lvl + 1)) == (r >> (lvl + 1))) & (((t >> lvl) & 1) == 1) & (((r >> lvl) & 1) == 0))
    masks.append(t == r)
    masks.append(r <= t)
    e = np.stack(blocks).astype(np.float32)
    msk = np.stack(masks).astype(np.float32)
    if reverse:
        e, msk = e[:, ::-1, ::-1], msk[:, ::-1, ::-1]
    return jnp.asarray(e.reshape(-1, n)), jnp.asarray(msk)


def _gla_intra_kernel(k_ref, v_ref, q_ref, a_ref, wa_ref, ba_ref, e_ref, m_ref,
                      oi_ref, qd_ref, kd_ref, dl_ref, la_ref, tot_ref, *, nchunk, reverse):
    n = GLA_CHUNK
    scale = GLA_DK ** -0.5

    a = a_ref[...]
    z = _dot(a, wa_ref[0]) + _dot(a, wa_ref[1]) + _dot(a, wa_ref[2]) + ba_ref[...]
    la = (jnp.minimum(z, 0.0) - jnp.log1p(jnp.exp(-jnp.abs(z)))) * (1.0 / GLA_TEMP)
    la_ref[...] = la
    totals = jnp.sum(la.reshape(nchunk, n, GLA_DK), axis=1)
    tot_ref[...] = totals
    dl_ref[...] = jnp.exp(totals)
    mild = jnp.min(totals) > -GLA_MILD_DECAY
    rowid = lax.broadcasted_iota(jnp.int32, (n, GLA_DK), 0)

    def load(c):
        rows = pl.ds(pl.multiple_of(c * n, n), n)
        return rows, q_ref[rows, :].astype(F32) * scale, k_ref[rows, :].astype(F32)

    def mild_chunk(c, carry):
        rows, q, k = load(c)
        la_c = la_ref[rows, :]
        total = tot_ref[pl.ds(c, 1), :]
        cum = la_c
        shift = 1
        while shift < n:
            cum = cum + jnp.where(rowid >= shift, pltpu.roll(cum, shift, axis=0), 0.0)
            shift *= 2
        if reverse:
            cum = total - cum + la_c
        qd = (q * jnp.exp(cum)).astype(BF16)
        kinv = k * jnp.exp(-cum)
        qd_ref[rows, :] = qd
        kd_ref[rows, :] = (kinv * jnp.exp(total)).astype(BF16)
        att = m_ref[GLA_LEVELS + 1] * _dot_nt(qd, kinv.astype(BF16))
        oi_ref[rows, :] = _dot(att.astype(BF16), v_ref[rows, :]).astype(BF16)
        return carry

    def harsh_chunk(c, carry):
        rows, q, k = load(c)
        ex = _dot_f32(e_ref[...], la_ref[rows, :])
        qd_ref[rows, :] = (q * jnp.exp(ex[0:n])).astype(BF16)
        kd_ref[rows, :] = (k * jnp.exp(ex[n:2 * n])).astype(BF16)
        att = m_ref[GLA_LEVELS] * _dot_nt(q.astype(BF16), k.astype(BF16))
        for lvl in range(GLA_LEVELS):
            qs = (q * jnp.exp(ex[(2 + 2 * lvl) * n:(3 + 2 * lvl) * n])).astype(BF16)
            ks = (k * jnp.exp(ex[(3 + 2 * lvl) * n:(4 + 2 * lvl) * n])).astype(BF16)
            att += m_ref[lvl] * _dot_nt(qs, ks)
        oi_ref[rows, :] = _dot(att.astype(BF16), v_ref[rows, :]).astype(BF16)
        return carry

    @pl.when(mild)
    def _():
        lax.fori_loop(0, nchunk, mild_chunk, 0, unroll=2)

    @pl.when(jnp.logical_not(mild))
    def _():
        lax.fori_loop(0, nchunk, harsh_chunk, 0)


def _gla_blocks(m):
    tb = min(512, m)
    return tb, tb // GLA_CHUNK


def _gla_intra(px, cols, wa_pad, ba, reverse):
    m = px.shape[0]
    tb, nchunk = _gla_blocks(m)
    kb, vb, qb, ab = cols
    e, msk = _gla_tables(reverse)
    return pl.pallas_call(
        functools.partial(_gla_intra_kernel, nchunk=nchunk, reverse=reverse),
        out_shape=[jax.ShapeDtypeStruct((m, GLA_VAL), BF16), jax.ShapeDtypeStruct((m, GLA_KEY), BF16),
                   jax.ShapeDtypeStruct((m, GLA_KEY), BF16), jax.ShapeDtypeStruct((m // GLA_CHUNK, GLA_KEY), F32)],
        grid=(m // tb, GLA_HEADS),
        in_specs=[pl.BlockSpec((tb, GLA_DK), lambda i, h: (i, kb + h)),
                  pl.BlockSpec((tb, GLA_DV), lambda i, h: (i, vb + h)),
                  pl.BlockSpec((tb, GLA_DK), lambda i, h: (i, qb + h)),
                  pl.BlockSpec((tb, LANE), lambda i, h: (i, ab)),
                  pl.BlockSpec((3, LANE, GLA_DK), lambda i, h: (0, 0, h)),
                  pl.BlockSpec((1, GLA_DK), lambda i, h: (0, h)),
                  pl.BlockSpec(e.shape, lambda i, h: (0, 0)),
                  pl.BlockSpec(msk.shape, lambda i, h: (0, 0, 0))],
        out_specs=[pl.BlockSpec((tb, GLA_DV), lambda i, h: (i, h)),
                   pl.BlockSpec((tb, GLA_DK), lambda i, h: (i, h)),
                   pl.BlockSpec((tb, GLA_DK), lambda i, h: (i, h)),
                   pl.BlockSpec((nchunk, GLA_DK), lambda i, h: (i, h))],
        scratch_shapes=[pltpu.VMEM((tb, GLA_DK), F32), pltpu.VMEM((nchunk, GLA_DK), F32)],
        compiler_params=_cp("parallel", "parallel"),
        name="gla_intra_bwd" if reverse else "gla_intra_fwd",
    )(px, px, px, px, wa_pad, ba, e, msk)


def _gla_scan_kernel(*refs, nchunk):
    i = pl.program_id(1)
    n = GLA_CHUNK
    ins, outs, states = refs[:12], refs[12:16], refs[16:]
    dirs = [(ins[6 * d:6 * d + 6], outs[2 * d:2 * d + 2], states[d], bool(d)) for d in range(2)]

    @pl.when(i == 0)
    def _():
        for (_, _, _, _, _, s0_ref), _, s_ref, _ in dirs:
            s_ref[...] = s0_ref[...]

    def chunk(cc, carry):
        for (qd_ref, kd_ref, v_ref, dl_ref, oi_ref, _), (o_ref, _), s_ref, reverse in dirs:
            c = (nchunk - 1 - cc) if reverse else cc
            rows = pl.ds(pl.multiple_of(c * n, n), n)
            st = s_ref[...]
            o_ref[rows, :] = (oi_ref[rows, :].astype(F32) + _dot_nt(qd_ref[rows, :], st.astype(BF16))).astype(BF16)
            s_ref[...] = st * dl_ref[pl.ds(c, 1), :] + _dot_tn(v_ref[rows, :], kd_ref[rows, :])
        return carry

    lax.fori_loop(0, nchunk, chunk, 0, unroll=True)

    @pl.when(i == pl.num_programs(1) - 1)
    def _():
        for _, (_, sf_ref), s_ref, _ in dirs:
            sf_ref[...] = s_ref[...]


def _gla_scan(px, vb, fwd, bwd):
    m = px.shape[0]
    tb, nchunk = _gla_blocks(m)
    nb = m // tb

    def specs(blk):
        return [pl.BlockSpec((tb, GLA_DK), lambda h, i: (blk(i), h)),
                pl.BlockSpec((tb, GLA_DK), lambda h, i: (blk(i), h)),
                pl.BlockSpec((tb, GLA_DV), lambda h, i: (blk(i), vb + h)),
                pl.BlockSpec((nchunk, GLA_DK), lambda h, i: (blk(i), h)),
                pl.BlockSpec((tb, GLA_DV), lambda h, i: (blk(i), h)),
                pl.BlockSpec((None, GLA_DV, GLA_DK), lambda h, i: (h, 0, 0))]

    def out_specs(blk):
        return [pl.BlockSpec((tb, GLA_DV), lambda h, i: (blk(i), h)),
                pl.BlockSpec((None, GLA_DV, GLA_DK), lambda h, i: (h, 0, 0))]

    ahead, back = (lambda i: i), (lambda i: nb - 1 - i)
    out_shape = [jax.ShapeDtypeStruct((m, GLA_VAL), BF16), jax.ShapeDtypeStruct((GLA_HEADS, GLA_DV, GLA_DK), F32)]
    args = []
    for qd, kd, dl, oi, s0 in (fwd, bwd):
        args += [qd, kd, px, dl, oi, s0]
    return pl.pallas_call(
        functools.partial(_gla_scan_kernel, nchunk=nchunk),
        out_shape=out_shape + out_shape,
        grid=(GLA_HEADS, nb),
        in_specs=specs(ahead) + specs(back),
        out_specs=out_specs(ahead) + out_specs(back),
        scratch_shapes=[pltpu.VMEM((GLA_DV, GLA_DK), F32), pltpu.VMEM((GLA_DV, GLA_DK), F32)],
        compiler_params=_cp("parallel", "arbitrary"),
        name="gla_scan",
    )(*args)


def _gla_gate_kernel(of_ref, ob_ref, g_ref, gain_ref, o_ref):
    gain = gain_ref[...]
    for h in range(GLA_HEADS):
        sl = slice(h * GLA_DV, (h + 1) * GLA_DV)
        o = _rms(of_ref[:, sl].astype(F32) + ob_ref[:, sl].astype(F32)) * gain
        g = g_ref[:, sl].astype(F32)
        o_ref[:, sl] = (o * (g * jax.nn.sigmoid(g))).astype(BF16)


def _gla_gate(o_f, o_b, px, g_blk, gain):
    m = o_f.shape[0]
    tm = min(256, m)
    return pl.pallas_call(
        _gla_gate_kernel,
        out_shape=jax.ShapeDtypeStruct((m, GLA_VAL), BF16),
        grid=(m // tm,),
        in_specs=[pl.BlockSpec((tm, GLA_VAL), lambda i: (i, 0)), pl.BlockSpec((tm, GLA_VAL), lambda i: (i, 0)),
                  pl.BlockSpec((tm, GLA_VAL), lambda i: (i, g_blk)), pl.BlockSpec((1, GLA_DV), lambda i: (0, 0))],
        out_specs=pl.BlockSpec((tm, GLA_VAL), lambda i: (i, 0)),
        compiler_params=_cp("parallel"),
        name="gla_gate",
    )(o_f, o_b, px, gain)


def _pad_cols(w, mult):
    pad = (-w.shape[1]) % mult
    return jnp.pad(w, ((0, 0), (0, pad))) if pad else w


def _row(v, width=None):
    v = v.reshape(1, -1).astype(F32)
    return _pad_cols(v, width) if width else v


MM_TN = 512

AB_CKV_BLK = MLA_Q_RANK // MLA_KV_RANK
AB_CONV_COL = MLA_Q_RANK + MLA_KV_RANK
AB_KR_BLK = (AB_CONV_COL + 3 * CONV_WIDTH) // LANE
GLA_K_BLK = 0
GLA_V_BLK = GLA_KEY // GLA_DV
GLA_Q_BLK = (GLA_KEY + GLA_VAL) // GLA_DK
GLA_G_BLK = (2 * GLA_KEY + GLA_VAL) // GLA_VAL
GLA_A_BLK = (2 * GLA_KEY + 2 * GLA_VAL) // LANE


def _ab_w_in(w):
    ckv = w[:, :MLA_KV_RANK]
    kr = w[:, MLA_KV_RANK:MLA_KV_RANK + MLA_ROPE]
    cq = w[:, MLA_KV_RANK + MLA_ROPE:MLA_KV_RANK + MLA_ROPE + MLA_Q_RANK]
    conv = w[:, MLA_KV_RANK + MLA_ROPE + MLA_Q_RANK:]
    return _pad_cols(jnp.concatenate([cq, ckv, conv, kr], axis=1), MM_TN).astype(BF16)


def _ab_w_uq(w):
    w = w.reshape(MLA_Q_RANK, MLA_HEADS, MLA_QK)
    w = jnp.pad(w, ((0, 0), (0, 0), (0, MLA_HEAD_PAD - MLA_QK)))
    return w.reshape(MLA_Q_RANK, MLA_HEADS * MLA_HEAD_PAD).astype(BF16)


def _gla_w_in(w):
    o1, o2 = GLA_KEY, GLA_KEY + GLA_VAL
    o3 = o2 + 2 * GLA_RANK
    k, v, a, q, g = w[:, :o1], w[:, o1:o2], w[:, o2:o3], w[:, o3:o3 + GLA_KEY], w[:, o3 + GLA_KEY:]
    return _pad_cols(jnp.concatenate([k, v, q, g, a], axis=1), MM_TN).astype(BF16)


def _gla_wa_pad(w_a, offset):
    w = jnp.zeros((LANE, GLA_KEY), F32).at[offset:offset + GLA_RANK].set(w_a.astype(F32))
    hi = w.astype(BF16)
    r1 = w - hi.astype(F32)
    mid = r1.astype(BF16)
    lo = (r1 - mid.astype(F32)).astype(BF16)
    return jnp.stack([hi, mid, lo])


def _rope_tables(pos_rows, pos_cols):
    half = MLA_ROPE // 4
    inv_freq = ROPE_BASE ** (-jnp.arange(half, dtype=F32) / half)
    ar = pos_rows.astype(F32)[:, None] * inv_freq[None, :]
    ac = pos_cols.astype(F32)[:, None] * inv_freq[None, :]
    zeros = jnp.zeros((ar.shape[0], LANE - MLA_ROPE), F32)
    cos = jnp.concatenate([jnp.cos(ar), jnp.cos(ar), jnp.cos(ac), jnp.cos(ac), zeros], axis=1)
    sin = jnp.concatenate([-jnp.sin(ar), jnp.sin(ar), -jnp.sin(ac), jnp.sin(ac), zeros], axis=1)
    return cos, sin


def _mixer_mla_conv(hx, hc, p, rope_x, rope_c, need_ctx):
    w_in, w_uq, w_ukv, w_out = p["w_in"], p["w_uq"], p["w_ukv"], p["w_out"]

    def project(h, rope):
        px = _mm([(h, w_in)], out_dtype=BF16, name="ab_in_proj")
        k, v, kk = _kv_up(px, AB_CKV_BLK, AB_KR_BLK, p["kv_a_gain"], w_ukv, p["k_gain_n"], p["k_gain_r"], *rope)
        return px, k, v, _colmax(kk)

    def branch_out(px, k, v, kmax, rope):
        q = _q_up(px, p["q_a_gain"], w_uq, p["q_gain_n"], p["q_gain_r"], *rope, kmax)
        o = _attention(q, k, v)
        y = _conv(px, AB_CONV_COL, p["conv_w"], p["conv_b"])
        return o, y

    px_c, k_c, v_c, kmax_c = project(hc, rope_c)
    px_x, k_x, v_x, kmax_x = project(hx, rope_x)
    k_all = jnp.concatenate([k_c, k_x], axis=0)
    v_all = jnp.concatenate([v_c, v_x], axis=0)
    out_x = branch_out(px_x, k_all, v_all, jnp.maximum(kmax_c, kmax_x), rope_x)
    out_c = branch_out(px_c, k_c, v_c, kmax_c, rope_c) if need_ctx else None
    half = MLA_HEADS * MLA_V
    return out_x, out_c, (w_out[:half], w_out[half:])


def _mixer_gla(hx, hc, p, need_ctx):
    cols = (GLA_K_BLK, GLA_V_BLK, GLA_Q_BLK, GLA_A_BLK)
    zero_state = jnp.zeros((GLA_HEADS, GLA_DV, GLA_DK), F32)

    def run(h, s_f, s_b):
        px = _mm([(h, p["w_in"])], out_dtype=BF16, name="gla_in_proj")
        sides = []
        for reverse, wa, ba, s0 in ((False, p["wa_f"], p["ba_f"], s_f), (True, p["wa_b"], p["ba_b"], s_b)):
            oi, qd, kd, dl = _gla_intra(px, cols, wa, ba, reverse)
            sides.append((qd, kd, dl, oi, s0))
        o_f, sf_fin, o_b, sb_fin = _gla_scan(px, GLA_V_BLK, *sides)
        return px, (o_f, o_b), (sf_fin, sb_fin)

    px_c, o_c, (s_f, s_b) = run(hc, zero_state, zero_state)
    px_x, o_x, _ = run(hx, s_f, s_b)
    out_x = _gla_gate(o_x[0], o_x[1], px_x, GLA_G_BLK, p["o_gain"])
    out_c = _gla_gate(o_c[0], o_c[1], px_c, GLA_G_BLK, p["o_gain"]) if need_ctx else None
    return out_x, out_c


def _moe_dense(xres, gain, shift, scale, gate, router, wg, wu, wd3):
    h, gates = _modulate(xres, gain, shift, scale, router=router, route="dense")
    hid = _moe_up(h, wg, wu, gates)
    wd = wd3.reshape(N_EXPERTS * D_FF, wd3.shape[-1])
    return _mm([(hid, wd)], out_dtype=F32, residual=(xres, gate), name="moe_down")


def _moe(xres, *args):
    routed = xres.shape[0] >= 4 * MOE_ROW_TILE and xres.shape[0] % MOE_TOK_TILE == 0
    return (_moe_routed if routed else _moe_dense)(xres, *args)


def kernel(x, c, ctx, c_ctx, ada_w, ada_b, norm_mix_gain, norm_ffn_gain, ab_w_in, ab_q_a_gain, ab_w_uq, ab_q_gain, ab_kv_a_gain, ab_w_ukv, ab_k_gain, ab_conv_w, ab_conv_b, ab_w_out, gla_w_in, gla_w_af, gla_b_af, gla_w_ab, gla_b_ab, gla_o_gain, gla_w_out, router_w, router_b, moe_w_gate, moe_w_up, moe_w_down):
    b, s, d = x.shape
    assert b == 1 and s % GRID_W == 0
    depth = ada_w.shape[0]
    xs, cx = x[0], ctx[0]
    n_ctx = cx.shape[0]

    mod = _adaln(c.reshape(d, 1), c_ctx.reshape(d, 1), ada_w, ada_b).reshape(depth, 2, 6, 1, d)
    tok = jnp.arange(s, dtype=jnp.int32)
    rope_x = _rope_tables(tok // GRID_W, tok % GRID_W)
    rope_c = (jnp.concatenate([jnp.ones((n_ctx, MLA_ROPE), F32), jnp.zeros((n_ctx, LANE - MLA_ROPE), F32)], axis=1),
              jnp.zeros((n_ctx, LANE), F32))
    wr = _pad_cols(router_w.astype(F32), LANE)
    wr_hi = wr.astype(BF16)
    wr_lo = (wr - wr_hi.astype(F32)).astype(BF16)
    router = (jnp.stack([wr_hi, wr_lo]), router_b.reshape(N_EXPERTS, 1).astype(F32))

    for i in range(depth):
        last = i == depth - 1
        j = i // 2
        mx, mc = mod[i, 0], mod[i, 1]
        gain_mix = _row(norm_mix_gain[i])
        hx = _modulate(xs, gain_mix, mx[0], mx[1])
        hc = _modulate(cx, gain_mix, mc[0], mc[1])
        if i % 2 == 0:
            p = dict(
                w_in=_ab_w_in(ab_w_in[j]), w_uq=_ab_w_uq(ab_w_uq[j]), w_ukv=ab_w_ukv[j].astype(BF16),
                w_out=ab_w_out[j].astype(BF16),
                q_a_gain=_row(ab_q_a_gain[j]), kv_a_gain=_row(ab_kv_a_gain[j]),
                q_gain_n=_row(ab_q_gain[j][:MLA_NOPE]), q_gain_r=_row(ab_q_gain[j][MLA_NOPE:], LANE),
                k_gain_n=_row(ab_k_gain[j][:MLA_NOPE]), k_gain_r=_row(ab_k_gain[j][MLA_NOPE:], LANE),
                conv_w=ab_conv_w[j].astype(F32), conv_b=_row(ab_conv_b[j]))
            (o_x, y_x), out_c, (w_o, w_y) = _mixer_mla_conv(hx, hc, p, rope_x, rope_c, not last)
            xs = _mm([(o_x, w_o), (y_x, w_y)], out_dtype=F32, residual=(xs, mx[2]), name="ab_out_proj")
            if not last:
                cx = _mm([(out_c[0], w_o), (out_c[1], w_y)], out_dtype=F32, residual=(cx, mc[2]), name="ab_out_proj")
        else:
            p = dict(
                w_in=_gla_w_in(gla_w_in[j]),
                wa_f=_gla_wa_pad(gla_w_af[j], 0), wa_b=_gla_wa_pad(gla_w_ab[j], GLA_RANK),
                ba_f=_row(gla_b_af[j]), ba_b=_row(gla_b_ab[j]), o_gain=_row(gla_o_gain[j]))
            w_out = gla_w_out[j].astype(BF16)
            y_x, y_c = _mixer_gla(hx, hc, p, not last)
            xs = _mm([(y_x, w_out)], out_dtype=F32, residual=(xs, mx[2]), name="gla_out_proj")
            if not last:
                cx = _mm([(y_c, w_out)], out_dtype=F32, residual=(cx, mc[2]), name="gla_out_proj")
        gain_ffn = _row(norm_ffn_gain[i])
        wg, wu = moe_w_gate[i].astype(BF16), moe_w_up[i].astype(BF16)
        wd = moe_w_down[i].astype(BF16)
        xs = _moe(xs, gain_ffn, mx[3], mx[4], mx[5], router, wg, wu, wd)
        if not last:
            cx = _moe(cx, gain_ffn, mc[3], mc[4], mc[5], router, wg, wu, wd)
    return xs[None]
```

```python
import functools

import numpy as np
import jax
import jax.numpy as jnp
from jax import lax
from jax.experimental import pallas as pl
from jax.experimental.pallas import tpu as pltpu

F32 = jnp.float32
BF16 = jnp.bfloat16
HIGHEST = lax.Precision.HIGHEST

LANE = 128
SUBLANE = 8
VMEM_LIMIT_BYTES = 56 * 1024 * 1024

GRID_W = 64
EPS = 1e-6
MLA_HEADS = 16
MLA_Q_RANK = 1024
MLA_KV_RANK = 512
MLA_NOPE = 128
MLA_ROPE = 64
MLA_V = 128
MLA_QK = MLA_NOPE + MLA_ROPE
MLA_HEAD_PAD = 2 * LANE
ROPE_BASE = 10000.0
LOG2E = 1.4426950408889634
CONV_WIDTH = 2048
GLA_HEADS = 8
GLA_DK = 256
GLA_DV = 512
GLA_KEY = GLA_HEADS * GLA_DK
GLA_VAL = GLA_HEADS * GLA_DV
GLA_RANK = 16
GLA_TEMP = 16.0
GLA_CHUNK = 64
N_EXPERTS = 16
N_GROUPS = 4
GROUP_SIZE = N_EXPERTS // N_GROUPS
D_FF = 256


def _cp(*sem):
    return pltpu.CompilerParams(dimension_semantics=sem, vmem_limit_bytes=VMEM_LIMIT_BYTES)


def _dot(a, b):
    return jnp.dot(a, b, preferred_element_type=F32)


def _dot_nt(a, b):
    return lax.dot_general(a, b, (((1,), (1,)), ((), ())), preferred_element_type=F32)


def _dot_tn(a, b):
    return lax.dot_general(a, b, (((0,), (0,)), ((), ())), preferred_element_type=F32)


def _dot_f32(a, b):
    return jnp.dot(a, b, precision=HIGHEST, preferred_element_type=F32)


def _rms(v, n=None):
    ss = jnp.sum(v * v, axis=-1, keepdims=True)
    return v * lax.rsqrt(ss * (1.0 / (n if n is not None else v.shape[-1])) + EPS)


def _adaln_kernel(c_ref, cc_ref, w_ref, b_ref, o_ref, acc_ref):
    k = pl.program_id(2)

    @pl.when(k == 0)
    def _():
        acc_ref[...] = jnp.zeros_like(acc_ref)

    w = w_ref[...]
    tk, tn = w.shape
    for r, ref in enumerate((c_ref, cc_ref)):
        cv = ref[...]
        cs = cv * jax.nn.sigmoid(cv)
        acc_ref[r] += jnp.sum((w * cs).reshape(tk // SUBLANE, SUBLANE, tn), axis=0)

    @pl.when(k == pl.num_programs(2) - 1)
    def _():
        o_ref[...] = jnp.sum(acc_ref[...], axis=1) + b_ref[...]


def _adaln(c_col, cc_col, ada_w, ada_b):
    depth, d, n = ada_w.shape
    tk, tn = 1024, 2048
    return pl.pallas_call(
        _adaln_kernel,
        out_shape=jax.ShapeDtypeStruct((depth, 2, n), F32),
        grid=(depth, n // tn, d // tk),
        in_specs=[
            pl.BlockSpec((tk, 1), lambda l, j, k: (k, 0)),
            pl.BlockSpec((tk, 1), lambda l, j, k: (k, 0)),
            pl.BlockSpec((None, tk, tn), lambda l, j, k: (l, k, j)),
            pl.BlockSpec((None, 1, tn), lambda l, j, k: (l, 0, j)),
        ],
        out_specs=pl.BlockSpec((None, 2, tn), lambda l, j, k: (l, 0, j)),
        scratch_shapes=[pltpu.VMEM((2, SUBLANE, tn), F32)],
        compiler_params=_cp("parallel", "parallel", "arbitrary"),
        name="adaln_mod",
    )(c_col, cc_col, ada_w, ada_b.reshape(depth, 1, n))


def _route_topk(logits, bias_col):
    tm = logits.shape[0]
    s = jax.nn.sigmoid(logits.T[:N_EXPERTS])
    sel = s + bias_col
    row = [sel[e:e + 1] for e in range(N_EXPERTS)]
    score = []
    for g in range(N_GROUPS):
        a, b, c2, d2 = row[GROUP_SIZE * g:GROUP_SIZE * (g + 1)]
        hi1, lo1 = jnp.maximum(a, b), jnp.minimum(a, b)
        hi2, lo2 = jnp.maximum(c2, d2), jnp.minimum(c2, d2)
        score.append(jnp.maximum(hi1, hi2) + jnp.maximum(jnp.minimum(hi1, hi2), jnp.maximum(lo1, lo2)))
    best, gidx = score[0], jnp.zeros((1, tm), jnp.int32)
    for g in range(1, N_GROUPS):
        upd = score[g] > best
        gidx = jnp.where(upd, g, gidx)
        best = jnp.where(upd, score[g], best)
    neg = jnp.full((1, tm), -jnp.inf, F32)
    cand = [jnp.where(gidx == (e // GROUP_SIZE), row[e], neg) for e in range(N_EXPERTS)]

    def argmax_first(vals):
        bv, bi = vals[0], jnp.zeros((1, tm), jnp.int32)
        for e in range(1, N_EXPERTS):
            upd = vals[e] > bv
            bi = jnp.where(upd, e, bi)
            bv = jnp.where(upd, vals[e], bv)
        return bi

    i1 = argmax_first(cand)
    i2 = argmax_first([jnp.where(i1 == e, neg, cand[e]) for e in range(N_EXPERTS)])
    eidx = lax.broadcasted_iota(jnp.int32, (N_EXPERTS, tm), 0)
    w1 = jnp.sum(jnp.where(eidx == i1, s, 0.0), axis=0, keepdims=True)
    w2 = jnp.sum(jnp.where(eidx == i2, s, 0.0), axis=0, keepdims=True)
    den = w1 + w2
    return i1, i2, w1 / den, w2 / den


def _dense_gates(i1, i2, w1, w2):
    efull = lax.broadcasted_iota(jnp.int32, (LANE, i1.shape[1]), 0)
    return (jnp.where(efull == i1, w1, 0.0) + jnp.where(efull == i2, w2, 0.0)).T


INFO_E1, INFO_E2, INFO_W1, INFO_W2, INFO_R1, INFO_R2 = range(6)


def _routing_record(i1, i2, w1, w2, carry_ref):
    tm = i1.shape[1]
    efull = lax.broadcasted_iota(jnp.int32, (LANE, tm), 0)
    rec_t = (jnp.where(efull == INFO_E1, i1.astype(F32), 0.0) + jnp.where(efull == INFO_E2, i2.astype(F32), 0.0)
             + jnp.where(efull == INFO_W1, w1, 0.0) + jnp.where(efull == INFO_W2, w2, 0.0))
    rec = rec_t.T
    onehot = jnp.where((efull == i1) | (efull == i2), 1.0, 0.0).T
    r = lax.broadcasted_iota(jnp.int32, (tm, tm), 0)
    c = lax.broadcasted_iota(jnp.int32, (tm, tm), 1)
    incl = _dot(jnp.where(r >= c, 1.0, 0.0).astype(BF16), onehot.astype(BF16)) + carry_ref[...]
    carry_ref[...] = incl[tm - 1:tm, :]
    rank = incl - onehot
    lane = lax.broadcasted_iota(jnp.int32, (tm, LANE), 1)
    lane_f = lane.astype(F32)
    r1 = jnp.sum(jnp.where(lane_f == rec[:, INFO_E1:INFO_E1 + 1], rank, 0.0), axis=-1, keepdims=True)
    r2 = jnp.sum(jnp.where(lane_f == rec[:, INFO_E2:INFO_E2 + 1], rank, 0.0), axis=-1, keepdims=True)
    return jnp.where(lane == INFO_R1, r1, jnp.where(lane == INFO_R2, r2, rec))


def _pack_bf16_pairs(v):
    n = v.shape[1] // 2
    hi = lax.bitcast_convert_type(v[:, :n].astype(BF16).astype(F32), jnp.uint32)
    lo = lax.bitcast_convert_type(v[:, n:].astype(BF16).astype(F32), jnp.uint32)
    return hi | (lo >> 16)


def _unpack_bf16_pairs(w):
    first = lax.bitcast_convert_type(w & jnp.uint32(0xFFFF0000), F32)
    second = lax.bitcast_convert_type(w << 16, F32)
    return first, second


def _modulate_kernel(x_ref, g_ref, sh_ref, sc_ref, *rest, route):
    x = x_ref[...]
    h = _rms(x) * g_ref[...] * (1.0 + sc_ref[...]) + sh_ref[...]
    if route is None:
        (h_ref,) = rest
        h_ref[...] = h.astype(h_ref.dtype)
        return
    wr_ref, br_ref = rest[:2]
    h_hi = h.astype(BF16)
    h_lo = (h - h_hi.astype(F32)).astype(BF16)
    logits = _dot(h_hi, wr_ref[0]) + (_dot(h_lo, wr_ref[0]) + _dot(h_hi, wr_ref[1]))
    topk = _route_topk(logits, br_ref[...])
    if route == "dense":
        h_ref, gates_ref = rest[2:]
        h_ref[...] = h.astype(h_ref.dtype)
        gates_ref[...] = _dense_gates(*topk)
    else:
        h_ref, rec_ref, cnt_ref, carry_ref = rest[2:]

        @pl.when(pl.program_id(0) == 0)
        def _():
            carry_ref[...] = jnp.zeros_like(carry_ref)

        h_ref[...] = _pack_bf16_pairs(h)
        rec_ref[...] = _routing_record(*topk, carry_ref)
        cnt_ref[...] = carry_ref[...]


def _modulate(x, gain, shift, scale, router=None, route=None):
    m, d = x.shape
    tm = min(256, m)
    vec = pl.BlockSpec((1, d), lambda i: (0, 0))
    in_specs = [pl.BlockSpec((tm, d), lambda i: (i, 0)), vec, vec, vec]
    args = [x, gain, shift, scale]
    out_shape = [jax.ShapeDtypeStruct((m, d), BF16)]
    out_specs = [pl.BlockSpec((tm, d), lambda i: (i, 0))]
    scratch = []
    if route is not None:
        wr, br = router
        in_specs += [pl.BlockSpec((2, d, LANE), lambda i: (0, 0, 0)), pl.BlockSpec((N_EXPERTS, 1), lambda i: (0, 0))]
        args += [wr, br]
        out_shape.append(jax.ShapeDtypeStruct((m, LANE), F32))
        out_specs.append(pl.BlockSpec((tm, LANE), lambda i: (i, 0)))
    if route == "routed":
        out_shape[0] = jax.ShapeDtypeStruct((m, d // 2), jnp.uint32)
        out_specs[0] = pl.BlockSpec((tm, d // 2), lambda i: (i, 0))
        out_shape.append(jax.ShapeDtypeStruct((1, LANE), F32))
        out_specs.append(pl.BlockSpec((1, LANE), lambda i: (0, 0)))
        scratch = [pltpu.VMEM((1, LANE), F32)]
    out = pl.pallas_call(
        functools.partial(_modulate_kernel, route=route),
        out_shape=out_shape, grid=(m // tm,), in_specs=in_specs, out_specs=out_specs, scratch_shapes=scratch,
        compiler_params=_cp("arbitrary" if route == "routed" else "parallel"),
        name="modulate" if route is None else "modulate_route_" + route,
    )(*args)
    return out[0] if route is None else out


def _mm_kernel(*refs, n_pairs, residual, modulated):
    a_refs, w_refs = refs[:n_pairs], refs[n_pairs:2 * n_pairs]
    rest = refs[2 * n_pairs:]
    if modulated:
        (g_ref, sh_ref, sc_ref), rest, h_ref = rest[:3], rest[3:-1], rest[-1]

        @pl.when(pl.program_id(1) == 0)
        def _():
            h_ref[...] = (_rms(a_refs[0][...]) * g_ref[...] * (1.0 + sc_ref[...]) + sh_ref[...]).astype(BF16)

        acc = _dot(h_ref[...], w_refs[0][...])
    else:
        acc = _dot(a_refs[0][...], w_refs[0][...])
    o_ref = rest[-1]
    for a_ref, w_ref in zip(a_refs[1:], w_refs[1:]):
        acc += _dot(a_ref[...], w_ref[...])
    if residual:
        x_ref, gate_ref = rest[:2]
        acc = x_ref[...] + gate_ref[...] * acc
    o_ref[...] = acc.astype(o_ref.dtype)


def _mm(pairs, *, out_dtype, residual=None, modulation=None, tm=1024, tn=512, name="mm"):
    m = pairs[0][0].shape[0]
    n = pairs[0][1].shape[1]
    if modulation is not None:
        tm = tm // 2
        tn = next(c for c in (1280, 1024, 512, 256, 128) if n % c == 0)
    tm, tn = min(tm, m), min(tn, n)
    assert m % tm == 0 and n % tn == 0, (m, n, tm, tn)
    in_specs = [pl.BlockSpec((tm, a.shape[1]), lambda i, j: (i, 0)) for a, _ in pairs]
    in_specs += [pl.BlockSpec((w.shape[0], tn), lambda i, j: (0, j)) for _, w in pairs]
    args = [a for a, _ in pairs] + [w for _, w in pairs]
    scratch = []
    if modulation is not None:
        k0 = pairs[0][0].shape[1]
        in_specs += [pl.BlockSpec((1, k0), lambda i, j: (0, 0))] * 3
        args += list(modulation)
        scratch = [pltpu.VMEM((tm, k0), BF16)]
    if residual is not None:
        in_specs += [pl.BlockSpec((tm, tn), lambda i, j: (i, j)), pl.BlockSpec((1, tn), lambda i, j: (0, j))]
        args += list(residual)
    return pl.pallas_call(
        functools.partial(_mm_kernel, n_pairs=len(pairs), residual=residual is not None,
                          modulated=modulation is not None),
        out_shape=jax.ShapeDtypeStruct((m, n), out_dtype),
        grid=(m // tm, n // tn), in_specs=in_specs,
        out_specs=pl.BlockSpec((tm, tn), lambda i, j: (i, j)),
        scratch_shapes=scratch,
        compiler_params=_cp("parallel", "arbitrary"),
        name=name,
    )(*args)


def _rope_rotate(v, cos, sin_signed):
    lane = lax.broadcasted_iota(jnp.int32, (1, LANE), 1)
    first_half = (lane % 32) < 16
    partner = jnp.where(first_half, pltpu.roll(v, LANE - 16, axis=1), pltpu.roll(v, 16, axis=1))
    return v * cos + partner * sin_signed


def _norm_bound_sq(gn, gr):
    return (MLA_NOPE * jnp.max(gn * gn, axis=-1, keepdims=True)
            + MLA_ROPE * jnp.max(gr * gr, axis=-1, keepdims=True))


def _q_up_kernel(cq_ref, ga_ref, w_ref, gn_ref, gr_ref, cos_ref, sin_ref, kgn_ref, kgr_ref, o_ref, *, scale):
    cqn = (_rms(cq_ref[...].astype(F32)) * ga_ref[...]).astype(BF16)
    q = _dot(cqn, w_ref[...])
    cos, sin = cos_ref[...], sin_ref[...]
    gn, gr = gn_ref[...] * scale, gr_ref[...] * scale
    lane = lax.broadcasted_iota(jnp.int32, (1, LANE), 1)
    offset = SCORE_CENTER - jnp.sqrt(_norm_bound_sq(gn, gr) * _norm_bound_sq(kgn_ref[...], kgr_ref[...]))
    for h in range(MLA_HEADS):
        o = h * MLA_HEAD_PAD
        qr = _rope_rotate(_rms(q[:, o + LANE:o + 2 * LANE], MLA_ROPE) * gr, cos, sin)
        o_ref[:, o:o + LANE] = (_rms(q[:, o:o + LANE]) * gn).astype(BF16)
        o_ref[:, o + LANE:o + 2 * LANE] = jnp.where(lane == OFFSET_LANE, offset, qr).astype(BF16)


def _q_up(px, ga, w, gn, gr, cos, sin, kgn, kgr):
    m = px.shape[0]
    tm = min(512, m)
    n = MLA_HEADS * MLA_HEAD_PAD
    vec = lambda width: pl.BlockSpec((1, width), lambda i: (0, 0))
    return pl.pallas_call(
        functools.partial(_q_up_kernel, scale=MLA_QK ** -0.5 * LOG2E),
        out_shape=jax.ShapeDtypeStruct((m, n), BF16),
        grid=(m // tm,),
        in_specs=[pl.BlockSpec((tm, MLA_Q_RANK), lambda i: (i, 0)), vec(MLA_Q_RANK),
                  pl.BlockSpec((MLA_Q_RANK, n), lambda i: (0, 0)), vec(LANE), vec(LANE),
                  pl.BlockSpec((tm, LANE), lambda i: (i, 0)), pl.BlockSpec((tm, LANE), lambda i: (i, 0)),
                  vec(LANE), vec(LANE)],
        out_specs=pl.BlockSpec((tm, n), lambda i: (i, 0)),
        compiler_params=_cp("parallel"),
        name="mla_q_up",
    )(px, ga, w, gn, gr, cos, sin, kgn, kgr)


def _kv_up_kernel(ckv_ref, kr_ref, ga_ref, w_ref, gn_ref, gr_ref, cos_ref, sin_ref, k_ref, v_ref):
    ckvn = (_rms(ckv_ref[...].astype(F32)) * ga_ref[...]).astype(BF16)
    kv = _dot(ckvn, w_ref[...])
    kr = _rms(kr_ref[...].astype(F32), MLA_ROPE) * gr_ref[...]
    kr = _rope_rotate(kr, cos_ref[...], sin_ref[...])
    lane = lax.broadcasted_iota(jnp.int32, (1, LANE), 1)
    krb = jnp.where(lane == OFFSET_LANE, 1.0, kr).astype(BF16)
    gn = gn_ref[...]
    for h in range(MLA_HEADS):
        o = h * MLA_HEAD_PAD
        k_ref[:, o:o + LANE] = (_rms(kv[:, o:o + LANE]) * gn).astype(BF16)
        k_ref[:, o + LANE:o + 2 * LANE] = krb
        v_ref[:, h * MLA_V:(h + 1) * MLA_V] = kv[:, o + LANE:o + 2 * LANE].astype(BF16)


def _kv_up(px, ckv_blk, kr_blk, ga, w, gn, gr, cos, sin):
    m = px.shape[0]
    tm = min(512, m)
    vec = lambda width: pl.BlockSpec((1, width), lambda i: (0, 0))
    return pl.pallas_call(
        _kv_up_kernel,
        out_shape=[jax.ShapeDtypeStruct((m, MLA_HEADS * MLA_HEAD_PAD), BF16),
                   jax.ShapeDtypeStruct((m, MLA_HEADS * MLA_V), BF16)],
        grid=(m // tm,),
        in_specs=[pl.BlockSpec((tm, MLA_KV_RANK), lambda i: (i, ckv_blk)),
                  pl.BlockSpec((tm, LANE), lambda i: (i, kr_blk)), vec(MLA_KV_RANK),
                  pl.BlockSpec((MLA_KV_RANK, MLA_HEADS * MLA_HEAD_PAD), lambda i: (0, 0)), vec(LANE), vec(LANE),
                  pl.BlockSpec((tm, LANE), lambda i: (i, 0)), pl.BlockSpec((tm, LANE), lambda i: (i, 0))],
        out_specs=[pl.BlockSpec((tm, MLA_HEADS * MLA_HEAD_PAD), lambda i: (i, 0)),
                   pl.BlockSpec((tm, MLA_HEADS * MLA_V), lambda i: (i, 0))],
        compiler_params=_cp("parallel"),
        name="mla_kv_up",
    )(px, px, ga, w, gn, gr, cos, sin)


def _attn_kernel(q_ref, k_ref, v_ref, o_ref, sa_ref, sb_ref, m_ref, l_ref, acc_ref, *, ck, n_chunks):
    def sub(i):
        return pl.ds(pl.multiple_of(i * ATTN_SUB, ATTN_SUB), ATTN_SUB)

    l_ref[...] = jnp.zeros_like(l_ref)
    acc_ref[...] = jnp.zeros_like(acc_ref)

    n_sub = n_chunks * ck // ATTN_SUB
    unroll = next(u for u in (13, 8, 5, 4, 3, 2, 1) if n_sub % u == 0)

    def stream(t, carry):
        lsum = l_ref[...]
        acc = acc_ref[...]
        for u in range(unroll):
            i = t * unroll + u
            p = jnp.exp2(_dot_nt(q_ref[...], k_ref[sub(i), :]))
            for part in range(ATTN_SUB // LANE):
                lsum += p[:, part * LANE:(part + 1) * LANE]
            acc += _dot(p.astype(BF16), v_ref[sub(i), :])
        l_ref[...] = lsum
        acc_ref[...] = acc
        return carry

    lax.fori_loop(0, n_sub // unroll, stream, 0)
    row_sum = jnp.sum(l_ref[...], axis=-1, keepdims=True)
    in_range = jnp.min(row_sum) >= SCORE_SUM_FLOOR

    @pl.when(in_range)
    def _():
        o_ref[...] = (acc_ref[...] / row_sum).astype(o_ref.dtype)

    @pl.when(jnp.logical_not(in_range))
    def _():
        _attn_online(q_ref, k_ref, v_ref, o_ref, sa_ref, sb_ref, m_ref, l_ref, acc_ref, ck=ck, n_chunks=n_chunks)


def _attn_online(q_ref, k_ref, v_ref, o_ref, sa_ref, sb_ref, m_ref, l_ref, acc_ref, *, ck, n_chunks):
    m_ref[...] = jnp.full_like(m_ref, -jnp.inf)
    l_ref[...] = jnp.zeros_like(l_ref)
    acc_ref[...] = jnp.zeros_like(acc_ref)

    def rows(c):
        return pl.ds(pl.multiple_of(c * ck, ck), ck)

    def scores(c, s_ref):
        s_ref[...] = _dot_nt(q_ref[...], k_ref[rows(c), :])

    def update(c, s_ref):
        m_prev = m_ref[...]
        m_new = jnp.maximum(m_prev, jnp.max(s_ref[...], axis=-1, keepdims=True))
        alpha = jnp.exp2(m_prev - m_new)
        m_ref[...] = m_new
        lsum = alpha * l_ref[...]
        acc = alpha * acc_ref[...]
        for j in range(ck // ATTN_SUB):
            p = jnp.exp2(s_ref[:, j * ATTN_SUB:(j + 1) * ATTN_SUB] - m_new)
            for t in range(ATTN_SUB // LANE):
                lsum += p[:, t * LANE:(t + 1) * LANE]
            acc += _dot(p.astype(BF16), v_ref[pl.ds(pl.multiple_of(c * ck + j * ATTN_SUB, ATTN_SUB), ATTN_SUB), :])
        l_ref[...] = lsum
        acc_ref[...] = acc

    bufs = (sa_ref, sb_ref)
    scores(0, sa_ref)
    group = 4 if (n_chunks - 1) % 4 == 0 else 2
    n_groups = (n_chunks - 1) // group

    def body(t, carry):
        c = group * t
        for u in range(group):
            scores(c + u + 1, bufs[(u + 1) % 2])
            update(c + u, bufs[u % 2])
        return carry

    if n_groups:
        lax.fori_loop(0, n_groups, body, 0)
    done = group * n_groups
    for c in range(done, n_chunks):
        if c + 1 < n_chunks:
            scores(c + 1, bufs[(c + 1) % 2])
        update(c, bufs[c % 2])
    o_ref[...] = (acc_ref[...] / jnp.sum(l_ref[...], axis=-1, keepdims=True)).astype(o_ref.dtype)


ATTN_SUB = 2 * LANE
OFFSET_LANE = MLA_ROPE
SCORE_CENTER = 64.0
SCORE_SUM_FLOOR = 2.0 ** -60


def _attn_tiles(tq_total, tk_total):
    tq = min(1024, tq_total)
    ck = tk_total
    for cand in (1280, 1024, 512, 256):
        if tk_total % cand == 0:
            ck = cand
            break
    return tq, ck


def _attention(q, k, v):
    tq_total, tk_total = q.shape[0], k.shape[0]
    tq, ck = _attn_tiles(tq_total, tk_total)
    return pl.pallas_call(
        functools.partial(_attn_kernel, ck=ck, n_chunks=tk_total // ck),
        out_shape=jax.ShapeDtypeStruct((tq_total, MLA_HEADS * MLA_V), BF16),
        grid=(MLA_HEADS, tq_total // tq),
        in_specs=[pl.BlockSpec((tq, MLA_HEAD_PAD), lambda h, i: (i, h)),
                  pl.BlockSpec((tk_total, MLA_HEAD_PAD), lambda h, i: (0, h), pipeline_mode=pl.Buffered(1)),
                  pl.BlockSpec((tk_total, MLA_V), lambda h, i: (0, h), pipeline_mode=pl.Buffered(1))],
        out_specs=pl.BlockSpec((tq, MLA_V), lambda h, i: (i, h)),
        scratch_shapes=[pltpu.VMEM((tq, ck), F32), pltpu.VMEM((tq, ck), F32),
                        pltpu.VMEM((tq, 1), F32), pltpu.VMEM((tq, LANE), F32), pltpu.VMEM((tq, MLA_V), F32)],
        compiler_params=_cp("parallel", "arbitrary"),
        name="mla_attention",
    )(q, k, v)


CONV_TILE = 512
HALO = 16


def _conv_kernel(gb_ref, gc_ref, u_ref, gcp_ref, up_ref, gcn_ref, un_ref, w_ref, b_ref, o_ref):
    i = pl.program_id(0)
    z = gc_ref[...].astype(F32) * u_ref[...].astype(F32)
    tm = z.shape[0]
    z_before = gcp_ref[HALO - 1:HALO, :].astype(F32) * up_ref[HALO - 1:HALO, :].astype(F32)
    z_after = gcn_ref[0:1, :].astype(F32) * un_ref[0:1, :].astype(F32)
    z_before = jnp.where(i > 0, z_before, 0.0)
    z_after = jnp.where(i < pl.num_programs(0) - 1, z_after, 0.0)
    rowid = lax.broadcasted_iota(jnp.int32, z.shape, 0)
    z_prev = jnp.where(rowid == 0, z_before, pltpu.roll(z, 1, axis=0))
    z_next = jnp.where(rowid == tm - 1, z_after, pltpu.roll(z, tm - 1, axis=0))
    w = w_ref[...]
    y = z_prev * w[0:1] + z * w[1:2] + z_next * w[2:3] + b_ref[...]
    o_ref[...] = (gb_ref[...].astype(F32) * y).astype(o_ref.dtype)


def _conv(px, col0, conv_w, conv_b):
    m = px.shape[0]
    tm = min(512, m)
    cb = CONV_WIDTH // CONV_TILE
    b0 = col0 // CONV_TILE
    nhalo = m // HALO
    main = lambda off: pl.BlockSpec((tm, CONV_TILE), lambda i, j: (i, b0 + off * cb + j))
    prev = lambda off: pl.BlockSpec(
        (HALO, CONV_TILE), lambda i, j: (jnp.maximum(i * (tm // HALO) - 1, 0), b0 + off * cb + j))
    nxt = lambda off: pl.BlockSpec(
        (HALO, CONV_TILE), lambda i, j: (jnp.minimum((i + 1) * (tm // HALO), nhalo - 1), b0 + off * cb + j))
    return pl.pallas_call(
        _conv_kernel,
        out_shape=jax.ShapeDtypeStruct((m, CONV_WIDTH), BF16),
        grid=(m // tm, cb),
        in_specs=[main(0), main(1), main(2), prev(1), prev(2), nxt(1), nxt(2),
                  pl.BlockSpec((3, CONV_TILE), lambda i, j: (0, j)),
                  pl.BlockSpec((1, CONV_TILE), lambda i, j: (0, j))],
        out_specs=pl.BlockSpec((tm, CONV_TILE), lambda i, j: (i, j)),
        compiler_params=_cp("parallel", "parallel"),
        name="gated_conv3",
    )(px, px, px, px, px, px, px, conv_w, conv_b)


def _moe_up_kernel(h_ref, wg_ref, wu_ref, gates_ref, o_ref):
    e = pl.program_id(1)
    h = h_ref[...]
    a = _dot(h, wg_ref[...])
    u = _dot(h, wu_ref[...])
    gates = gates_ref[...]
    lane = lax.broadcasted_iota(jnp.int32, gates.shape, 1)
    g = jnp.sum(jnp.where(lane == e, gates, 0.0), axis=-1, keepdims=True)
    o_ref[...] = (a * jax.nn.sigmoid(a) * u * g).astype(o_ref.dtype)


def _moe_up(h, wg, wu, gates):
    m, d = h.shape
    tm = min(1024, m)
    return pl.pallas_call(
        _moe_up_kernel,
        out_shape=jax.ShapeDtypeStruct((m, N_EXPERTS * D_FF), BF16),
        grid=(m // tm, N_EXPERTS),
        in_specs=[pl.BlockSpec((tm, d), lambda i, e: (i, 0)),
                  pl.BlockSpec((None, d, D_FF), lambda i, e: (e, 0, 0)),
                  pl.BlockSpec((None, d, D_FF), lambda i, e: (e, 0, 0)),
                  pl.BlockSpec((tm, LANE), lambda i, e: (i, 0))],
        out_specs=pl.BlockSpec((tm, D_FF), lambda i, e: (i, e)),
        compiler_params=_cp("parallel", "arbitrary"),
        name="moe_up",
    )(h, wg, wu, gates)


MOE_TOK_TILE = 256
MOE_ROW_TILE = 256


def _row_copy(src, src_row, dst, dst_row, sem):
    return pltpu.make_async_copy(src.at[pl.ds(src_row, 1)], dst.at[pl.ds(dst_row, 1)], sem)


def _moe_dispatch_kernel(pos_ref, pad_ref, h_ref, xs_ref, sem):
    tm = h_ref.shape[0]
    n_pad = pad_ref.shape[1]

    def token_rows(start):
        def one(r, carry):
            for k in range(2):
                cp = _row_copy(h_ref, r, xs_ref, pos_ref[0, 2 * r + k], sem)
                cp.start() if start else cp.wait()
            return carry

        lax.fori_loop(0, tm, one, 0, unroll=8)

    def pad_rows(start):
        def one(j, carry):
            cp = _row_copy(h_ref, 0, xs_ref, pad_ref[0, j], sem)
            cp.start() if start else cp.wait()
            return carry

        lax.fori_loop(0, n_pad, one, 0, unroll=8)

    token_rows(True)
    pad_rows(True)
    token_rows(False)
    pad_rows(False)


def _moe_dispatch(h_packed, pos, pad_pos, n_rows):
    m, w = h_packed.shape
    tm = MOE_TOK_TILE
    n_pad = pad_pos.shape[-1]
    return pl.pallas_call(
        _moe_dispatch_kernel,
        out_shape=jax.ShapeDtypeStruct((n_rows, w), jnp.uint32),
        grid=(m // tm,),
        in_specs=[pl.BlockSpec((None, 1, 2 * tm), lambda i: (i, 0, 0), memory_space=pltpu.SMEM),
                  pl.BlockSpec((None, 1, n_pad), lambda i: (i, 0, 0), memory_space=pltpu.SMEM),
                  pl.BlockSpec((tm, w), lambda i: (i, 0))],
        out_specs=pl.BlockSpec(memory_space=pl.ANY),
        scratch_shapes=[pltpu.SemaphoreType.DMA],
        compiler_params=_cp("arbitrary"),
        name="moe_dispatch",
    )(pos, pad_pos, h_packed)


def _moe_experts_kernel(te_ref, xs_ref, wg_ref, wu_ref, wd_ref, ys_ref):
    del te_ref
    half = xs_ref.shape[1]
    x1, x2 = _unpack_bf16_pairs(xs_ref[...])
    x1, x2 = x1.astype(BF16), x2.astype(BF16)
    a = _dot(x1, wg_ref[:half, :]) + _dot(x2, wg_ref[half:, :])
    u = _dot(x1, wu_ref[:half, :]) + _dot(x2, wu_ref[half:, :])
    hid = (a * jax.nn.sigmoid(a) * u).astype(BF16)
    ys_ref[...] = _pack_bf16_pairs(_dot(hid, wd_ref[...]))


def _moe_experts(tile_expert, xs, wg, wu, wd):
    n_rows, w = xs.shape
    d = 2 * w
    t = MOE_ROW_TILE
    return pl.pallas_call(
        _moe_experts_kernel,
        out_shape=jax.ShapeDtypeStruct((n_rows, w), jnp.uint32),
        grid_spec=pltpu.PrefetchScalarGridSpec(
            num_scalar_prefetch=1, grid=(n_rows // t,),
            in_specs=[pl.BlockSpec((t, w), lambda i, te: (i, 0)),
                      pl.BlockSpec((None, d, D_FF), lambda i, te: (te[i], 0, 0)),
                      pl.BlockSpec((None, d, D_FF), lambda i, te: (te[i], 0, 0)),
                      pl.BlockSpec((None, D_FF, d), lambda i, te: (te[i], 0, 0))],
            out_specs=pl.BlockSpec((t, w), lambda i, te: (i, 0))),
        compiler_params=_cp("arbitrary"),
        name="moe_experts",
    )(tile_expert, xs, wg, wu, wd)


def _moe_combine_kernel(pos_ref, posn_ref, x_ref, gate_ref, rec_ref, ys_ref, o_ref, g1_ref, g2_ref, sem):
    _, tm, half = g1_ref.shape
    bufs = (g1_ref, g2_ref)
    i = pl.program_id(0)
    slot = i % 2

    def row_copy(p_ref, s, r, k):
        return _row_copy(ys_ref, p_ref[0, 2 * r + k], bufs[k].at[s], r, sem.at[s])

    def all_rows(p_ref, s, start):
        def one(r, carry):
            for k in range(2):
                cp = row_copy(p_ref, s, r, k)
                cp.start() if start else cp.wait()
            return carry

        lax.fori_loop(0, tm, one, 0, unroll=8)

    @pl.when(i == 0)
    def _():
        all_rows(pos_ref, slot, True)

    all_rows(pos_ref, slot, False)
    rb = 2 * SUBLANE

    def combine(prefetch, b, carry):
        rs = pl.ds(pl.multiple_of(b * rb, rb), rb)
        rec = rec_ref[rs, :]
        w1, w2 = rec[:, INFO_W1:INFO_W1 + 1], rec[:, INFO_W2:INFO_W2 + 1]
        y1 = _unpack_bf16_pairs(g1_ref[slot, rs, :])
        y2 = _unpack_bf16_pairs(g2_ref[slot, rs, :])
        for part in range(2):
            cols = slice(part * half, (part + 1) * half)
            o_ref[rs, cols] = x_ref[rs, cols] + gate_ref[:, cols] * (w1 * y1[part] + w2 * y2[part])
        if prefetch:
            for r in range(rb):
                for k in range(2):
                    row_copy(posn_ref, 1 - slot, b * rb + r, k).start()
        return carry

    has_next = i + 1 < pl.num_programs(0)

    @pl.when(has_next)
    def _():
        lax.fori_loop(0, tm // rb, functools.partial(combine, True), 0)

    @pl.when(jnp.logical_not(has_next))
    def _():
        lax.fori_loop(0, tm // rb, functools.partial(combine, False), 0)


def _moe_combine(pos, xres, gate, rec, ys):
    m, d = xres.shape
    tm = MOE_TOK_TILE
    last = m // tm - 1
    return pl.pallas_call(
        _moe_combine_kernel,
        out_shape=jax.ShapeDtypeStruct((m, d), F32),
        grid=(m // tm,),
        in_specs=[pl.BlockSpec((None, 1, 2 * tm), lambda i: (i, 0, 0), memory_space=pltpu.SMEM),
                  pl.BlockSpec((None, 1, 2 * tm), lambda i: (jnp.minimum(i + 1, last), 0, 0),
                               memory_space=pltpu.SMEM),
                  pl.BlockSpec((tm, d), lambda i: (i, 0)),
                  pl.BlockSpec((1, d), lambda i: (0, 0)),
                  pl.BlockSpec((tm, LANE), lambda i: (i, 0)),
                  pl.BlockSpec(memory_space=pl.ANY)],
        out_specs=pl.BlockSpec((tm, d), lambda i: (i, 0)),
        scratch_shapes=[pltpu.VMEM((2, tm, d // 2), jnp.uint32), pltpu.VMEM((2, tm, d // 2), jnp.uint32),
                        pltpu.SemaphoreType.DMA((2,))],
        compiler_params=_cp("arbitrary"),
        name="moe_combine",
    )(pos, pos, xres, gate, rec, ys)


def _moe_routed(xres, gain, shift, scale, gate, router, wg, wu, wd3):
    m = xres.shape[0]
    tm, t = MOE_TOK_TILE, MOE_ROW_TILE
    h_packed, rec, counts = _modulate(xres, gain, shift, scale, router=router, route="routed")
    n_rows = 2 * m + N_EXPERTS * t
    cnt = counts[0, :N_EXPERTS].astype(jnp.int32)
    seg = ((cnt + t - 1) // t) * t
    seg_end = jnp.cumsum(seg)
    seg_start = seg_end - seg
    e12 = rec[:, INFO_E1:INFO_E2 + 1].astype(jnp.int32)
    r12 = rec[:, INFO_R1:INFO_R2 + 1].astype(jnp.int32)
    pos = (seg_start[e12] + r12).reshape(m // tm, 1, 2 * tm)
    tile_start = jnp.arange(n_rows // t, dtype=jnp.int32) * t
    tile_expert = jnp.minimum(jnp.sum(tile_start[:, None] >= seg_end[None, :], axis=1), N_EXPERTS - 1).astype(jnp.int32)
    n_steps = m // tm
    assert (n_rows - 2 * m) % n_steps == 0
    pad_cnt = jnp.concatenate([seg - cnt, n_rows - seg_end[-1:]])
    pad_end = jnp.cumsum(pad_cnt)
    pad_first = jnp.concatenate([seg_start + cnt, seg_end[-1:]])
    j = jnp.arange(n_rows - 2 * m, dtype=jnp.int32)
    rng = jnp.sum(j[:, None] >= pad_end[None, :], axis=1)
    pad_pos = (pad_first[rng] + (j - (pad_end - pad_cnt)[rng])).astype(jnp.int32).reshape(n_steps, 1, -1)
    xs = _moe_dispatch(h_packed, pos, pad_pos, n_rows)
    ys = _moe_experts(tile_expert, xs, wg, wu, wd3)
    return _moe_combine(pos, xres, gate, rec, ys)


GLA_LEVELS = 6
GLA_MILD_DECAY = 50.0


def _gla_tables(reverse):
    n = GLA_CHUNK
    t = np.arange(n)[:, None]
    r = np.arange(n)[None, :]
    blocks = [r <= t, r > t]
    masks = []
    for lvl in range(GLA_LEVELS):
        bs = 1 << lvl
        blocks.append((r >= (t & ~(bs - 1))) & (r <= t))
        blocks.append((r > t) & (r <= (t | (bs - 1))))
        masks.append(((t >> (lvl + 1)) == (r >> (lvl + 1))) & (((t >> lvl) & 1) == 1) & (((r >> lvl) & 1) == 0))
    masks.append(t == r)
    masks.append(r <= t)
    e = np.stack(blocks).astype(np.float32)
    msk = np.stack(masks).astype(np.float32)
    if reverse:
        e, msk = e[:, ::-1, ::-1], msk[:, ::-1, ::-1]
    return jnp.asarray(e.reshape(-1, n)), jnp.asarray(msk)


def _gla_intra_kernel(k_ref, v_ref, q_ref, a_ref, wa_ref, ba_ref, e_ref, m_ref,
                      oi_ref, qd_ref, kd_ref, dl_ref, la_ref, tot_ref, *, nchunk, reverse):
    n = GLA_CHUNK
    scale = GLA_DK ** -0.5

    a = a_ref[...]
    z = _dot(a, wa_ref[0]) + _dot(a, wa_ref[1]) + _dot(a, wa_ref[2]) + ba_ref[...]
    la = (jnp.minimum(z, 0.0) - jnp.log1p(jnp.exp(-jnp.abs(z)))) * (1.0 / GLA_TEMP)
    la_ref[...] = la
    totals = jnp.sum(la.reshape(nchunk, n, GLA_DK), axis=1)
    tot_ref[...] = totals
    dl_ref[...] = jnp.exp(totals)
    mild = jnp.min(totals) > -GLA_MILD_DECAY
    rowid = lax.broadcasted_iota(jnp.int32, (n, GLA_DK), 0)

    def load(c):
        rows = pl.ds(pl.multiple_of(c * n, n), n)
        return rows, q_ref[rows, :].astype(F32) * scale, k_ref[rows, :].astype(F32)

    def mild_chunk(c, carry):
        rows, q, k = load(c)
        la_c = la_ref[rows, :]
        total = tot_ref[pl.ds(c, 1), :]
        cum = la_c
        shift = 1
        while shift < n:
            cum = cum + jnp.where(rowid >= shift, pltpu.roll(cum, shift, axis=0), 0.0)
            shift *= 2
        if reverse:
            cum = total - cum + la_c
        qd = (q * jnp.exp(cum)).astype(BF16)
        kinv = k * jnp.exp(-cum)
        qd_ref[rows, :] = qd
        kd_ref[rows, :] = (kinv * jnp.exp(total)).astype(BF16)
        att = m_ref[GLA_LEVELS + 1] * _dot_nt(qd, kinv.astype(BF16))
        oi_ref[rows, :] = _dot(att.astype(BF16), v_ref[rows, :]).astype(BF16)
        return carry

    def harsh_chunk(c, carry):
        rows, q, k = load(c)
        ex = _dot_f32(e_ref[...], la_ref[rows, :])
        qd_ref[rows, :] = (q * jnp.exp(ex[0:n])).astype(BF16)
        kd_ref[rows, :] = (k * jnp.exp(ex[n:2 * n])).astype(BF16)
        att = m_ref[GLA_LEVELS] * _dot_nt(q.astype(BF16), k.astype(BF16))
        for lvl in range(GLA_LEVELS):
            qs = (q * jnp.exp(ex[(2 + 2 * lvl) * n:(3 + 2 * lvl) * n])).astype(BF16)
            ks = (k * jnp.exp(ex[(3 + 2 * lvl) * n:(4 + 2 * lvl) * n])).astype(BF16)
            att += m_ref[lvl] * _dot_nt(qs, ks)
        oi_ref[rows, :] = _dot(att.astype(BF16), v_ref[rows, :]).astype(BF16)
        return carry

    @pl.when(mild)
    def _():
        lax.fori_loop(0, nchunk, mild_chunk, 0, unroll=2)

    @pl.when(jnp.logical_not(mild))
    def _():
        lax.fori_loop(0, nchunk, harsh_chunk, 0)


def _gla_blocks(m):
    tb = min(512, m)
    return tb, tb // GLA_CHUNK


def _gla_intra(px, cols, wa_pad, ba, reverse):
    m = px.shape[0]
    tb, nchunk = _gla_blocks(m)
    kb, vb, qb, ab = cols
    e, msk = _gla_tables(reverse)
    return pl.pallas_call(
        functools.partial(_gla_intra_kernel, nchunk=nchunk, reverse=reverse),
        out_shape=[jax.ShapeDtypeStruct((m, GLA_VAL), BF16), jax.ShapeDtypeStruct((m, GLA_KEY), BF16),
                   jax.ShapeDtypeStruct((m, GLA_KEY), BF16), jax.ShapeDtypeStruct((m // GLA_CHUNK, GLA_KEY), F32)],
        grid=(m // tb, GLA_HEADS),
        in_specs=[pl.BlockSpec((tb, GLA_DK), lambda i, h: (i, kb + h)),
                  pl.BlockSpec((tb, GLA_DV), lambda i, h: (i, vb + h)),
                  pl.BlockSpec((tb, GLA_DK), lambda i, h: (i, qb + h)),
                  pl.BlockSpec((tb, LANE), lambda i, h: (i, ab)),
                  pl.BlockSpec((3, LANE, GLA_DK), lambda i, h: (0, 0, h)),
                  pl.BlockSpec((1, GLA_DK), lambda i, h: (0, h)),
                  pl.BlockSpec(e.shape, lambda i, h: (0, 0)),
                  pl.BlockSpec(msk.shape, lambda i, h: (0, 0, 0))],
        out_specs=[pl.BlockSpec((tb, GLA_DV), lambda i, h: (i, h)),
                   pl.BlockSpec((tb, GLA_DK), lambda i, h: (i, h)),
                   pl.BlockSpec((tb, GLA_DK), lambda i, h: (i, h)),
                   pl.BlockSpec((nchunk, GLA_DK), lambda i, h: (i, h))],
        scratch_shapes=[pltpu.VMEM((tb, GLA_DK), F32), pltpu.VMEM((nchunk, GLA_DK), F32)],
        compiler_params=_cp("parallel", "parallel"),
        name="gla_intra_bwd" if reverse else "gla_intra_fwd",
    )(px, px, px, px, wa_pad, ba, e, msk)


def _gla_scan_kernel(*refs, nchunk):
    i = pl.program_id(1)
    n = GLA_CHUNK
    ins, outs, states = refs[:12], refs[12:16], refs[16:]
    dirs = [(ins[6 * d:6 * d + 6], outs[2 * d:2 * d + 2], states[d], bool(d)) for d in range(2)]

    @pl.when(i == 0)
    def _():
        for (_, _, _, _, _, s0_ref), _, s_ref, _ in dirs:
            s_ref[...] = s0_ref[...]

    def chunk(cc, carry):
        for (qd_ref, kd_ref, v_ref, dl_ref, oi_ref, _), (o_ref, _), s_ref, reverse in dirs:
            c = (nchunk - 1 - cc) if reverse else cc
            rows = pl.ds(pl.multiple_of(c * n, n), n)
            st = s_ref[...]
            o_ref[rows, :] = (oi_ref[rows, :].astype(F32) + _dot_nt(qd_ref[rows, :], st.astype(BF16))).astype(BF16)
            s_ref[...] = st * dl_ref[pl.ds(c, 1), :] + _dot_tn(v_ref[rows, :], kd_ref[rows, :])
        return carry

    lax.fori_loop(0, nchunk, chunk, 0, unroll=True)

    @pl.when(i == pl.num_programs(1) - 1)
    def _():
        for _, (_, sf_ref), s_ref, _ in dirs:
            sf_ref[...] = s_ref[...]


def _gla_scan(px, vb, fwd, bwd):
    m = px.shape[0]
    tb, nchunk = _gla_blocks(m)
    nb = m // tb

    def specs(blk):
        return [pl.BlockSpec((tb, GLA_DK), lambda h, i: (blk(i), h)),
                pl.BlockSpec((tb, GLA_DK), lambda h, i: (blk(i), h)),
                pl.BlockSpec((tb, GLA_DV), lambda h, i: (blk(i), vb + h)),
                pl.BlockSpec((nchunk, GLA_DK), lambda h, i: (blk(i), h)),
                pl.BlockSpec((tb, GLA_DV), lambda h, i: (blk(i), h)),
                pl.BlockSpec((None, GLA_DV, GLA_DK), lambda h, i: (h, 0, 0))]

    def out_specs(blk):
        return [pl.BlockSpec((tb, GLA_DV), lambda h, i: (blk(i), h)),
                pl.BlockSpec((None, GLA_DV, GLA_DK), lambda h, i: (h, 0, 0))]

    ahead, back = (lambda i: i), (lambda i: nb - 1 - i)
    out_shape = [jax.ShapeDtypeStruct((m, GLA_VAL), BF16), jax.ShapeDtypeStruct((GLA_HEADS, GLA_DV, GLA_DK), F32)]
    args = []
    for qd, kd, dl, oi, s0 in (fwd, bwd):
        args += [qd, kd, px, dl, oi, s0]
    return pl.pallas_call(
        functools.partial(_gla_scan_kernel, nchunk=nchunk),
        out_shape=out_shape + out_shape,
        grid=(GLA_HEADS, nb),
        in_specs=specs(ahead) + specs(back),
        out_specs=out_specs(ahead) + out_specs(back),
        scratch_shapes=[pltpu.VMEM((GLA_DV, GLA_DK), F32), pltpu.VMEM((GLA_DV, GLA_DK), F32)],
        compiler_params=_cp("parallel", "arbitrary"),
        name="gla_scan",
    )(*args)


def _gla_gate_kernel(of_ref, ob_ref, g_ref, gain_ref, o_ref):
    gain = gain_ref[...]
    for h in range(GLA_HEADS):
        sl = slice(h * GLA_DV, (h + 1) * GLA_DV)
        o = _rms(of_ref[:, sl].astype(F32) + ob_ref[:, sl].astype(F32)) * gain
        g = g_ref[:, sl].astype(F32)
        o_ref[:, sl] = (o * (g * jax.nn.sigmoid(g))).astype(BF16)


def _gla_gate(o_f, o_b, px, g_blk, gain):
    m = o_f.shape[0]
    tm = min(256, m)
    return pl.pallas_call(
        _gla_gate_kernel,
        out_shape=jax.ShapeDtypeStruct((m, GLA_VAL), BF16),
        grid=(m // tm,),
        in_specs=[pl.BlockSpec((tm, GLA_VAL), lambda i: (i, 0)), pl.BlockSpec((tm, GLA_VAL), lambda i: (i, 0)),
                  pl.BlockSpec((tm, GLA_VAL), lambda i: (i, g_blk)), pl.BlockSpec((1, GLA_DV), lambda i: (0, 0))],
        out_specs=pl.BlockSpec((tm, GLA_VAL), lambda i: (i, 0)),
        compiler_params=_cp("parallel"),
        name="gla_gate",
    )(o_f, o_b, px, gain)


def _pad_cols(w, mult):
    pad = (-w.shape[1]) % mult
    return jnp.pad(w, ((0, 0), (0, pad))) if pad else w


def _row(v, width=None):
    v = v.reshape(1, -1).astype(F32)
    return _pad_cols(v, width) if width else v


MM_TN = 512

AB_CKV_BLK = MLA_Q_RANK // MLA_KV_RANK
AB_CONV_COL = MLA_Q_RANK + MLA_KV_RANK
AB_KR_BLK = (AB_CONV_COL + 3 * CONV_WIDTH) // LANE
GLA_K_BLK = 0
GLA_V_BLK = GLA_KEY // GLA_DV
GLA_Q_BLK = (GLA_KEY + GLA_VAL) // GLA_DK
GLA_G_BLK = (2 * GLA_KEY + GLA_VAL) // GLA_VAL
GLA_A_BLK = (2 * GLA_KEY + 2 * GLA_VAL) // LANE


def _ab_w_in(w):
    ckv = w[:, :MLA_KV_RANK]
    kr = w[:, MLA_KV_RANK:MLA_KV_RANK + MLA_ROPE]
    cq = w[:, MLA_KV_RANK + MLA_ROPE:MLA_KV_RANK + MLA_ROPE + MLA_Q_RANK]
    conv = w[:, MLA_KV_RANK + MLA_ROPE + MLA_Q_RANK:]
    return _pad_cols(jnp.concatenate([cq, ckv, conv, kr], axis=1), MM_TN).astype(BF16)


def _ab_w_uq(w):
    w = w.reshape(MLA_Q_RANK, MLA_HEADS, MLA_QK)
    w = jnp.pad(w, ((0, 0), (0, 0), (0, MLA_HEAD_PAD - MLA_QK)))
    return w.reshape(MLA_Q_RANK, MLA_HEADS * MLA_HEAD_PAD).astype(BF16)


def _gla_w_in(w):
    o1, o2 = GLA_KEY, GLA_KEY + GLA_VAL
    o3 = o2 + 2 * GLA_RANK
    k, v, a, q, g = w[:, :o1], w[:, o1:o2], w[:, o2:o3], w[:, o3:o3 + GLA_KEY], w[:, o3 + GLA_KEY:]
    return _pad_cols(jnp.concatenate([k, v, q, g, a], axis=1), MM_TN).astype(BF16)


def _gla_wa_pad(w_a, offset):
    w = jnp.zeros((LANE, GLA_KEY), F32).at[offset:offset + GLA_RANK].set(w_a.astype(F32))
    hi = w.astype(BF16)
    r1 = w - hi.astype(F32)
    mid = r1.astype(BF16)
    lo = (r1 - mid.astype(F32)).astype(BF16)
    return jnp.stack([hi, mid, lo])


def _rope_tables(pos_rows, pos_cols):
    half = MLA_ROPE // 4
    inv_freq = ROPE_BASE ** (-jnp.arange(half, dtype=F32) / half)
    ar = pos_rows.astype(F32)[:, None] * inv_freq[None, :]
    ac = pos_cols.astype(F32)[:, None] * inv_freq[None, :]
    zeros = jnp.zeros((ar.shape[0], LANE - MLA_ROPE), F32)
    cos = jnp.concatenate([jnp.cos(ar), jnp.cos(ar), jnp.cos(ac), jnp.cos(ac), zeros], axis=1)
    sin = jnp.concatenate([-jnp.sin(ar), jnp.sin(ar), -jnp.sin(ac), jnp.sin(ac), zeros], axis=1)
    return cos, sin


def _mixer_mla_conv(hx, hc, p, rope_x, rope_c, need_ctx):
    w_in, w_uq, w_ukv, w_out = p["w_in"], p["w_uq"], p["w_ukv"], p["w_out"]

    def project(h, rope):
        px = _mm([(h[0], w_in)], out_dtype=BF16, modulation=h[1], name="ab_in_proj")
        k, v = _kv_up(px, AB_CKV_BLK, AB_KR_BLK, p["kv_a_gain"], w_ukv, p["k_gain_n"], p["k_gain_r"], *rope)
        return px, k, v

    def branch_out(px, k, v, rope):
        q = _q_up(px, p["q_a_gain"], w_uq, p["q_gain_n"], p["q_gain_r"], *rope, p["k_gain_n"], p["k_gain_r"])
        o = _attention(q, k, v)
        y = _conv(px, AB_CONV_COL, p["conv_w"], p["conv_b"])
        return o, y

    px_c, k_c, v_c = project(hc, rope_c)
    px_x, k_x, v_x = project(hx, rope_x)
    k_all = jnp.concatenate([k_c, k_x], axis=0)
    v_all = jnp.concatenate([v_c, v_x], axis=0)
    out_x = branch_out(px_x, k_all, v_all, rope_x)
    out_c = branch_out(px_c, k_c, v_c, rope_c) if need_ctx else None
    half = MLA_HEADS * MLA_V
    return out_x, out_c, (w_out[:half], w_out[half:])


def _mixer_gla(hx, hc, p, need_ctx):
    cols = (GLA_K_BLK, GLA_V_BLK, GLA_Q_BLK, GLA_A_BLK)
    zero_state = jnp.zeros((GLA_HEADS, GLA_DV, GLA_DK), F32)

    def run(h, s_f, s_b):
        px = _mm([(h[0], p["w_in"])], out_dtype=BF16, modulation=h[1], name="gla_in_proj")
        sides = []
        for reverse, wa, ba, s0 in ((False, p["wa_f"], p["ba_f"], s_f), (True, p["wa_b"], p["ba_b"], s_b)):
            oi, qd, kd, dl = _gla_intra(px, cols, wa, ba, reverse)
            sides.append((qd, kd, dl, oi, s0))
        o_f, sf_fin, o_b, sb_fin = _gla_scan(px, GLA_V_BLK, *sides)
        return px, (o_f, o_b), (sf_fin, sb_fin)

    px_c, o_c, (s_f, s_b) = run(hc, zero_state, zero_state)
    px_x, o_x, _ = run(hx, s_f, s_b)
    out_x = _gla_gate(o_x[0], o_x[1], px_x, GLA_G_BLK, p["o_gain"])
    out_c = _gla_gate(o_c[0], o_c[1], px_c, GLA_G_BLK, p["o_gain"]) if need_ctx else None
    return out_x, out_c


def _moe_dense(xres, gain, shift, scale, gate, router, wg, wu, wd3):
    h, gates = _modulate(xres, gain, shift, scale, router=router, route="dense")
    hid = _moe_up(h, wg, wu, gates)
    wd = wd3.reshape(N_EXPERTS * D_FF, wd3.shape[-1])
    return _mm([(hid, wd)], out_dtype=F32, residual=(xres, gate), name="moe_down")


def _moe(xres, *args):
    routed = xres.shape[0] >= 4 * MOE_ROW_TILE and xres.shape[0] % MOE_TOK_TILE == 0
    return (_moe_routed if routed else _moe_dense)(xres, *args)


def kernel(x, c, ctx, c_ctx, ada_w, ada_b, norm_mix_gain, norm_ffn_gain, ab_w_in, ab_q_a_gain, ab_w_uq, ab_q_gain, ab_kv_a_gain, ab_w_ukv, ab_k_gain, ab_conv_w, ab_conv_b, ab_w_out, gla_w_in, gla_w_af, gla_b_af, gla_w_ab, gla_b_ab, gla_o_gain, gla_w_out, router_w, router_b, moe_w_gate, moe_w_up, moe_w_down):
    b, s, d = x.shape
    assert b == 1 and s % GRID_W == 0
    depth = ada_w.shape[0]
    xs, cx = x[0], ctx[0]
    n_ctx = cx.shape[0]

    mod = _adaln(c.reshape(d, 1), c_ctx.reshape(d, 1), ada_w, ada_b).reshape(depth, 2, 6, 1, d)
    tok = jnp.arange(s, dtype=jnp.int32)
    rope_x = _rope_tables(tok // GRID_W, tok % GRID_W)
    rope_c = (jnp.concatenate([jnp.ones((n_ctx, MLA_ROPE), F32), jnp.zeros((n_ctx, LANE - MLA_ROPE), F32)], axis=1),
              jnp.zeros((n_ctx, LANE), F32))
    wr = _pad_cols(router_w.astype(F32), LANE)
    wr_hi = wr.astype(BF16)
    wr_lo = (wr - wr_hi.astype(F32)).astype(BF16)
    router = (jnp.stack([wr_hi, wr_lo]), router_b.reshape(N_EXPERTS, 1).astype(F32))

    for i in range(depth):
        last = i == depth - 1
        j = i // 2
        mx, mc = mod[i, 0], mod[i, 1]
        gain_mix = _row(norm_mix_gain[i])
        hx = (xs, (gain_mix, mx[0], mx[1]))
        hc = (cx, (gain_mix, mc[0], mc[1]))
        if i % 2 == 0:
            p = dict(
                w_in=_ab_w_in(ab_w_in[j]), w_uq=_ab_w_uq(ab_w_uq[j]), w_ukv=ab_w_ukv[j].astype(BF16),
                w_out=ab_w_out[j].astype(BF16),
                q_a_gain=_row(ab_q_a_gain[j]), kv_a_gain=_row(ab_kv_a_gain[j]),
                q_gain_n=_row(ab_q_gain[j][:MLA_NOPE]), q_gain_r=_row(ab_q_gain[j][MLA_NOPE:], LANE),
                k_gain_n=_row(ab_k_gain[j][:MLA_NOPE]), k_gain_r=_row(ab_k_gain[j][MLA_NOPE:], LANE),
                conv_w=ab_conv_w[j].astype(F32), conv_b=_row(ab_conv_b[j]))
            (o_x, y_x), out_c, (w_o, w_y) = _mixer_mla_conv(hx, hc, p, rope_x, rope_c, not last)
            xs = _mm([(o_x, w_o), (y_x, w_y)], out_dtype=F32, residual=(xs, mx[2]), name="ab_out_proj")
            if not last:
                cx = _mm([(out_c[0], w_o), (out_c[1], w_y)], out_dtype=F32, residual=(cx, mc[2]), name="ab_out_proj")
        else:
            p = dict(
                w_in=_gla_w_in(gla_w_in[j]),
                wa_f=_gla_wa_pad(gla_w_af[j], 0), wa_b=_gla_wa_pad(gla_w_ab[j], GLA_RANK),
                ba_f=_row(gla_b_af[j]), ba_b=_row(gla_b_ab[j]), o_gain=_row(gla_o_gain[j]))
            w_out = gla_w_out[j].astype(BF16)
            y_x, y_c = _mixer_gla(hx, hc, p, not last)
            xs = _mm([(y_x, w_out)], out_dtype=F32, residual=(xs, mx[2]), name="gla_out_proj")
            if not last:
                cx = _mm([(y_c, w_out)], out_dtype=F32, residual=(cx, mc[2]), name="gla_out_proj")
        gain_ffn = _row(norm_ffn_gain[i])
        wg, wu = moe_w_gate[i].astype(BF16), moe_w_up[i].astype(BF16)
        wd = moe_w_down[i].astype(BF16)
        xs = _moe(xs, gain_ffn, mx[3], mx[4], mx[5], router, wg, wu, wd)
        if not last:
            cx = _moe(cx, gain_ffn, mc[3], mc[4], mc[5], router, wg, wu, wd)
    return xs[None]
```

```python
import functools

import numpy as np
import jax
import jax.numpy as jnp
from jax import lax
from jax.experimental import pallas as pl
from jax.experimental.pallas import tpu as pltpu

F32 = jnp.float32
BF16 = jnp.bfloat16
HIGHEST = lax.Precision.HIGHEST

LANE = 128
SUBLANE = 8
VMEM_LIMIT_BYTES = 56 * 1024 * 1024

GRID_W = 64
EPS = 1e-6
MLA_HEADS = 16
MLA_Q_RANK = 1024
MLA_KV_RANK = 512
MLA_NOPE = 128
MLA_ROPE = 64
MLA_V = 128
MLA_QK = MLA_NOPE + MLA_ROPE
MLA_HEAD_PAD = 2 * LANE
ROPE_BASE = 10000.0
LOG2E = 1.4426950408889634
CONV_WIDTH = 2048
GLA_HEADS = 8
GLA_DK = 256
GLA_DV = 512
GLA_KEY = GLA_HEADS * GLA_DK
GLA_VAL = GLA_HEADS * GLA_DV
GLA_RANK = 16
GLA_TEMP = 16.0
GLA_CHUNK = 64
N_EXPERTS = 16
N_GROUPS = 4
GROUP_SIZE = N_EXPERTS // N_GROUPS
D_FF = 256


def _cp(*sem):
    return pltpu.CompilerParams(dimension_semantics=sem, vmem_limit_bytes=VMEM_LIMIT_BYTES)


def _dot(a, b):
    return jnp.dot(a, b, preferred_element_type=F32)


def _dot_nt(a, b):
    return lax.dot_general(a, b, (((1,), (1,)), ((), ())), preferred_element_type=F32)


def _dot_tn(a, b):
    return lax.dot_general(a, b, (((0,), (0,)), ((), ())), preferred_element_type=F32)


def _dot_f32(a, b):
    return jnp.dot(a, b, precision=HIGHEST, preferred_element_type=F32)


def _rms(v, n=None):
    ss = jnp.sum(v * v, axis=-1, keepdims=True)
    return v * lax.rsqrt(ss * (1.0 / (n if n is not None else v.shape[-1])) + EPS)


def _adaln_kernel(c_ref, cc_ref, w_ref, b_ref, o_ref, acc_ref):
    k = pl.program_id(2)

    @pl.when(k == 0)
    def _():
        acc_ref[...] = jnp.zeros_like(acc_ref)

    w = w_ref[...]
    tk, tn = w.shape
    for r, ref in enumerate((c_ref, cc_ref)):
        cv = ref[...]
        cs = cv * jax.nn.sigmoid(cv)
        acc_ref[r] += jnp.sum((w * cs).reshape(tk // SUBLANE, SUBLANE, tn), axis=0)

    @pl.when(k == pl.num_programs(2) - 1)
    def _():
        o_ref[...] = jnp.sum(acc_ref[...], axis=1) + b_ref[...]


def _adaln(c_col, cc_col, ada_w, ada_b):
    depth, d, n = ada_w.shape
    tk, tn = 1024, 2048
    return pl.pallas_call(
        _adaln_kernel,
        out_shape=jax.ShapeDtypeStruct((depth, 2, n), F32),
        grid=(depth, n // tn, d // tk),
        in_specs=[
            pl.BlockSpec((tk, 1), lambda l, j, k: (k, 0)),
            pl.BlockSpec((tk, 1), lambda l, j, k: (k, 0)),
            pl.BlockSpec((None, tk, tn), lambda l, j, k: (l, k, j)),
            pl.BlockSpec((None, 1, tn), lambda l, j, k: (l, 0, j)),
        ],
        out_specs=pl.BlockSpec((None, 2, tn), lambda l, j, k: (l, 0, j)),
        scratch_shapes=[pltpu.VMEM((2, SUBLANE, tn), F32)],
        compiler_params=_cp("parallel", "parallel", "arbitrary"),
        name="adaln_mod",
    )(c_col, cc_col, ada_w, ada_b.reshape(depth, 1, n))


def _route_topk(logits, bias_col):
    tm = logits.shape[0]
    s = jax.nn.sigmoid(logits.T[:N_EXPERTS])
    sel = s + bias_col
    row = [sel[e:e + 1] for e in range(N_EXPERTS)]
    score = []
    for g in range(N_GROUPS):
        a, b, c2, d2 = row[GROUP_SIZE * g:GROUP_SIZE * (g + 1)]
        hi1, lo1 = jnp.maximum(a, b), jnp.minimum(a, b)
        hi2, lo2 = jnp.maximum(c2, d2), jnp.minimum(c2, d2)
        score.append(jnp.maximum(hi1, hi2) + jnp.maximum(jnp.minimum(hi1, hi2), jnp.maximum(lo1, lo2)))
    best, gidx = score[0], jnp.zeros((1, tm), jnp.int32)
    for g in range(1, N_GROUPS):
        upd = score[g] > best
        gidx = jnp.where(upd, g, gidx)
        best = jnp.where(upd, score[g], best)
    neg = jnp.full((1, tm), -jnp.inf, F32)
    cand = [jnp.where(gidx == (e // GROUP_SIZE), row[e], neg) for e in range(N_EXPERTS)]

    def argmax_first(vals):
        bv, bi = vals[0], jnp.zeros((1, tm), jnp.int32)
        for e in range(1, N_EXPERTS):
            upd = vals[e] > bv
            bi = jnp.where(upd, e, bi)
            bv = jnp.where(upd, vals[e], bv)
        return bi

    i1 = argmax_first(cand)
    i2 = argmax_first([jnp.where(i1 == e, neg, cand[e]) for e in range(N_EXPERTS)])
    eidx = lax.broadcasted_iota(jnp.int32, (N_EXPERTS, tm), 0)
    w1 = jnp.sum(jnp.where(eidx == i1, s, 0.0), axis=0, keepdims=True)
    w2 = jnp.sum(jnp.where(eidx == i2, s, 0.0), axis=0, keepdims=True)
    den = w1 + w2
    return i1, i2, w1 / den, w2 / den


def _dense_gates(i1, i2, w1, w2):
    efull = lax.broadcasted_iota(jnp.int32, (LANE, i1.shape[1]), 0)
    return (jnp.where(efull == i1, w1, 0.0) + jnp.where(efull == i2, w2, 0.0)).T


INFO_E1, INFO_E2, INFO_W1, INFO_W2, INFO_R1, INFO_R2 = range(6)


def _routing_record(i1, i2, w1, w2, carry_ref):
    tm = i1.shape[1]
    efull = lax.broadcasted_iota(jnp.int32, (LANE, tm), 0)
    rec_t = (jnp.where(efull == INFO_E1, i1.astype(F32), 0.0) + jnp.where(efull == INFO_E2, i2.astype(F32), 0.0)
             + jnp.where(efull == INFO_W1, w1, 0.0) + jnp.where(efull == INFO_W2, w2, 0.0))
    rec = rec_t.T
    onehot = jnp.where((efull == i1) | (efull == i2), 1.0, 0.0).T
    r = lax.broadcasted_iota(jnp.int32, (tm, tm), 0)
    c = lax.broadcasted_iota(jnp.int32, (tm, tm), 1)
    incl = _dot(jnp.where(r >= c, 1.0, 0.0).astype(BF16), onehot.astype(BF16)) + carry_ref[...]
    carry_ref[...] = incl[tm - 1:tm, :]
    rank = incl - onehot
    lane = lax.broadcasted_iota(jnp.int32, (tm, LANE), 1)
    lane_f = lane.astype(F32)
    r1 = jnp.sum(jnp.where(lane_f == rec[:, INFO_E1:INFO_E1 + 1], rank, 0.0), axis=-1, keepdims=True)
    r2 = jnp.sum(jnp.where(lane_f == rec[:, INFO_E2:INFO_E2 + 1], rank, 0.0), axis=-1, keepdims=True)
    return jnp.where(lane == INFO_R1, r1, jnp.where(lane == INFO_R2, r2, rec))


def _pack_bf16_pairs(v):
    n = v.shape[1] // 2
    hi = lax.bitcast_convert_type(v[:, :n].astype(BF16).astype(F32), jnp.uint32)
    lo = lax.bitcast_convert_type(v[:, n:].astype(BF16).astype(F32), jnp.uint32)
    return hi | (lo >> 16)


def _unpack_bf16_pairs(w):
    first = lax.bitcast_convert_type(w & jnp.uint32(0xFFFF0000), F32)
    second = lax.bitcast_convert_type(w << 16, F32)
    return first, second


def _modulate_kernel(x_ref, g_ref, sh_ref, sc_ref, *rest, route):
    x = x_ref[...]
    h = _rms(x) * g_ref[...] * (1.0 + sc_ref[...]) + sh_ref[...]
    if route is None:
        (h_ref,) = rest
        h_ref[...] = h.astype(h_ref.dtype)
        return
    wr_ref, br_ref = rest[:2]
    h_hi = h.astype(BF16)
    h_lo = (h - h_hi.astype(F32)).astype(BF16)
    logits = _dot(h_hi, wr_ref[0]) + (_dot(h_lo, wr_ref[0]) + _dot(h_hi, wr_ref[1]))
    topk = _route_topk(logits, br_ref[...])
    if route == "dense":
        h_ref, gates_ref = rest[2:]
        h_ref[...] = h.astype(h_ref.dtype)
        gates_ref[...] = _dense_gates(*topk)
    else:
        h_ref, rec_ref, cnt_ref, carry_ref = rest[2:]

        @pl.when(pl.program_id(0) == 0)
        def _():
            carry_ref[...] = jnp.zeros_like(carry_ref)

        h_ref[...] = _pack_bf16_pairs(h)
        rec_ref[...] = _routing_record(*topk, carry_ref)
        cnt_ref[...] = carry_ref[...]


def _modulate(x, gain, shift, scale, router=None, route=None):
    m, d = x.shape
    tm = min(256, m)
    vec = pl.BlockSpec((1, d), lambda i: (0, 0))
    in_specs = [pl.BlockSpec((tm, d), lambda i: (i, 0)), vec, vec, vec]
    args = [x, gain, shift, scale]
    out_shape = [jax.ShapeDtypeStruct((m, d), BF16)]
    out_specs = [pl.BlockSpec((tm, d), lambda i: (i, 0))]
    scratch = []
    if route is not None:
        wr, br = router
        in_specs += [pl.BlockSpec((2, d, LANE), lambda i: (0, 0, 0)), pl.BlockSpec((N_EXPERTS, 1), lambda i: (0, 0))]
        args += [wr, br]
        out_shape.append(jax.ShapeDtypeStruct((m, LANE), F32))
        out_specs.append(pl.BlockSpec((tm, LANE), lambda i: (i, 0)))
    if route == "routed":
        out_shape[0] = jax.ShapeDtypeStruct((m, d // 2), jnp.uint32)
        out_specs[0] = pl.BlockSpec((tm, d // 2), lambda i: (i, 0))
        out_shape.append(jax.ShapeDtypeStruct((1, LANE), F32))
        out_specs.append(pl.BlockSpec((1, LANE), lambda i: (0, 0)))
        scratch = [pltpu.VMEM((1, LANE), F32)]
    out = pl.pallas_call(
        functools.partial(_modulate_kernel, route=route),
        out_shape=out_shape, grid=(m // tm,), in_specs=in_specs, out_specs=out_specs, scratch_shapes=scratch,
        compiler_params=_cp("arbitrary" if route == "routed" else "parallel"),
        name="modulate" if route is None else "modulate_route_" + route,
    )(*args)
    return out[0] if route is None else out


def _mm_kernel(*refs, n_pairs, residual, modulated):
    a_refs, w_refs = refs[:n_pairs], refs[n_pairs:2 * n_pairs]
    rest = refs[2 * n_pairs:]
    if modulated:
        (g_ref, sh_ref, sc_ref), rest, h_ref = rest[:3], rest[3:-1], rest[-1]

        @pl.when(pl.program_id(1) == 0)
        def _():
            h_ref[...] = (_rms(a_refs[0][...]) * g_ref[...] * (1.0 + sc_ref[...]) + sh_ref[...]).astype(BF16)

        acc = _dot(h_ref[...], w_refs[0][...])
    else:
        acc = _dot(a_refs[0][...], w_refs[0][...])
    o_ref = rest[-1]
    for a_ref, w_ref in zip(a_refs[1:], w_refs[1:]):
        acc += _dot(a_ref[...], w_ref[...])
    if residual:
        x_ref, gate_ref = rest[:2]
        acc = x_ref[...] + gate_ref[...] * acc
    o_ref[...] = acc.astype(o_ref.dtype)


def _mm(pairs, *, out_dtype, residual=None, modulation=None, tm=1024, tn=512, name="mm"):
    m = pairs[0][0].shape[0]
    n = pairs[0][1].shape[1]
    if modulation is not None:
        tm = tm // 2
        tn = next(c for c in (1280, 1024, 512, 256, 128) if n % c == 0)
    tm, tn = min(tm, m), min(tn, n)
    assert m % tm == 0 and n % tn == 0, (m, n, tm, tn)
    in_specs = [pl.BlockSpec((tm, a.shape[1]), lambda i, j: (i, 0)) for a, _ in pairs]
    in_specs += [pl.BlockSpec((w.shape[0], tn), lambda i, j: (0, j)) for _, w in pairs]
    args = [a for a, _ in pairs] + [w for _, w in pairs]
    scratch = []
    if modulation is not None:
        k0 = pairs[0][0].shape[1]
        in_specs += [pl.BlockSpec((1, k0), lambda i, j: (0, 0))] * 3
        args += list(modulation)
        scratch = [pltpu.VMEM((tm, k0), BF16)]
    if residual is not None:
        in_specs += [pl.BlockSpec((tm, tn), lambda i, j: (i, j)), pl.BlockSpec((1, tn), lambda i, j: (0, j))]
        args += list(residual)
    return pl.pallas_call(
        functools.partial(_mm_kernel, n_pairs=len(pairs), residual=residual is not None,
                          modulated=modulation is not None),
        out_shape=jax.ShapeDtypeStruct((m, n), out_dtype),
        grid=(m // tm, n // tn), in_specs=in_specs,
        out_specs=pl.BlockSpec((tm, tn), lambda i, j: (i, j)),
        scratch_shapes=scratch,
        compiler_params=_cp("parallel", "arbitrary"),
        name=name,
    )(*args)


def _rope_rotate(v, cos, sin_signed):
    lane = lax.broadcasted_iota(jnp.int32, (1, LANE), 1)
    first_half = (lane % 32) < 16
    partner = jnp.where(first_half, pltpu.roll(v, LANE - 16, axis=1), pltpu.roll(v, 16, axis=1))
    return v * cos + partner * sin_signed


def _norm_bound_sq(gn, gr):
    return (MLA_NOPE * jnp.max(gn * gn, axis=-1, keepdims=True)
            + MLA_ROPE * jnp.max(gr * gr, axis=-1, keepdims=True))


def _q_up_kernel(cq_ref, ga_ref, w_ref, gn_ref, gr_ref, cos_ref, sin_ref, kgn_ref, kgr_ref, o_ref, *, scale):
    cqn = (_rms(cq_ref[...].astype(F32)) * ga_ref[...]).astype(BF16)
    q = _dot(cqn, w_ref[...])
    cos, sin = cos_ref[...], sin_ref[...]
    gn, gr = gn_ref[...] * scale, gr_ref[...] * scale
    lane = lax.broadcasted_iota(jnp.int32, (1, LANE), 1)
    offset = SCORE_CENTER - jnp.sqrt(_norm_bound_sq(gn, gr) * _norm_bound_sq(kgn_ref[...], kgr_ref[...]))
    for h in range(MLA_HEADS):
        o = h * MLA_HEAD_PAD
        qr = _rope_rotate(_rms(q[:, o + LANE:o + 2 * LANE], MLA_ROPE) * gr, cos, sin)
        o_ref[:, o:o + LANE] = (_rms(q[:, o:o + LANE]) * gn).astype(BF16)
        o_ref[:, o + LANE:o + 2 * LANE] = jnp.where(lane == OFFSET_LANE, offset, qr).astype(BF16)


def _q_up(px, ga, w, gn, gr, cos, sin, kgn, kgr):
    m = px.shape[0]
    tm = min(512, m)
    n = MLA_HEADS * MLA_HEAD_PAD
    vec = lambda width: pl.BlockSpec((1, width), lambda i: (0, 0))
    return pl.pallas_call(
        functools.partial(_q_up_kernel, scale=MLA_QK ** -0.5 * LOG2E),
        out_shape=jax.ShapeDtypeStruct((m, n), BF16),
        grid=(m // tm,),
        in_specs=[pl.BlockSpec((tm, MLA_Q_RANK), lambda i: (i, 0)), vec(MLA_Q_RANK),
                  pl.BlockSpec((MLA_Q_RANK, n), lambda i: (0, 0)), vec(LANE), vec(LANE),
                  pl.BlockSpec((tm, LANE), lambda i: (i, 0)), pl.BlockSpec((tm, LANE), lambda i: (i, 0)),
                  vec(LANE), vec(LANE)],
        out_specs=pl.BlockSpec((tm, n), lambda i: (i, 0)),
        compiler_params=_cp("parallel"),
        name="mla_q_up",
    )(px, ga, w, gn, gr, cos, sin, kgn, kgr)


def _kv_up_kernel(ckv_ref, kr_ref, ga_ref, w_ref, gn_ref, gr_ref, cos_ref, sin_ref, k_ref, v_ref):
    ckvn = (_rms(ckv_ref[...].astype(F32)) * ga_ref[...]).astype(BF16)
    kv = _dot(ckvn, w_ref[...])
    kr = _rms(kr_ref[...].astype(F32), MLA_ROPE) * gr_ref[...]
    kr = _rope_rotate(kr, cos_ref[...], sin_ref[...])
    lane = lax.broadcasted_iota(jnp.int32, (1, LANE), 1)
    krb = jnp.where(lane == OFFSET_LANE, 1.0, kr).astype(BF16)
    gn = gn_ref[...]
    for h in range(MLA_HEADS):
        o = h * MLA_HEAD_PAD
        k_ref[:, o:o + LANE] = (_rms(kv[:, o:o + LANE]) * gn).astype(BF16)
        k_ref[:, o + LANE:o + 2 * LANE] = krb
        v_ref[:, h * MLA_V:(h + 1) * MLA_V] = kv[:, o + LANE:o + 2 * LANE].astype(BF16)


def _kv_up(px, ckv_blk, kr_blk, ga, w, gn, gr, cos, sin):
    m = px.shape[0]
    tm = min(512, m)
    vec = lambda width: pl.BlockSpec((1, width), lambda i: (0, 0))
    return pl.pallas_call(
        _kv_up_kernel,
        out_shape=[jax.ShapeDtypeStruct((m, MLA_HEADS * MLA_HEAD_PAD), BF16),
                   jax.ShapeDtypeStruct((m, MLA_HEADS * MLA_V), BF16)],
        grid=(m // tm,),
        in_specs=[pl.BlockSpec((tm, MLA_KV_RANK), lambda i: (i, ckv_blk)),
                  pl.BlockSpec((tm, LANE), lambda i: (i, kr_blk)), vec(MLA_KV_RANK),
                  pl.BlockSpec((MLA_KV_RANK, MLA_HEADS * MLA_HEAD_PAD), lambda i: (0, 0)), vec(LANE), vec(LANE),
                  pl.BlockSpec((tm, LANE), lambda i: (i, 0)), pl.BlockSpec((tm, LANE), lambda i: (i, 0))],
        out_specs=[pl.BlockSpec((tm, MLA_HEADS * MLA_HEAD_PAD), lambda i: (i, 0)),
                   pl.BlockSpec((tm, MLA_HEADS * MLA_V), lambda i: (i, 0))],
        compiler_params=_cp("parallel"),
        name="mla_kv_up",
    )(px, px, ga, w, gn, gr, cos, sin)


def _attn_kernel(q_ref, k_ref, v_ref, o_ref, sa_ref, sb_ref, m_ref, l_ref, acc_ref, *, ck, n_chunks):
    def sub(i):
        return pl.ds(pl.multiple_of(i * ATTN_SUB, ATTN_SUB), ATTN_SUB)

    l_ref[...] = jnp.zeros_like(l_ref)
    acc_ref[...] = jnp.zeros_like(acc_ref)

    n_sub = n_chunks * ck // ATTN_SUB
    unroll = next(u for u in (13, 8, 5, 4, 3, 2, 1) if n_sub % u == 0)

    def stream(t, carry):
        lsum = l_ref[...]
        acc = acc_ref[...]
        for u in range(unroll):
            i = t * unroll + u
            p = jnp.exp2(_dot_nt(q_ref[...], k_ref[sub(i), :]))
            for part in range(ATTN_SUB // LANE):
                lsum += p[:, part * LANE:(part + 1) * LANE]
            acc += _dot(p.astype(BF16), v_ref[sub(i), :])
        l_ref[...] = lsum
        acc_ref[...] = acc
        return carry

    lax.fori_loop(0, n_sub // unroll, stream, 0)
    row_sum = jnp.sum(l_ref[...], axis=-1, keepdims=True)
    in_range = jnp.min(row_sum) >= SCORE_SUM_FLOOR

    @pl.when(in_range)
    def _():
        o_ref[...] = (acc_ref[...] / row_sum).astype(o_ref.dtype)

    @pl.when(jnp.logical_not(in_range))
    def _():
        _attn_online(q_ref, k_ref, v_ref, o_ref, sa_ref, sb_ref, m_ref, l_ref, acc_ref, ck=ck, n_chunks=n_chunks)


def _attn_online(q_ref, k_ref, v_ref, o_ref, sa_ref, sb_ref, m_ref, l_ref, acc_ref, *, ck, n_chunks):
    m_ref[...] = jnp.full_like(m_ref, -jnp.inf)
    l_ref[...] = jnp.zeros_like(l_ref)
    acc_ref[...] = jnp.zeros_like(acc_ref)

    def rows(c):
        return pl.ds(pl.multiple_of(c * ck, ck), ck)

    def scores(c, s_ref):
        s_ref[...] = _dot_nt(q_ref[...], k_ref[rows(c), :])

    def update(c, s_ref):
        m_prev = m_ref[...]
        m_new = jnp.maximum(m_prev, jnp.max(s_ref[...], axis=-1, keepdims=True))
        alpha = jnp.exp2(m_prev - m_new)
        m_ref[...] = m_new
        lsum = alpha * l_ref[...]
        acc = alpha * acc_ref[...]
        for j in range(ck // ATTN_SUB):
            p = jnp.exp2(s_ref[:, j * ATTN_SUB:(j + 1) * ATTN_SUB] - m_new)
            for t in range(ATTN_SUB // LANE):
                lsum += p[:, t * LANE:(t + 1) * LANE]
            acc += _dot(p.astype(BF16), v_ref[pl.ds(pl.multiple_of(c * ck + j * ATTN_SUB, ATTN_SUB), ATTN_SUB), :])
        l_ref[...] = lsum
        acc_ref[...] = acc

    bufs = (sa_ref, sb_ref)
    scores(0, sa_ref)
    group = 4 if (n_chunks - 1) % 4 == 0 else 2
    n_groups = (n_chunks - 1) // group

    def body(t, carry):
        c = group * t
        for u in range(group):
            scores(c + u + 1, bufs[(u + 1) % 2])
            update(c + u, bufs[u % 2])
        return carry

    if n_groups:
        lax.fori_loop(0, n_groups, body, 0)
    done = group * n_groups
    for c in range(done, n_chunks):
        if c + 1 < n_chunks:
            scores(c + 1, bufs[(c + 1) % 2])
        update(c, bufs[c % 2])
    o_ref[...] = (acc_ref[...] / jnp.sum(l_ref[...], axis=-1, keepdims=True)).astype(o_ref.dtype)


ATTN_SUB = 2 * LANE
OFFSET_LANE = MLA_ROPE
SCORE_CENTER = 64.0
SCORE_SUM_FLOOR = 2.0 ** -60


def _attn_tiles(tq_total, tk_total):
    tq = min(1024, tq_total)
    ck = tk_total
    for cand in (1280, 1024, 512, 256):
        if tk_total % cand == 0:
            ck = cand
            break
    return tq, ck


def _attention(q, k, v):
    tq_total, tk_total = q.shape[0], k.shape[0]
    tq, ck = _attn_tiles(tq_total, tk_total)
    return pl.pallas_call(
        functools.partial(_attn_kernel, ck=ck, n_chunks=tk_total // ck),
        out_shape=jax.ShapeDtypeStruct((tq_total, MLA_HEADS * MLA_V), BF16),
        grid=(MLA_HEADS, tq_total // tq),
        in_specs=[pl.BlockSpec((tq, MLA_HEAD_PAD), lambda h, i: (i, h)),
                  pl.BlockSpec((tk_total, MLA_HEAD_PAD), lambda h, i: (0, h), pipeline_mode=pl.Buffered(1)),
                  pl.BlockSpec((tk_total, MLA_V), lambda h, i: (0, h), pipeline_mode=pl.Buffered(1))],
        out_specs=pl.BlockSpec((tq, MLA_V), lambda h, i: (i, h)),
        scratch_shapes=[pltpu.VMEM((tq, ck), F32), pltpu.VMEM((tq, ck), F32),
                        pltpu.VMEM((tq, 1), F32), pltpu.VMEM((tq, LANE), F32), pltpu.VMEM((tq, MLA_V), F32)],
        compiler_params=_cp("parallel", "arbitrary"),
        name="mla_attention",
    )(q, k, v)


CONV_TILE = 512
HALO = 16


def _conv_kernel(gb_ref, gc_ref, u_ref, gcp_ref, up_ref, gcn_ref, un_ref, w_ref, b_ref, o_ref):
    i = pl.program_id(0)
    z = gc_ref[...].astype(F32) * u_ref[...].astype(F32)
    tm = z.shape[0]
    z_before = gcp_ref[HALO - 1:HALO, :].astype(F32) * up_ref[HALO - 1:HALO, :].astype(F32)
    z_after = gcn_ref[0:1, :].astype(F32) * un_ref[0:1, :].astype(F32)
    z_before = jnp.where(i > 0, z_before, 0.0)
    z_after = jnp.where(i < pl.num_programs(0) - 1, z_after, 0.0)
    rowid = lax.broadcasted_iota(jnp.int32, z.shape, 0)
    z_prev = jnp.where(rowid == 0, z_before, pltpu.roll(z, 1, axis=0))
    z_next = jnp.where(rowid == tm - 1, z_after, pltpu.roll(z, tm - 1, axis=0))
    w = w_ref[...]
    y = z_prev * w[0:1] + z * w[1:2] + z_next * w[2:3] + b_ref[...]
    o_ref[...] = (gb_ref[...].astype(F32) * y).astype(o_ref.dtype)


def _conv(px, col0, conv_w, conv_b):
    m = px.shape[0]
    tm = min(512, m)
    cb = CONV_WIDTH // CONV_TILE
    b0 = col0 // CONV_TILE
    nhalo = m // HALO
    main = lambda off: pl.BlockSpec((tm, CONV_TILE), lambda i, j: (i, b0 + off * cb + j))
    prev = lambda off: pl.BlockSpec(
        (HALO, CONV_TILE), lambda i, j: (jnp.maximum(i * (tm // HALO) - 1, 0), b0 + off * cb + j))
    nxt = lambda off: pl.BlockSpec(
        (HALO, CONV_TILE), lambda i, j: (jnp.minimum((i + 1) * (tm // HALO), nhalo - 1), b0 + off * cb + j))
    return pl.pallas_call(
        _conv_kernel,
        out_shape=jax.ShapeDtypeStruct((m, CONV_WIDTH), BF16),
        grid=(m // tm, cb),
        in_specs=[main(0), main(1), main(2), prev(1), prev(2), nxt(1), nxt(2),
                  pl.BlockSpec((3, CONV_TILE), lambda i, j: (0, j)),
                  pl.BlockSpec((1, CONV_TILE), lambda i, j: (0, j))],
        out_specs=pl.BlockSpec((tm, CONV_TILE), lambda i, j: (i, j)),
        compiler_params=_cp("parallel", "parallel"),
        name="gated_conv3",
    )(px, px, px, px, px, px, px, conv_w, conv_b)


def _moe_up_kernel(h_ref, wg_ref, wu_ref, gates_ref, o_ref):
    e = pl.program_id(1)
    h = h_ref[...]
    a = _dot(h, wg_ref[...])
    u = _dot(h, wu_ref[...])
    gates = gates_ref[...]
    lane = lax.broadcasted_iota(jnp.int32, gates.shape, 1)
    g = jnp.sum(jnp.where(lane == e, gates, 0.0), axis=-1, keepdims=True)
    o_ref[...] = (a * jax.nn.sigmoid(a) * u * g).astype(o_ref.dtype)


def _moe_up(h, wg, wu, gates):
    m, d = h.shape
    tm = min(1024, m)
    return pl.pallas_call(
        _moe_up_kernel,
        out_shape=jax.ShapeDtypeStruct((m, N_EXPERTS * D_FF), BF16),
        grid=(m // tm, N_EXPERTS),
        in_specs=[pl.BlockSpec((tm, d), lambda i, e: (i, 0)),
                  pl.BlockSpec((None, d, D_FF), lambda i, e: (e, 0, 0)),
                  pl.BlockSpec((None, d, D_FF), lambda i, e: (e, 0, 0)),
                  pl.BlockSpec((tm, LANE), lambda i, e: (i, 0))],
        out_specs=pl.BlockSpec((tm, D_FF), lambda i, e: (i, e)),
        compiler_params=_cp("parallel", "arbitrary"),
        name="moe_up",
    )(h, wg, wu, gates)


MOE_TOK_TILE = 256
MOE_ROW_TILE = 256


def _row_copy(src, src_row, dst, dst_row, sem):
    return pltpu.make_async_copy(src.at[pl.ds(src_row, 1)], dst.at[pl.ds(dst_row, 1)], sem)


def _moe_dispatch_kernel(pos_ref, pad_ref, h_ref, xs_ref, sem):
    tm = h_ref.shape[0]
    n_pad = pad_ref.shape[1]

    def token_rows(start):
        def one(r, carry):
            for k in range(2):
                cp = _row_copy(h_ref, r, xs_ref, pos_ref[0, 2 * r + k], sem)
                cp.start() if start else cp.wait()
            return carry

        lax.fori_loop(0, tm, one, 0, unroll=8)

    def pad_rows(start):
        def one(j, carry):
            cp = _row_copy(h_ref, 0, xs_ref, pad_ref[0, j], sem)
            cp.start() if start else cp.wait()
            return carry

        lax.fori_loop(0, n_pad, one, 0, unroll=8)

    token_rows(True)
    pad_rows(True)
    token_rows(False)
    pad_rows(False)


def _moe_dispatch(h_packed, pos, pad_pos, n_rows):
    m, w = h_packed.shape
    tm = MOE_TOK_TILE
    n_pad = pad_pos.shape[-1]
    return pl.pallas_call(
        _moe_dispatch_kernel,
        out_shape=jax.ShapeDtypeStruct((n_rows, w), jnp.uint32),
        grid=(m // tm,),
        in_specs=[pl.BlockSpec((None, 1, 2 * tm), lambda i: (i, 0, 0), memory_space=pltpu.SMEM),
                  pl.BlockSpec((None, 1, n_pad), lambda i: (i, 0, 0), memory_space=pltpu.SMEM),
                  pl.BlockSpec((tm, w), lambda i: (i, 0))],
        out_specs=pl.BlockSpec(memory_space=pl.ANY),
        scratch_shapes=[pltpu.SemaphoreType.DMA],
        compiler_params=_cp("arbitrary"),
        name="moe_dispatch",
    )(pos, pad_pos, h_packed)


def _moe_experts_kernel(te_ref, xs_ref, wg_ref, wu_ref, wd_ref, ys_ref):
    del te_ref
    half = xs_ref.shape[1]
    x1, x2 = _unpack_bf16_pairs(xs_ref[...])
    x1, x2 = x1.astype(BF16), x2.astype(BF16)
    a = _dot(x1, wg_ref[:half, :]) + _dot(x2, wg_ref[half:, :])
    u = _dot(x1, wu_ref[:half, :]) + _dot(x2, wu_ref[half:, :])
    hid = (a * jax.nn.sigmoid(a) * u).astype(BF16)
    ys_ref[...] = _pack_bf16_pairs(_dot(hid, wd_ref[...]))


def _moe_experts(tile_expert, xs, wg, wu, wd):
    n_rows, w = xs.shape
    d = 2 * w
    t = MOE_ROW_TILE
    return pl.pallas_call(
        _moe_experts_kernel,
        out_shape=jax.ShapeDtypeStruct((n_rows, w), jnp.uint32),
        grid_spec=pltpu.PrefetchScalarGridSpec(
            num_scalar_prefetch=1, grid=(n_rows // t,),
            in_specs=[pl.BlockSpec((t, w), lambda i, te: (i, 0)),
                      pl.BlockSpec((None, d, D_FF), lambda i, te: (te[i], 0, 0)),
                      pl.BlockSpec((None, d, D_FF), lambda i, te: (te[i], 0, 0)),
                      pl.BlockSpec((None, D_FF, d), lambda i, te: (te[i], 0, 0))],
            out_specs=pl.BlockSpec((t, w), lambda i, te: (i, 0))),
        compiler_params=_cp("arbitrary"),
        name="moe_experts",
    )(tile_expert, xs, wg, wu, wd)


def _moe_combine_kernel(pos_ref, posn_ref, x_ref, gate_ref, rec_ref, ys_ref, o_ref, g1_ref, g2_ref, sem):
    _, tm, half = g1_ref.shape
    bufs = (g1_ref, g2_ref)
    i = pl.program_id(0)
    slot = i % 2

    def row_copy(p_ref, s, r, k):
        return _row_copy(ys_ref, p_ref[0, 2 * r + k], bufs[k].at[s], r, sem.at[s])

    def all_rows(p_ref, s, start):
        def one(r, carry):
            for k in range(2):
                cp = row_copy(p_ref, s, r, k)
                cp.start() if start else cp.wait()
            return carry

        lax.fori_loop(0, tm, one, 0, unroll=8)

    @pl.when(i == 0)
    def _():
        all_rows(pos_ref, slot, True)

    all_rows(pos_ref, slot, False)
    rb = 2 * SUBLANE

    def combine(prefetch, b, carry):
        rs = pl.ds(pl.multiple_of(b * rb, rb), rb)
        rec = rec_ref[rs, :]
        w1, w2 = rec[:, INFO_W1:INFO_W1 + 1], rec[:, INFO_W2:INFO_W2 + 1]
        y1 = _unpack_bf16_pairs(g1_ref[slot, rs, :])
        y2 = _unpack_bf16_pairs(g2_ref[slot, rs, :])
        for part in range(2):
            cols = slice(part * half, (part + 1) * half)
            o_ref[rs, cols] = x_ref[rs, cols] + gate_ref[:, cols] * (w1 * y1[part] + w2 * y2[part])
        if prefetch:
            for r in range(rb):
                for k in range(2):
                    row_copy(posn_ref, 1 - slot, b * rb + r, k).start()
        return carry

    has_next = i + 1 < pl.num_programs(0)

    @pl.when(has_next)
    def _():
        lax.fori_loop(0, tm // rb, functools.partial(combine, True), 0)

    @pl.when(jnp.logical_not(has_next))
    def _():
        lax.fori_loop(0, tm // rb, functools.partial(combine, False), 0)


def _moe_combine(pos, xres, gate, rec, ys):
    m, d = xres.shape
    tm = MOE_TOK_TILE
    last = m // tm - 1
    return pl.pallas_call(
        _moe_combine_kernel,
        out_shape=jax.ShapeDtypeStruct((m, d), F32),
        grid=(m // tm,),
        in_specs=[pl.BlockSpec((None, 1, 2 * tm), lambda i: (i, 0, 0), memory_space=pltpu.SMEM),
                  pl.BlockSpec((None, 1, 2 * tm), lambda i: (jnp.minimum(i + 1, last), 0, 0),
                               memory_space=pltpu.SMEM),
                  pl.BlockSpec((tm, d), lambda i: (i, 0)),
                  pl.BlockSpec((1, d), lambda i: (0, 0)),
                  pl.BlockSpec((tm, LANE), lambda i: (i, 0)),
                  pl.BlockSpec(memory_space=pl.ANY)],
        out_specs=pl.BlockSpec((tm, d), lambda i: (i, 0)),
        scratch_shapes=[pltpu.VMEM((2, tm, d // 2), jnp.uint32), pltpu.VMEM((2, tm, d // 2), jnp.uint32),
                        pltpu.SemaphoreType.DMA((2,))],
        compiler_params=_cp("arbitrary"),
        name="moe_combine",
    )(pos, pos, xres, gate, rec, ys)


def _moe_routed(xres, gain, shift, scale, gate, router, wg, wu, wd3):
    m = xres.shape[0]
    tm, t = MOE_TOK_TILE, MOE_ROW_TILE
    h_packed, rec, counts = _modulate(xres, gain, shift, scale, router=router, route="routed")
    n_rows = 2 * m + N_EXPERTS * t
    cnt = counts[0, :N_EXPERTS].astype(jnp.int32)
    seg = ((cnt + t - 1) // t) * t
    seg_end = jnp.cumsum(seg)
    seg_start = seg_end - seg
    e12 = rec[:, INFO_E1:INFO_E2 + 1].astype(jnp.int32)
    r12 = rec[:, INFO_R1:INFO_R2 + 1].astype(jnp.int32)
    pos = (seg_start[e12] + r12).reshape(m // tm, 1, 2 * tm)
    tile_start = jnp.arange(n_rows // t, dtype=jnp.int32) * t
    tile_expert = jnp.minimum(jnp.sum(tile_start[:, None] >= seg_end[None, :], axis=1), N_EXPERTS - 1).astype(jnp.int32)
    n_steps = m // tm
    assert (n_rows - 2 * m) % n_steps == 0
    pad_cnt = jnp.concatenate([seg - cnt, n_rows - seg_end[-1:]])
    pad_end = jnp.cumsum(pad_cnt)
    pad_first = jnp.concatenate([seg_start + cnt, seg_end[-1:]])
    j = jnp.arange(n_rows - 2 * m, dtype=jnp.int32)
    rng = jnp.sum(j[:, None] >= pad_end[None, :], axis=1)
    pad_pos = (pad_first[rng] + (j - (pad_end - pad_cnt)[rng])).astype(jnp.int32).reshape(n_steps, 1, -1)
    xs = _moe_dispatch(h_packed, pos, pad_pos, n_rows)
    ys = _moe_experts(tile_expert, xs, wg, wu, wd3)
    return _moe_combine(pos, xres, gate, rec, ys)


GLA_LEVELS = 6
GLA_MILD_DECAY = 50.0


def _gla_tables(reverse):
    n = GLA_CHUNK
    t = np.arange(n)[:, None]
    r = np.arange(n)[None, :]
    blocks = [r <= t, r > t]
    masks = []
    for lvl in range(GLA_LEVELS):
        bs = 1 << lvl
        blocks.append((r >= (t & ~(bs - 1))) & (r <= t))
        blocks.append((r > t) & (r <= (t | (bs - 1))))
        masks.append(((t >> (lvl + 1)) == (r >> (lvl + 1))) & (((t >> lvl) & 1) == 1) & (((r >> lvl) & 1) == 0))
    masks.append(t == r)
    masks.append(r <= t)
    e = np.stack(blocks).astype(np.float32)
    msk = np.stack(masks).astype(np.float32)
    if reverse:
        e, msk = e[:, ::-1, ::-1], msk[:, ::-1, ::-1]
    return jnp.asarray(e.reshape(-1, n)), jnp.asarray(msk)


def _gla_intra_kernel(k_ref, v_ref, q_ref, a_ref, wa_ref, ba_ref, e_ref, m_ref,
                      oi_ref, qd_ref, kd_ref, dl_ref, la_ref, tot_ref, *, nchunk, reverse):
    n = GLA_CHUNK
    scale = GLA_DK ** -0.5

    a = a_ref[...]
    z = _dot(a, wa_ref[0]) + _dot(a, wa_ref[1]) + _dot(a, wa_ref[2]) + ba_ref[...]
    la = (jnp.minimum(z, 0.0) - jnp.log1p(jnp.exp(-jnp.abs(z)))) * (1.0 / GLA_TEMP)
    la_ref[...] = la
    totals = jnp.sum(la.reshape(nchunk, n, GLA_DK), axis=1)
    tot_ref[...] = totals
    dl_ref[...] = jnp.exp(totals)
    mild = jnp.min(totals) > -GLA_MILD_DECAY
    rowid = lax.broadcasted_iota(jnp.int32, (n, GLA_DK), 0)

    def load(c):
        rows = pl.ds(pl.multiple_of(c * n, n), n)
        return rows, q_ref[rows, :].astype(F32) * scale, k_ref[rows, :].astype(F32)

    def mild_chunk(c, carry):
        rows, q, k = load(c)
        la_c = la_ref[rows, :]
        total = tot_ref[pl.ds(c, 1), :]
        cum = la_c
        shift = 1
        while shift < n:
            cum = cum + jnp.where(rowid >= shift, pltpu.roll(cum, shift, axis=0), 0.0)
            shift *= 2
        if reverse:
            cum = total - cum + la_c
        qd = (q * jnp.exp(cum)).astype(BF16)
        kinv = k * jnp.exp(-cum)
        qd_ref[rows, :] = qd
        kd_ref[rows, :] = (kinv * jnp.exp(total)).astype(BF16)
        att = m_ref[GLA_LEVELS + 1] * _dot_nt(qd, kinv.astype(BF16))
        oi_ref[rows, :] = _dot(att.astype(BF16), v_ref[rows, :]).astype(BF16)
        return carry

    def harsh_chunk(c, carry):
        rows, q, k = load(c)
        ex = _dot_f32(e_ref[...], la_ref[rows, :])
        qd_ref[rows, :] = (q * jnp.exp(ex[0:n])).astype(BF16)
        kd_ref[rows, :] = (k * jnp.exp(ex[n:2 * n])).astype(BF16)
        att = m_ref[GLA_LEVELS] * _dot_nt(q.astype(BF16), k.astype(BF16))
        for lvl in range(GLA_LEVELS):
            qs = (q * jnp.exp(ex[(2 + 2 * lvl) * n:(3 + 2 * lvl) * n])).astype(BF16)
            ks = (k * jnp.exp(ex[(3 + 2 * lvl) * n:(4 + 2 * lvl) * n])).astype(BF16)
            att += m_ref[lvl] * _dot_nt(qs, ks)
        oi_ref[rows, :] = _dot(att.astype(BF16), v_ref[rows, :]).astype(BF16)
        return carry

    @pl.when(mild)
    def _():
        lax.fori_loop(0, nchunk, mild_chunk, 0, unroll=True)

    @pl.when(jnp.logical_not(mild))
    def _():
        lax.fori_loop(0, nchunk, harsh_chunk, 0)


def _gla_blocks(m):
    tb = min(512, m)
    return tb, tb // GLA_CHUNK


def _gla_intra(px, cols, wa_pad, ba, reverse):
    m = px.shape[0]
    tb, nchunk = _gla_blocks(m)
    kb, vb, qb, ab = cols
    e, msk = _gla_tables(reverse)
    return pl.pallas_call(
        functools.partial(_gla_intra_kernel, nchunk=nchunk, reverse=reverse),
        out_shape=[jax.ShapeDtypeStruct((m, GLA_VAL), BF16), jax.ShapeDtypeStruct((m, GLA_KEY), BF16),
                   jax.ShapeDtypeStruct((m, GLA_KEY), BF16), jax.ShapeDtypeStruct((m // GLA_CHUNK, GLA_KEY), F32)],
        grid=(m // tb, GLA_HEADS),
        in_specs=[pl.BlockSpec((tb, GLA_DK), lambda i, h: (i, kb + h)),
                  pl.BlockSpec((tb, GLA_DV), lambda i, h: (i, vb + h)),
                  pl.BlockSpec((tb, GLA_DK), lambda i, h: (i, qb + h)),
                  pl.BlockSpec((tb, LANE), lambda i, h: (i, ab)),
                  pl.BlockSpec((3, LANE, GLA_DK), lambda i, h: (0, 0, h)),
                  pl.BlockSpec((1, GLA_DK), lambda i, h: (0, h)),
                  pl.BlockSpec(e.shape, lambda i, h: (0, 0)),
                  pl.BlockSpec(msk.shape, lambda i, h: (0, 0, 0))],
        out_specs=[pl.BlockSpec((tb, GLA_DV), lambda i, h: (i, h)),
                   pl.BlockSpec((tb, GLA_DK), lambda i, h: (i, h)),
                   pl.BlockSpec((tb, GLA_DK), lambda i, h: (i, h)),
                   pl.BlockSpec((nchunk, GLA_DK), lambda i, h: (i, h))],
        scratch_shapes=[pltpu.VMEM((tb, GLA_DK), F32), pltpu.VMEM((nchunk, GLA_DK), F32)],
        compiler_params=_cp("parallel", "parallel"),
        name="gla_intra_bwd" if reverse else "gla_intra_fwd",
    )(px, px, px, px, wa_pad, ba, e, msk)


def _gla_scan_kernel(*refs, nchunk):
    i = pl.program_id(1)
    n = GLA_CHUNK
    ins, outs, states = refs[:12], refs[12:16], refs[16:]
    dirs = [(ins[6 * d:6 * d + 6], outs[2 * d:2 * d + 2], states[d], bool(d)) for d in range(2)]

    @pl.when(i == 0)
    def _():
        for (_, _, _, _, _, s0_ref), _, s_ref, _ in dirs:
            s_ref[...] = s0_ref[...]

    def chunk(cc, carry):
        for (qd_ref, kd_ref, v_ref, dl_ref, oi_ref, _), (o_ref, _), s_ref, reverse in dirs:
            c = (nchunk - 1 - cc) if reverse else cc
            rows = pl.ds(pl.multiple_of(c * n, n), n)
            st = s_ref[...]
            o_ref[rows, :] = (oi_ref[rows, :].astype(F32) + _dot_nt(qd_ref[rows, :], st.astype(BF16))).astype(BF16)
            s_ref[...] = st * dl_ref[pl.ds(c, 1), :] + _dot_tn(v_ref[rows, :], kd_ref[rows, :])
        return carry

    lax.fori_loop(0, nchunk, chunk, 0, unroll=True)

    @pl.when(i == pl.num_programs(1) - 1)
    def _():
        for _, (_, sf_ref), s_ref, _ in dirs:
            sf_ref[...] = s_ref[...]


def _gla_scan(px, vb, fwd, bwd):
    m = px.shape[0]
    tb, nchunk = _gla_blocks(m)
    nb = m // tb

    def specs(blk):
        return [pl.BlockSpec((tb, GLA_DK), lambda h, i: (blk(i), h)),
                pl.BlockSpec((tb, GLA_DK), lambda h, i: (blk(i), h)),
                pl.BlockSpec((tb, GLA_DV), lambda h, i: (blk(i), vb + h)),
                pl.BlockSpec((nchunk, GLA_DK), lambda h, i: (blk(i), h)),
                pl.BlockSpec((tb, GLA_DV), lambda h, i: (blk(i), h)),
                pl.BlockSpec((None, GLA_DV, GLA_DK), lambda h, i: (h, 0, 0))]

    def out_specs(blk):
        return [pl.BlockSpec((tb, GLA_DV), lambda h, i: (blk(i), h)),
                pl.BlockSpec((None, GLA_DV, GLA_DK), lambda h, i: (h, 0, 0))]

    ahead, back = (lambda i: i), (lambda i: nb - 1 - i)
    out_shape = [jax.ShapeDtypeStruct((m, GLA_VAL), BF16), jax.ShapeDtypeStruct((GLA_HEADS, GLA_DV, GLA_DK), F32)]
    args = []
    for qd, kd, dl, oi, s0 in (fwd, bwd):
        args += [qd, kd, px, dl, oi, s0]
    return pl.pallas_call(
        functools.partial(_gla_scan_kernel, nchunk=nchunk),
        out_shape=out_shape + out_shape,
        grid=(GLA_HEADS, nb),
        in_specs=specs(ahead) + specs(back),
        out_specs=out_specs(ahead) + out_specs(back),
        scratch_shapes=[pltpu.VMEM((GLA_DV, GLA_DK), F32), pltpu.VMEM((GLA_DV, GLA_DK), F32)],
        compiler_params=_cp("parallel", "arbitrary"),
        name="gla_scan",
    )(*args)


def _gla_gate_kernel(of_ref, ob_ref, g_ref, gain_ref, o_ref):
    gain = gain_ref[...]
    for h in range(GLA_HEADS):
        sl = slice(h * GLA_DV, (h + 1) * GLA_DV)
        o = _rms(of_ref[:, sl].astype(F32) + ob_ref[:, sl].astype(F32)) * gain
        g = g_ref[:, sl].astype(F32)
        o_ref[:, sl] = (o * (g * jax.nn.sigmoid(g))).astype(BF16)


def _gla_gate(o_f, o_b, px, g_blk, gain):
    m = o_f.shape[0]
    tm = min(256, m)
    return pl.pallas_call(
        _gla_gate_kernel,
        out_shape=jax.ShapeDtypeStruct((m, GLA_VAL), BF16),
        grid=(m // tm,),
        in_specs=[pl.BlockSpec((tm, GLA_VAL), lambda i: (i, 0)), pl.BlockSpec((tm, GLA_VAL), lambda i: (i, 0)),
                  pl.BlockSpec((tm, GLA_VAL), lambda i: (i, g_blk)), pl.BlockSpec((1, GLA_DV), lambda i: (0, 0))],
        out_specs=pl.BlockSpec((tm, GLA_VAL), lambda i: (i, 0)),
        compiler_params=_cp("parallel"),
        name="gla_gate",
    )(o_f, o_b, px, gain)


def _pad_cols(w, mult):
    pad = (-w.shape[1]) % mult
    return jnp.pad(w, ((0, 0), (0, pad))) if pad else w


def _row(v, width=None):
    v = v.reshape(1, -1).astype(F32)
    return _pad_cols(v, width) if width else v


MM_TN = 512

AB_CKV_BLK = MLA_Q_RANK // MLA_KV_RANK
AB_CONV_COL = MLA_Q_RANK + MLA_KV_RANK
AB_KR_BLK = (AB_CONV_COL + 3 * CONV_WIDTH) // LANE
GLA_K_BLK = 0
GLA_V_BLK = GLA_KEY // GLA_DV
GLA_Q_BLK = (GLA_KEY + GLA_VAL) // GLA_DK
GLA_G_BLK = (2 * GLA_KEY + GLA_VAL) // GLA_VAL
GLA_A_BLK = (2 * GLA_KEY + 2 * GLA_VAL) // LANE


def _ab_w_in(w):
    ckv = w[:, :MLA_KV_RANK]
    kr = w[:, MLA_KV_RANK:MLA_KV_RANK + MLA_ROPE]
    cq = w[:, MLA_KV_RANK + MLA_ROPE:MLA_KV_RANK + MLA_ROPE + MLA_Q_RANK]
    conv = w[:, MLA_KV_RANK + MLA_ROPE + MLA_Q_RANK:]
    return _pad_cols(jnp.concatenate([cq, ckv, conv, kr], axis=1), MM_TN).astype(BF16)


def _ab_w_uq(w):
    w = w.reshape(MLA_Q_RANK, MLA_HEADS, MLA_QK)
    w = jnp.pad(w, ((0, 0), (0, 0), (0, MLA_HEAD_PAD - MLA_QK)))
    return w.reshape(MLA_Q_RANK, MLA_HEADS * MLA_HEAD_PAD).astype(BF16)


def _gla_w_in(w):
    o1, o2 = GLA_KEY, GLA_KEY + GLA_VAL
    o3 = o2 + 2 * GLA_RANK
    k, v, a, q, g = w[:, :o1], w[:, o1:o2], w[:, o2:o3], w[:, o3:o3 + GLA_KEY], w[:, o3 + GLA_KEY:]
    return _pad_cols(jnp.concatenate([k, v, q, g, a], axis=1), MM_TN).astype(BF16)


def _gla_wa_pad(w_a, offset):
    w = jnp.zeros((LANE, GLA_KEY), F32).at[offset:offset + GLA_RANK].set(w_a.astype(F32))
    hi = w.astype(BF16)
    r1 = w - hi.astype(F32)
    mid = r1.astype(BF16)
    lo = (r1 - mid.astype(F32)).astype(BF16)
    return jnp.stack([hi, mid, lo])


def _rope_tables(pos_rows, pos_cols):
    half = MLA_ROPE // 4
    inv_freq = ROPE_BASE ** (-jnp.arange(half, dtype=F32) / half)
    ar = pos_rows.astype(F32)[:, None] * inv_freq[None, :]
    ac = pos_cols.astype(F32)[:, None] * inv_freq[None, :]
    zeros = jnp.zeros((ar.shape[0], LANE - MLA_ROPE), F32)
    cos = jnp.concatenate([jnp.cos(ar), jnp.cos(ar), jnp.cos(ac), jnp.cos(ac), zeros], axis=1)
    sin = jnp.concatenate([-jnp.sin(ar), jnp.sin(ar), -jnp.sin(ac), jnp.sin(ac), zeros], axis=1)
    return cos, sin


def _mixer_mla_conv(hx, hc, p, rope_x, rope_c, need_ctx):
    w_in, w_uq, w_ukv, w_out = p["w_in"], p["w_uq"], p["w_ukv"], p["w_out"]

    def project(h, rope):
        px = _mm([(h[0], w_in)], out_dtype=BF16, modulation=h[1], name="ab_in_proj")
        k, v = _kv_up(px, AB_CKV_BLK, AB_KR_BLK, p["kv_a_gain"], w_ukv, p["k_gain_n"], p["k_gain_r"], *rope)
        return px, k, v

    def branch_out(px, k, v, rope):
        q = _q_up(px, p["q_a_gain"], w_uq, p["q_gain_n"], p["q_gain_r"], *rope, p["k_gain_n"], p["k_gain_r"])
        o = _attention(q, k, v)
        y = _conv(px, AB_CONV_COL, p["conv_w"], p["conv_b"])
        return o, y

    px_c, k_c, v_c = project(hc, rope_c)
    px_x, k_x, v_x = project(hx, rope_x)
    k_all = jnp.concatenate([k_c, k_x], axis=0)
    v_all = jnp.concatenate([v_c, v_x], axis=0)
    out_x = branch_out(px_x, k_all, v_all, rope_x)
    out_c = branch_out(px_c, k_c, v_c, rope_c) if need_ctx else None
    half = MLA_HEADS * MLA_V
    return out_x, out_c, (w_out[:half], w_out[half:])


def _mixer_gla(hx, hc, p, need_ctx):
    cols = (GLA_K_BLK, GLA_V_BLK, GLA_Q_BLK, GLA_A_BLK)
    zero_state = jnp.zeros((GLA_HEADS, GLA_DV, GLA_DK), F32)

    def run(h, s_f, s_b):
        px = _mm([(h[0], p["w_in"])], out_dtype=BF16, modulation=h[1], name="gla_in_proj")
        sides = []
        for reverse, wa, ba, s0 in ((False, p["wa_f"], p["ba_f"], s_f), (True, p["wa_b"], p["ba_b"], s_b)):
            oi, qd, kd, dl = _gla_intra(px, cols, wa, ba, reverse)
            sides.append((qd, kd, dl, oi, s0))
        o_f, sf_fin, o_b, sb_fin = _gla_scan(px, GLA_V_BLK, *sides)
        return px, (o_f, o_b), (sf_fin, sb_fin)

    px_c, o_c, (s_f, s_b) = run(hc, zero_state, zero_state)
    px_x, o_x, _ = run(hx, s_f, s_b)
    out_x = _gla_gate(o_x[0], o_x[1], px_x, GLA_G_BLK, p["o_gain"])
    out_c = _gla_gate(o_c[0], o_c[1], px_c, GLA_G_BLK, p["o_gain"]) if need_ctx else None
    return out_x, out_c


def _moe_dense(xres, gain, shift, scale, gate, router, wg, wu, wd3):
    h, gates = _modulate(xres, gain, shift, scale, router=router, route="dense")
    hid = _moe_up(h, wg, wu, gates)
    wd = wd3.reshape(N_EXPERTS * D_FF, wd3.shape[-1])
    return _mm([(hid, wd)], out_dtype=F32, residual=(xres, gate), name="moe_down")


def _moe(xres, *args):
    routed = xres.shape[0] >= 4 * MOE_ROW_TILE and xres.shape[0] % MOE_TOK_TILE == 0
    return (_moe_routed if routed else _moe_dense)(xres, *args)


def kernel(x, c, ctx, c_ctx, ada_w, ada_b, norm_mix_gain, norm_ffn_gain, ab_w_in, ab_q_a_gain, ab_w_uq, ab_q_gain, ab_kv_a_gain, ab_w_ukv, ab_k_gain, ab_conv_w, ab_conv_b, ab_w_out, gla_w_in, gla_w_af, gla_b_af, gla_w_ab, gla_b_ab, gla_o_gain, gla_w_out, router_w, router_b, moe_w_gate, moe_w_up, moe_w_down):
    b, s, d = x.shape
    assert b == 1 and s % GRID_W == 0
    depth = ada_w.shape[0]
    xs, cx = x[0], ctx[0]
    n_ctx = cx.shape[0]

    mod = _adaln(c.reshape(d, 1), c_ctx.reshape(d, 1), ada_w, ada_b).reshape(depth, 2, 6, 1, d)
    tok = jnp.arange(s, dtype=jnp.int32)
    rope_x = _rope_tables(tok // GRID_W, tok % GRID_W)
    rope_c = (jnp.concatenate([jnp.ones((n_ctx, MLA_ROPE), F32), jnp.zeros((n_ctx, LANE - MLA_ROPE), F32)], axis=1),
              jnp.zeros((n_ctx, LANE), F32))
    wr = _pad_cols(router_w.astype(F32), LANE)
    wr_hi = wr.astype(BF16)
    wr_lo = (wr - wr_hi.astype(F32)).astype(BF16)
    router = (jnp.stack([wr_hi, wr_lo]), router_b.reshape(N_EXPERTS, 1).astype(F32))

    for i in range(depth):
        last = i == depth - 1
        j = i // 2
        mx, mc = mod[i, 0], mod[i, 1]
        gain_mix = _row(norm_mix_gain[i])
        hx = (xs, (gain_mix, mx[0], mx[1]))
        hc = (cx, (gain_mix, mc[0], mc[1]))
        if i % 2 == 0:
            p = dict(
                w_in=_ab_w_in(ab_w_in[j]), w_uq=_ab_w_uq(ab_w_uq[j]), w_ukv=ab_w_ukv[j].astype(BF16),
                w_out=ab_w_out[j].astype(BF16),
                q_a_gain=_row(ab_q_a_gain[j]), kv_a_gain=_row(ab_kv_a_gain[j]),
                q_gain_n=_row(ab_q_gain[j][:MLA_NOPE]), q_gain_r=_row(ab_q_gain[j][MLA_NOPE:], LANE),
                k_gain_n=_row(ab_k_gain[j][:MLA_NOPE]), k_gain_r=_row(ab_k_gain[j][MLA_NOPE:], LANE),
                conv_w=ab_conv_w[j].astype(F32), conv_b=_row(ab_conv_b[j]))
            (o_x, y_x), out_c, (w_o, w_y) = _mixer_mla_conv(hx, hc, p, rope_x, rope_c, not last)
            xs = _mm([(o_x, w_o), (y_x, w_y)], out_dtype=F32, residual=(xs, mx[2]), name="ab_out_proj")
            if not last:
                cx = _mm([(out_c[0], w_o), (out_c[1], w_y)], out_dtype=F32, residual=(cx, mc[2]), name="ab_out_proj")
        else:
            p = dict(
                w_in=_gla_w_in(gla_w_in[j]),
                wa_f=_gla_wa_pad(gla_w_af[j], 0), wa_b=_gla_wa_pad(gla_w_ab[j], GLA_RANK),
                ba_f=_row(gla_b_af[j]), ba_b=_row(gla_b_ab[j]), o_gain=_row(gla_o_gain[j]))
            w_out = gla_w_out[j].astype(BF16)
            y_x, y_c = _mixer_gla(hx, hc, p, not last)
            xs = _mm([(y_x, w_out)], out_dtype=F32, residual=(xs, mx[2]), name="gla_out_proj")
            if not last:
                cx = _mm([(y_c, w_out)], out_dtype=F32, residual=(cx, mc[2]), name="gla_out_proj")
        gain_ffn = _row(norm_ffn_gain[i])
        wg, wu = moe_w_gate[i].astype(BF16), moe_w_up[i].astype(BF16)
        wd = moe_w_down[i].astype(BF16)
        xs = _moe(xs, gain_ffn, mx[3], mx[4], mx[5], router, wg, wu, wd)
        if not last:
            cx = _moe(cx, gain_ffn, mc[3], mc[4], mc[5], router, wg, wu, wd)
    return xs[None]
```

```python
import functools

import numpy as np
import jax
import jax.numpy as jnp
from jax import lax
from jax.experimental import pallas as pl
from jax.experimental.pallas import tpu as pltpu

F32 = jnp.float32
BF16 = jnp.bfloat16
HIGHEST = lax.Precision.HIGHEST

LANE = 128
SUBLANE = 8
VMEM_LIMIT_BYTES = 56 * 1024 * 1024

GRID_W = 64
EPS = 1e-6
MLA_HEADS = 16
MLA_Q_RANK = 1024
MLA_KV_RANK = 512
MLA_NOPE = 128
MLA_ROPE = 64
MLA_V = 128
MLA_QK = MLA_NOPE + MLA_ROPE
MLA_HEAD_PAD = 2 * LANE
ROPE_BASE = 10000.0
LOG2E = 1.4426950408889634
CONV_WIDTH = 2048
GLA_HEADS = 8
GLA_DK = 256
GLA_DV = 512
GLA_KEY = GLA_HEADS * GLA_DK
GLA_VAL = GLA_HEADS * GLA_DV
GLA_RANK = 16
GLA_TEMP = 16.0
GLA_CHUNK = 64
N_EXPERTS = 16
N_GROUPS = 4
GROUP_SIZE = N_EXPERTS // N_GROUPS
D_FF = 256


def _cp(*sem):
    return pltpu.CompilerParams(dimension_semantics=sem, vmem_limit_bytes=VMEM_LIMIT_BYTES)


def _dot(a, b):
    return jnp.dot(a, b, preferred_element_type=F32)


def _dot_nt(a, b):
    return lax.dot_general(a, b, (((1,), (1,)), ((), ())), preferred_element_type=F32)


def _dot_tn(a, b):
    return lax.dot_general(a, b, (((0,), (0,)), ((), ())), preferred_element_type=F32)


def _dot_f32(a, b):
    return jnp.dot(a, b, precision=HIGHEST, preferred_element_type=F32)


def _rms(v, n=None):
    ss = jnp.sum(v * v, axis=-1, keepdims=True)
    return v * lax.rsqrt(ss * (1.0 / (n if n is not None else v.shape[-1])) + EPS)


def _adaln_kernel(c_ref, cc_ref, w_ref, b_ref, o_ref, acc_ref):
    k = pl.program_id(2)

    @pl.when(k == 0)
    def _():
        acc_ref[...] = jnp.zeros_like(acc_ref)

    w = w_ref[...]
    tk, tn = w.shape
    for r, ref in enumerate((c_ref, cc_ref)):
        cv = ref[...]
        cs = cv * jax.nn.sigmoid(cv)
        acc_ref[r] += jnp.sum((w * cs).reshape(tk // SUBLANE, SUBLANE, tn), axis=0)

    @pl.when(k == pl.num_programs(2) - 1)
    def _():
        o_ref[...] = jnp.sum(acc_ref[...], axis=1) + b_ref[...]


def _adaln(c_col, cc_col, ada_w, ada_b):
    depth, d, n = ada_w.shape
    tk, tn = 2048, 2048
    return pl.pallas_call(
        _adaln_kernel,
        out_shape=jax.ShapeDtypeStruct((depth, 2, n), F32),
        grid=(depth, n // tn, d // tk),
        in_specs=[
            pl.BlockSpec((tk, 1), lambda l, j, k: (k, 0)),
            pl.BlockSpec((tk, 1), lambda l, j, k: (k, 0)),
            pl.BlockSpec((None, tk, tn), lambda l, j, k: (l, k, j)),
            pl.BlockSpec((None, 1, tn), lambda l, j, k: (l, 0, j)),
        ],
        out_specs=pl.BlockSpec((None, 2, tn), lambda l, j, k: (l, 0, j)),
        scratch_shapes=[pltpu.VMEM((2, SUBLANE, tn), F32)],
        compiler_params=_cp("parallel", "parallel", "arbitrary"),
        name="adaln_mod",
    )(c_col, cc_col, ada_w, ada_b.reshape(depth, 1, n))


def _route_topk(logits, bias_col):
    tm = logits.shape[0]
    s = jax.nn.sigmoid(logits.T[:N_EXPERTS])
    sel = s + bias_col
    row = [sel[e:e + 1] for e in range(N_EXPERTS)]
    score = []
    for g in range(N_GROUPS):
        a, b, c2, d2 = row[GROUP_SIZE * g:GROUP_SIZE * (g + 1)]
        hi1, lo1 = jnp.maximum(a, b), jnp.minimum(a, b)
        hi2, lo2 = jnp.maximum(c2, d2), jnp.minimum(c2, d2)
        score.append(jnp.maximum(hi1, hi2) + jnp.maximum(jnp.minimum(hi1, hi2), jnp.maximum(lo1, lo2)))
    best, gidx = score[0], jnp.zeros((1, tm), jnp.int32)
    for g in range(1, N_GROUPS):
        upd = score[g] > best
        gidx = jnp.where(upd, g, gidx)
        best = jnp.where(upd, score[g], best)
    neg = jnp.full((1, tm), -jnp.inf, F32)
    cand = [jnp.where(gidx == (e // GROUP_SIZE), row[e], neg) for e in range(N_EXPERTS)]

    def argmax_first(vals):
        bv, bi = vals[0], jnp.zeros((1, tm), jnp.int32)
        for e in range(1, N_EXPERTS):
            upd = vals[e] > bv
            bi = jnp.where(upd, e, bi)
            bv = jnp.where(upd, vals[e], bv)
        return bi

    i1 = argmax_first(cand)
    i2 = argmax_first([jnp.where(i1 == e, neg, cand[e]) for e in range(N_EXPERTS)])
    eidx = lax.broadcasted_iota(jnp.int32, (N_EXPERTS, tm), 0)
    w1 = jnp.sum(jnp.where(eidx == i1, s, 0.0), axis=0, keepdims=True)
    w2 = jnp.sum(jnp.where(eidx == i2, s, 0.0), axis=0, keepdims=True)
    den = w1 + w2
    return i1, i2, w1 / den, w2 / den


def _dense_gates(i1, i2, w1, w2):
    efull = lax.broadcasted_iota(jnp.int32, (LANE, i1.shape[1]), 0)
    return (jnp.where(efull == i1, w1, 0.0) + jnp.where(efull == i2, w2, 0.0)).T


INFO_E1, INFO_E2, INFO_W1, INFO_W2, INFO_R1, INFO_R2 = range(6)


def _routing_record(i1, i2, w1, w2, carry_ref):
    tm = i1.shape[1]
    efull = lax.broadcasted_iota(jnp.int32, (LANE, tm), 0)
    rec_t = (jnp.where(efull == INFO_E1, i1.astype(F32), 0.0) + jnp.where(efull == INFO_E2, i2.astype(F32), 0.0)
             + jnp.where(efull == INFO_W1, w1, 0.0) + jnp.where(efull == INFO_W2, w2, 0.0))
    rec = rec_t.T
    onehot = jnp.where((efull == i1) | (efull == i2), 1.0, 0.0).T
    r = lax.broadcasted_iota(jnp.int32, (tm, tm), 0)
    c = lax.broadcasted_iota(jnp.int32, (tm, tm), 1)
    incl = _dot(jnp.where(r >= c, 1.0, 0.0).astype(BF16), onehot.astype(BF16)) + carry_ref[...]
    carry_ref[...] = incl[tm - 1:tm, :]
    rank = incl - onehot
    lane = lax.broadcasted_iota(jnp.int32, (tm, LANE), 1)
    lane_f = lane.astype(F32)
    r1 = jnp.sum(jnp.where(lane_f == rec[:, INFO_E1:INFO_E1 + 1], rank, 0.0), axis=-1, keepdims=True)
    r2 = jnp.sum(jnp.where(lane_f == rec[:, INFO_E2:INFO_E2 + 1], rank, 0.0), axis=-1, keepdims=True)
    return jnp.where(lane == INFO_R1, r1, jnp.where(lane == INFO_R2, r2, rec))


def _pack_bf16_pairs(v):
    n = v.shape[1] // 2
    hi = lax.bitcast_convert_type(v[:, :n].astype(BF16).astype(F32), jnp.uint32)
    lo = lax.bitcast_convert_type(v[:, n:].astype(BF16).astype(F32), jnp.uint32)
    return hi | (lo >> 16)


def _unpack_bf16_pairs(w):
    first = lax.bitcast_convert_type(w & jnp.uint32(0xFFFF0000), F32)
    second = lax.bitcast_convert_type(w << 16, F32)
    return first, second


def _modulate_kernel(x_ref, g_ref, sh_ref, sc_ref, *rest, route):
    x = x_ref[...]
    h = _rms(x) * g_ref[...] * (1.0 + sc_ref[...]) + sh_ref[...]
    if route is None:
        (h_ref,) = rest
        h_ref[...] = h.astype(h_ref.dtype)
        return
    wr_ref, br_ref = rest[:2]
    h_hi = h.astype(BF16)
    h_lo = (h - h_hi.astype(F32)).astype(BF16)
    logits = _dot(h_hi, wr_ref[0]) + (_dot(h_lo, wr_ref[0]) + _dot(h_hi, wr_ref[1]))
    topk = _route_topk(logits, br_ref[...])
    if route == "dense":
        h_ref, gates_ref = rest[2:]
        h_ref[...] = h.astype(h_ref.dtype)
        gates_ref[...] = _dense_gates(*topk)
    else:
        h_ref, rec_ref, cnt_ref, carry_ref = rest[2:]

        @pl.when(pl.program_id(0) == 0)
        def _():
            carry_ref[...] = jnp.zeros_like(carry_ref)

        h_ref[...] = _pack_bf16_pairs(h)
        rec_ref[...] = _routing_record(*topk, carry_ref)
        cnt_ref[...] = carry_ref[...]


def _modulate(x, gain, shift, scale, router=None, route=None):
    m, d = x.shape
    tm = min(256, m)
    vec = pl.BlockSpec((1, d), lambda i: (0, 0))
    in_specs = [pl.BlockSpec((tm, d), lambda i: (i, 0)), vec, vec, vec]
    args = [x, gain, shift, scale]
    out_shape = [jax.ShapeDtypeStruct((m, d), BF16)]
    out_specs = [pl.BlockSpec((tm, d), lambda i: (i, 0))]
    scratch = []
    if route is not None:
        wr, br = router
        in_specs += [pl.BlockSpec((2, d, LANE), lambda i: (0, 0, 0)), pl.BlockSpec((N_EXPERTS, 1), lambda i: (0, 0))]
        args += [wr, br]
        out_shape.append(jax.ShapeDtypeStruct((m, LANE), F32))
        out_specs.append(pl.BlockSpec((tm, LANE), lambda i: (i, 0)))
    if route == "routed":
        out_shape[0] = jax.ShapeDtypeStruct((m, d // 2), jnp.uint32)
        out_specs[0] = pl.BlockSpec((tm, d // 2), lambda i: (i, 0))
        out_shape.append(jax.ShapeDtypeStruct((1, LANE), F32))
        out_specs.append(pl.BlockSpec((1, LANE), lambda i: (0, 0)))
        scratch = [pltpu.VMEM((1, LANE), F32)]
    out = pl.pallas_call(
        functools.partial(_modulate_kernel, route=route),
        out_shape=out_shape, grid=(m // tm,), in_specs=in_specs, out_specs=out_specs, scratch_shapes=scratch,
        compiler_params=_cp("arbitrary" if route == "routed" else "parallel"),
        name="modulate" if route is None else "modulate_route_" + route,
    )(*args)
    return out[0] if route is None else out


def _mm_kernel(*refs, n_pairs, residual, modulated):
    a_refs, w_refs = refs[:n_pairs], refs[n_pairs:2 * n_pairs]
    rest = refs[2 * n_pairs:]
    if modulated:
        (g_ref, sh_ref, sc_ref), rest, h_ref = rest[:3], rest[3:-1], rest[-1]

        @pl.when(pl.program_id(1) == 0)
        def _():
            h_ref[...] = (_rms(a_refs[0][...]) * g_ref[...] * (1.0 + sc_ref[...]) + sh_ref[...]).astype(BF16)

        acc = _dot(h_ref[...], w_refs[0][...])
    else:
        acc = _dot(a_refs[0][...], w_refs[0][...])
    o_ref = rest[-1]
    for a_ref, w_ref in zip(a_refs[1:], w_refs[1:]):
        acc += _dot(a_ref[...], w_ref[...])
    if residual:
        x_ref, gate_ref = rest[:2]
        acc = x_ref[...] + gate_ref[...] * acc
    o_ref[...] = acc.astype(o_ref.dtype)


def _mm(pairs, *, out_dtype, residual=None, modulation=None, tm=1024, tn=512, name="mm"):
    m = pairs[0][0].shape[0]
    n = pairs[0][1].shape[1]
    if modulation is not None:
        tm = tm // 2
        tn = next(c for c in (1280, 1024, 512, 256, 128) if n % c == 0)
    tm, tn = min(tm, m), min(tn, n)
    assert m % tm == 0 and n % tn == 0, (m, n, tm, tn)
    in_specs = [pl.BlockSpec((tm, a.shape[1]), lambda i, j: (i, 0)) for a, _ in pairs]
    in_specs += [pl.BlockSpec((w.shape[0], tn), lambda i, j: (0, j)) for _, w in pairs]
    args = [a for a, _ in pairs] + [w for _, w in pairs]
    scratch = []
    if modulation is not None:
        k0 = pairs[0][0].shape[1]
        in_specs += [pl.BlockSpec((1, k0), lambda i, j: (0, 0))] * 3
        args += list(modulation)
        scratch = [pltpu.VMEM((tm, k0), BF16)]
    if residual is not None:
        in_specs += [pl.BlockSpec((tm, tn), lambda i, j: (i, j)), pl.BlockSpec((1, tn), lambda i, j: (0, j))]
        args += list(residual)
    return pl.pallas_call(
        functools.partial(_mm_kernel, n_pairs=len(pairs), residual=residual is not None,
                          modulated=modulation is not None),
        out_shape=jax.ShapeDtypeStruct((m, n), out_dtype),
        grid=(m // tm, n // tn), in_specs=in_specs,
        out_specs=pl.BlockSpec((tm, tn), lambda i, j: (i, j)),
        scratch_shapes=scratch,
        compiler_params=_cp("parallel", "arbitrary"),
        name=name,
    )(*args)


def _rope_rotate(v, cos, sin_signed):
    lane = lax.broadcasted_iota(jnp.int32, (1, LANE), 1)
    first_half = (lane % 32) < 16
    partner = jnp.where(first_half, pltpu.roll(v, LANE - 16, axis=1), pltpu.roll(v, 16, axis=1))
    return v * cos + partner * sin_signed


def _norm_bound_sq(gn, gr):
    return (MLA_NOPE * jnp.max(gn * gn, axis=-1, keepdims=True)
            + MLA_ROPE * jnp.max(gr * gr, axis=-1, keepdims=True))


def _q_up_kernel(cq_ref, ga_ref, w_ref, gn_ref, gr_ref, cos_ref, sin_ref, kgn_ref, kgr_ref, o_ref, *, scale):
    cqn = (_rms(cq_ref[...].astype(F32)) * ga_ref[...]).astype(BF16)
    q = _dot(cqn, w_ref[...])
    cos, sin = cos_ref[...], sin_ref[...]
    gn, gr = gn_ref[...] * scale, gr_ref[...] * scale
    lane = lax.broadcasted_iota(jnp.int32, (1, LANE), 1)
    offset = SCORE_CENTER - jnp.sqrt(_norm_bound_sq(gn, gr) * _norm_bound_sq(kgn_ref[...], kgr_ref[...]))
    for h in range(MLA_HEADS):
        o = h * MLA_HEAD_PAD
        qr = _rope_rotate(_rms(q[:, o + LANE:o + 2 * LANE], MLA_ROPE) * gr, cos, sin)
        o_ref[:, o:o + LANE] = (_rms(q[:, o:o + LANE]) * gn).astype(BF16)
        o_ref[:, o + LANE:o + 2 * LANE] = jnp.where(lane == OFFSET_LANE, offset, qr).astype(BF16)


def _q_up(px, ga, w, gn, gr, cos, sin, kgn, kgr):
    m = px.shape[0]
    tm = min(512, m)
    n = MLA_HEADS * MLA_HEAD_PAD
    vec = lambda width: pl.BlockSpec((1, width), lambda i: (0, 0))
    return pl.pallas_call(
        functools.partial(_q_up_kernel, scale=MLA_QK ** -0.5 * LOG2E),
        out_shape=jax.ShapeDtypeStruct((m, n), BF16),
        grid=(m // tm,),
        in_specs=[pl.BlockSpec((tm, MLA_Q_RANK), lambda i: (i, 0)), vec(MLA_Q_RANK),
                  pl.BlockSpec((MLA_Q_RANK, n), lambda i: (0, 0)), vec(LANE), vec(LANE),
                  pl.BlockSpec((tm, LANE), lambda i: (i, 0)), pl.BlockSpec((tm, LANE), lambda i: (i, 0)),
                  vec(LANE), vec(LANE)],
        out_specs=pl.BlockSpec((tm, n), lambda i: (i, 0)),
        compiler_params=_cp("parallel"),
        name="mla_q_up",
    )(px, ga, w, gn, gr, cos, sin, kgn, kgr)


def _kv_up_kernel(ckv_ref, kr_ref, ga_ref, w_ref, gn_ref, gr_ref, cos_ref, sin_ref, k_ref, v_ref):
    ckvn = (_rms(ckv_ref[...].astype(F32)) * ga_ref[...]).astype(BF16)
    kv = _dot(ckvn, w_ref[...])
    kr = _rms(kr_ref[...].astype(F32), MLA_ROPE) * gr_ref[...]
    kr = _rope_rotate(kr, cos_ref[...], sin_ref[...])
    lane = lax.broadcasted_iota(jnp.int32, (1, LANE), 1)
    krb = jnp.where(lane == OFFSET_LANE, 1.0, kr).astype(BF16)
    gn = gn_ref[...]
    for h in range(MLA_HEADS):
        o = h * MLA_HEAD_PAD
        k_ref[:, o:o + LANE] = (_rms(kv[:, o:o + LANE]) * gn).astype(BF16)
        k_ref[:, o + LANE:o + 2 * LANE] = krb
        v_ref[:, h * MLA_V:(h + 1) * MLA_V] = kv[:, o + LANE:o + 2 * LANE].astype(BF16)


def _kv_up(px, ckv_blk, kr_blk, ga, w, gn, gr, cos, sin):
    m = px.shape[0]
    tm = min(512, m)
    vec = lambda width: pl.BlockSpec((1, width), lambda i: (0, 0))
    return pl.pallas_call(
        _kv_up_kernel,
        out_shape=[jax.ShapeDtypeStruct((m, MLA_HEADS * MLA_HEAD_PAD), BF16),
                   jax.ShapeDtypeStruct((m, MLA_HEADS * MLA_V), BF16)],
        grid=(m // tm,),
        in_specs=[pl.BlockSpec((tm, MLA_KV_RANK), lambda i: (i, ckv_blk)),
                  pl.BlockSpec((tm, LANE), lambda i: (i, kr_blk)), vec(MLA_KV_RANK),
                  pl.BlockSpec((MLA_KV_RANK, MLA_HEADS * MLA_HEAD_PAD), lambda i: (0, 0)), vec(LANE), vec(LANE),
                  pl.BlockSpec((tm, LANE), lambda i: (i, 0)), pl.BlockSpec((tm, LANE), lambda i: (i, 0))],
        out_specs=[pl.BlockSpec((tm, MLA_HEADS * MLA_HEAD_PAD), lambda i: (i, 0)),
                   pl.BlockSpec((tm, MLA_HEADS * MLA_V), lambda i: (i, 0))],
        compiler_params=_cp("parallel"),
        name="mla_kv_up",
    )(px, px, ga, w, gn, gr, cos, sin)


def _attn_kernel(q_ref, k_ref, v_ref, o_ref, sa_ref, sb_ref, m_ref, l_ref, acc_ref, *, ck, n_chunks):
    def sub(i):
        return pl.ds(pl.multiple_of(i * ATTN_SUB, ATTN_SUB), ATTN_SUB)

    l_ref[...] = jnp.zeros_like(l_ref)
    acc_ref[...] = jnp.zeros_like(acc_ref)

    n_sub = n_chunks * ck // ATTN_SUB
    unroll = next(u for u in (13, 8, 5, 4, 3, 2, 1) if n_sub % u == 0)

    def stream(t, carry):
        lsum = l_ref[...]
        acc = acc_ref[...]
        for u in range(unroll):
            i = t * unroll + u
            p = jnp.exp2(_dot_nt(q_ref[...], k_ref[sub(i), :]))
            for part in range(ATTN_SUB // LANE):
                lsum += p[:, part * LANE:(part + 1) * LANE]
            acc += _dot(p.astype(BF16), v_ref[sub(i), :])
        l_ref[...] = lsum
        acc_ref[...] = acc
        return carry

    lax.fori_loop(0, n_sub // unroll, stream, 0)
    row_sum = jnp.sum(l_ref[...], axis=-1, keepdims=True)
    in_range = jnp.min(row_sum) >= SCORE_SUM_FLOOR

    @pl.when(in_range)
    def _():
        o_ref[...] = (acc_ref[...] / row_sum).astype(o_ref.dtype)

    @pl.when(jnp.logical_not(in_range))
    def _():
        _attn_online(q_ref, k_ref, v_ref, o_ref, sa_ref, sb_ref, m_ref, l_ref, acc_ref, ck=ck, n_chunks=n_chunks)


def _attn_online(q_ref, k_ref, v_ref, o_ref, sa_ref, sb_ref, m_ref, l_ref, acc_ref, *, ck, n_chunks):
    m_ref[...] = jnp.full_like(m_ref, -jnp.inf)
    l_ref[...] = jnp.zeros_like(l_ref)
    acc_ref[...] = jnp.zeros_like(acc_ref)

    def rows(c):
        return pl.ds(pl.multiple_of(c * ck, ck), ck)

    def scores(c, s_ref):
        s_ref[...] = _dot_nt(q_ref[...], k_ref[rows(c), :])

    def update(c, s_ref):
        m_prev = m_ref[...]
        m_new = jnp.maximum(m_prev, jnp.max(s_ref[...], axis=-1, keepdims=True))
        alpha = jnp.exp2(m_prev - m_new)
        m_ref[...] = m_new
        lsum = alpha * l_ref[...]
        acc = alpha * acc_ref[...]
        for j in range(ck // ATTN_SUB):
            p = jnp.exp2(s_ref[:, j * ATTN_SUB:(j + 1) * ATTN_SUB] - m_new)
            for t in range(ATTN_SUB // LANE):
                lsum += p[:, t * LANE:(t + 1) * LANE]
            acc += _dot(p.astype(BF16), v_ref[pl.ds(pl.multiple_of(c * ck + j * ATTN_SUB, ATTN_SUB), ATTN_SUB), :])
        l_ref[...] = lsum
        acc_ref[...] = acc

    bufs = (sa_ref, sb_ref)
    scores(0, sa_ref)
    group = 4 if (n_chunks - 1) % 4 == 0 else 2
    n_groups = (n_chunks - 1) // group

    def body(t, carry):
        c = group * t
        for u in range(group):
            scores(c + u + 1, bufs[(u + 1) % 2])
            update(c + u, bufs[u % 2])
        return carry

    if n_groups:
        lax.fori_loop(0, n_groups, body, 0)
    done = group * n_groups
    for c in range(done, n_chunks):
        if c + 1 < n_chunks:
            scores(c + 1, bufs[(c + 1) % 2])
        update(c, bufs[c % 2])
    o_ref[...] = (acc_ref[...] / jnp.sum(l_ref[...], axis=-1, keepdims=True)).astype(o_ref.dtype)


ATTN_SUB = 2 * LANE
OFFSET_LANE = MLA_ROPE
SCORE_CENTER = 64.0
SCORE_SUM_FLOOR = 2.0 ** -60


def _attn_tiles(tq_total, tk_total):
    tq = min(1024, tq_total)
    ck = tk_total
    for cand in (1280, 1024, 512, 256):
        if tk_total % cand == 0:
            ck = cand
            break
    return tq, ck


def _attention(q, k, v):
    tq_total, tk_total = q.shape[0], k.shape[0]
    tq, ck = _attn_tiles(tq_total, tk_total)
    return pl.pallas_call(
        functools.partial(_attn_kernel, ck=ck, n_chunks=tk_total // ck),
        out_shape=jax.ShapeDtypeStruct((tq_total, MLA_HEADS * MLA_V), BF16),
        grid=(MLA_HEADS, tq_total // tq),
        in_specs=[pl.BlockSpec((tq, MLA_HEAD_PAD), lambda h, i: (i, h)),
                  pl.BlockSpec((tk_total, MLA_HEAD_PAD), lambda h, i: (0, h), pipeline_mode=pl.Buffered(1)),
                  pl.BlockSpec((tk_total, MLA_V), lambda h, i: (0, h), pipeline_mode=pl.Buffered(1))],
        out_specs=pl.BlockSpec((tq, MLA_V), lambda h, i: (i, h)),
        scratch_shapes=[pltpu.VMEM((tq, ck), F32), pltpu.VMEM((tq, ck), F32),
                        pltpu.VMEM((tq, 1), F32), pltpu.VMEM((tq, LANE), F32), pltpu.VMEM((tq, MLA_V), F32)],
        compiler_params=_cp("parallel", "arbitrary"),
        name="mla_attention",
    )(q, k, v)


CONV_TILE = 512
HALO = 16


def _conv_kernel(gb_ref, gc_ref, u_ref, gcp_ref, up_ref, gcn_ref, un_ref, w_ref, b_ref, o_ref):
    i = pl.program_id(0)
    z = gc_ref[...].astype(F32) * u_ref[...].astype(F32)
    tm = z.shape[0]
    z_before = gcp_ref[HALO - 1:HALO, :].astype(F32) * up_ref[HALO - 1:HALO, :].astype(F32)
    z_after = gcn_ref[0:1, :].astype(F32) * un_ref[0:1, :].astype(F32)
    z_before = jnp.where(i > 0, z_before, 0.0)
    z_after = jnp.where(i < pl.num_programs(0) - 1, z_after, 0.0)
    rowid = lax.broadcasted_iota(jnp.int32, z.shape, 0)
    z_prev = jnp.where(rowid == 0, z_before, pltpu.roll(z, 1, axis=0))
    z_next = jnp.where(rowid == tm - 1, z_after, pltpu.roll(z, tm - 1, axis=0))
    w = w_ref[...]
    y = z_prev * w[0:1] + z * w[1:2] + z_next * w[2:3] + b_ref[...]
    o_ref[...] = (gb_ref[...].astype(F32) * y).astype(o_ref.dtype)


def _conv(px, col0, conv_w, conv_b):
    m = px.shape[0]
    tm = min(512, m)
    cb = CONV_WIDTH // CONV_TILE
    b0 = col0 // CONV_TILE
    nhalo = m // HALO
    main = lambda off: pl.BlockSpec((tm, CONV_TILE), lambda i, j: (i, b0 + off * cb + j))
    prev = lambda off: pl.BlockSpec(
        (HALO, CONV_TILE), lambda i, j: (jnp.maximum(i * (tm // HALO) - 1, 0), b0 + off * cb + j))
    nxt = lambda off: pl.BlockSpec(
        (HALO, CONV_TILE), lambda i, j: (jnp.minimum((i + 1) * (tm // HALO), nhalo - 1), b0 + off * cb + j))
    return pl.pallas_call(
        _conv_kernel,
        out_shape=jax.ShapeDtypeStruct((m, CONV_WIDTH), BF16),
        grid=(m // tm, cb),
        in_specs=[main(0), main(1), main(2), prev(1), prev(2), nxt(1), nxt(2),
                  pl.BlockSpec((3, CONV_TILE), lambda i, j: (0, j)),
                  pl.BlockSpec((1, CONV_TILE), lambda i, j: (0, j))],
        out_specs=pl.BlockSpec((tm, CONV_TILE), lambda i, j: (i, j)),
        compiler_params=_cp("parallel", "parallel"),
        name="gated_conv3",
    )(px, px, px, px, px, px, px, conv_w, conv_b)


def _moe_up_kernel(h_ref, wg_ref, wu_ref, gates_ref, o_ref):
    e = pl.program_id(1)
    h = h_ref[...]
    a = _dot(h, wg_ref[...])
    u = _dot(h, wu_ref[...])
    gates = gates_ref[...]
    lane = lax.broadcasted_iota(jnp.int32, gates.shape, 1)
    g = jnp.sum(jnp.where(lane == e, gates, 0.0), axis=-1, keepdims=True)
    o_ref[...] = (a * jax.nn.sigmoid(a) * u * g).astype(o_ref.dtype)


def _moe_up(h, wg, wu, gates):
    m, d = h.shape
    tm = min(1024, m)
    return pl.pallas_call(
        _moe_up_kernel,
        out_shape=jax.ShapeDtypeStruct((m, N_EXPERTS * D_FF), BF16),
        grid=(m // tm, N_EXPERTS),
        in_specs=[pl.BlockSpec((tm, d), lambda i, e: (i, 0)),
                  pl.BlockSpec((None, d, D_FF), lambda i, e: (e, 0, 0)),
                  pl.BlockSpec((None, d, D_FF), lambda i, e: (e, 0, 0)),
                  pl.BlockSpec((tm, LANE), lambda i, e: (i, 0))],
        out_specs=pl.BlockSpec((tm, D_FF), lambda i, e: (i, e)),
        compiler_params=_cp("parallel", "arbitrary"),
        name="moe_up",
    )(h, wg, wu, gates)


MOE_TOK_TILE = 256
MOE_ROW_TILE = 256


def _row_copy(src, src_row, dst, dst_row, sem):
    return pltpu.make_async_copy(src.at[pl.ds(src_row, 1)], dst.at[pl.ds(dst_row, 1)], sem)


def _moe_dispatch_kernel(pos_ref, pad_ref, h_ref, xs_ref, sem):
    tm = h_ref.shape[0]
    n_pad = pad_ref.shape[1]

    def token_rows(start):
        def one(r, carry):
            for k in range(2):
                cp = _row_copy(h_ref, r, xs_ref, pos_ref[0, 2 * r + k], sem)
                cp.start() if start else cp.wait()
            return carry

        lax.fori_loop(0, tm, one, 0, unroll=8)

    def pad_rows(start):
        def one(j, carry):
            cp = _row_copy(h_ref, 0, xs_ref, pad_ref[0, j], sem)
            cp.start() if start else cp.wait()
            return carry

        lax.fori_loop(0, n_pad, one, 0, unroll=8)

    token_rows(True)
    pad_rows(True)
    token_rows(False)
    pad_rows(False)


def _moe_dispatch(h_packed, pos, pad_pos, n_rows):
    m, w = h_packed.shape
    tm = MOE_TOK_TILE
    n_pad = pad_pos.shape[-1]
    return pl.pallas_call(
        _moe_dispatch_kernel,
        out_shape=jax.ShapeDtypeStruct((n_rows, w), jnp.uint32),
        grid=(m // tm,),
        in_specs=[pl.BlockSpec((None, 1, 2 * tm), lambda i: (i, 0, 0), memory_space=pltpu.SMEM),
                  pl.BlockSpec((None, 1, n_pad), lambda i: (i, 0, 0), memory_space=pltpu.SMEM),
                  pl.BlockSpec((tm, w), lambda i: (i, 0))],
        out_specs=pl.BlockSpec(memory_space=pl.ANY),
        scratch_shapes=[pltpu.SemaphoreType.DMA],
        compiler_params=_cp("arbitrary"),
        name="moe_dispatch",
    )(pos, pad_pos, h_packed)


def _moe_experts_kernel(te_ref, xs_ref, wg_ref, wu_ref, wd_ref, ys_ref):
    del te_ref
    half = xs_ref.shape[1]
    x1, x2 = _unpack_bf16_pairs(xs_ref[...])
    x1, x2 = x1.astype(BF16), x2.astype(BF16)
    a = _dot(x1, wg_ref[:half, :]) + _dot(x2, wg_ref[half:, :])
    u = _dot(x1, wu_ref[:half, :]) + _dot(x2, wu_ref[half:, :])
    hid = (a * jax.nn.sigmoid(a) * u).astype(BF16)
    ys_ref[...] = _pack_bf16_pairs(_dot(hid, wd_ref[...]))


def _moe_experts(tile_expert, xs, wg, wu, wd):
    n_rows, w = xs.shape
    d = 2 * w
    t = MOE_ROW_TILE
    return pl.pallas_call(
        _moe_experts_kernel,
        out_shape=jax.ShapeDtypeStruct((n_rows, w), jnp.uint32),
        grid_spec=pltpu.PrefetchScalarGridSpec(
            num_scalar_prefetch=1, grid=(n_rows // t,),
            in_specs=[pl.BlockSpec((t, w), lambda i, te: (i, 0)),
                      pl.BlockSpec((None, d, D_FF), lambda i, te: (te[i], 0, 0)),
                      pl.BlockSpec((None, d, D_FF), lambda i, te: (te[i], 0, 0)),
                      pl.BlockSpec((None, D_FF, d), lambda i, te: (te[i], 0, 0))],
            out_specs=pl.BlockSpec((t, w), lambda i, te: (i, 0))),
        compiler_params=_cp("arbitrary"),
        name="moe_experts",
    )(tile_expert, xs, wg, wu, wd)


def _moe_combine_kernel(pos_ref, posn_ref, x_ref, gate_ref, rec_ref, ys_ref, o_ref, g1_ref, g2_ref, sem):
    _, tm, half = g1_ref.shape
    bufs = (g1_ref, g2_ref)
    i = pl.program_id(0)
    slot = i % 2

    def row_copy(p_ref, s, r, k):
        return _row_copy(ys_ref, p_ref[0, 2 * r + k], bufs[k].at[s], r, sem.at[s])

    def all_rows(p_ref, s, start):
        def one(r, carry):
            for k in range(2):
                cp = row_copy(p_ref, s, r, k)
                cp.start() if start else cp.wait()
            return carry

        lax.fori_loop(0, tm, one, 0, unroll=8)

    @pl.when(i == 0)
    def _():
        all_rows(pos_ref, slot, True)

    all_rows(pos_ref, slot, False)
    rb = 2 * SUBLANE

    def combine(prefetch, b, carry):
        rs = pl.ds(pl.multiple_of(b * rb, rb), rb)
        rec = rec_ref[rs, :]
        w1, w2 = rec[:, INFO_W1:INFO_W1 + 1], rec[:, INFO_W2:INFO_W2 + 1]
        y1 = _unpack_bf16_pairs(g1_ref[slot, rs, :])
        y2 = _unpack_bf16_pairs(g2_ref[slot, rs, :])
        for part in range(2):
            cols = slice(part * half, (part + 1) * half)
            o_ref[rs, cols] = x_ref[rs, cols] + gate_ref[:, cols] * (w1 * y1[part] + w2 * y2[part])
        if prefetch:
            for r in range(rb):
                for k in range(2):
                    row_copy(posn_ref, 1 - slot, b * rb + r, k).start()
        return carry

    has_next = i + 1 < pl.num_programs(0)

    @pl.when(has_next)
    def _():
        lax.fori_loop(0, tm // rb, functools.partial(combine, True), 0)

    @pl.when(jnp.logical_not(has_next))
    def _():
        lax.fori_loop(0, tm // rb, functools.partial(combine, False), 0)


def _moe_combine(pos, xres, gate, rec, ys):
    m, d = xres.shape
    tm = MOE_TOK_TILE
    last = m // tm - 1
    return pl.pallas_call(
        _moe_combine_kernel,
        out_shape=jax.ShapeDtypeStruct((m, d), F32),
        grid=(m // tm,),
        in_specs=[pl.BlockSpec((None, 1, 2 * tm), lambda i: (i, 0, 0), memory_space=pltpu.SMEM),
                  pl.BlockSpec((None, 1, 2 * tm), lambda i: (jnp.minimum(i + 1, last), 0, 0),
                               memory_space=pltpu.SMEM),
                  pl.BlockSpec((tm, d), lambda i: (i, 0)),
                  pl.BlockSpec((1, d), lambda i: (0, 0)),
                  pl.BlockSpec((tm, LANE), lambda i: (i, 0)),
                  pl.BlockSpec(memory_space=pl.ANY)],
        out_specs=pl.BlockSpec((tm, d), lambda i: (i, 0)),
        scratch_shapes=[pltpu.VMEM((2, tm, d // 2), jnp.uint32), pltpu.VMEM((2, tm, d // 2), jnp.uint32),
                        pltpu.SemaphoreType.DMA((2,))],
        compiler_params=_cp("arbitrary"),
        name="moe_combine",
    )(pos, pos, xres, gate, rec, ys)


def _moe_routed(xres, gain, shift, scale, gate, router, wg, wu, wd3):
    m = xres.shape[0]
    tm, t = MOE_TOK_TILE, MOE_ROW_TILE
    h_packed, rec, counts = _modulate(xres, gain, shift, scale, router=router, route="routed")
    n_rows = 2 * m + N_EXPERTS * t
    cnt = counts[0, :N_EXPERTS].astype(jnp.int32)
    seg = ((cnt + t - 1) // t) * t
    seg_end = jnp.cumsum(seg)
    seg_start = seg_end - seg
    e12 = rec[:, INFO_E1:INFO_E2 + 1].astype(jnp.int32)
    r12 = rec[:, INFO_R1:INFO_R2 + 1].astype(jnp.int32)
    pos = (seg_start[e12] + r12).reshape(m // tm, 1, 2 * tm)
    tile_start = jnp.arange(n_rows // t, dtype=jnp.int32) * t
    tile_expert = jnp.minimum(jnp.sum(tile_start[:, None] >= seg_end[None, :], axis=1), N_EXPERTS - 1).astype(jnp.int32)
    n_steps = m // tm
    assert (n_rows - 2 * m) % n_steps == 0
    pad_cnt = jnp.concatenate([seg - cnt, n_rows - seg_end[-1:]])
    pad_end = jnp.cumsum(pad_cnt)
    pad_first = jnp.concatenate([seg_start + cnt, seg_end[-1:]])
    j = jnp.arange(n_rows - 2 * m, dtype=jnp.int32)
    rng = jnp.sum(j[:, None] >= pad_end[None, :], axis=1)
    pad_pos = (pad_first[rng] + (j - (pad_end - pad_cnt)[rng])).astype(jnp.int32).reshape(n_steps, 1, -1)
    xs = _moe_dispatch(h_packed, pos, pad_pos, n_rows)
    ys = _moe_experts(tile_expert, xs, wg, wu, wd3)
    return _moe_combine(pos, xres, gate, rec, ys)


GLA_LEVELS = 6
GLA_MILD_DECAY = 50.0


def _gla_tables(reverse):
    n = GLA_CHUNK
    t = np.arange(n)[:, None]
    r = np.arange(n)[None, :]
    blocks = [r <= t, r > t]
    masks = []
    for lvl in range(GLA_LEVELS):
        bs = 1 << lvl
        blocks.append((r >= (t & ~(bs - 1))) & (r <= t))
        blocks.append((r > t) & (r <= (t | (bs - 1))))
        masks.append(((t >> (lvl + 1)) == (r >> (lvl + 1))) & (((t >> lvl) & 1) == 1) & (((r >> lvl) & 1) == 0))
    masks.append(t == r)
    masks.append(r <= t)
    e = np.stack(blocks).astype(np.float32)
    msk = np.stack(masks).astype(np.float32)
    if reverse:
        e, msk = e[:, ::-1, ::-1], msk[:, ::-1, ::-1]
    return jnp.asarray(e.reshape(-1, n)), jnp.asarray(msk)


def _gla_intra_kernel(k_ref, v_ref, q_ref, a_ref, wa_ref, ba_ref, e_ref, m_ref,
                      oi_ref, qd_ref, kd_ref, dl_ref, la_ref, tot_ref, *, nchunk, reverse):
    n = GLA_CHUNK
    scale = GLA_DK ** -0.5

    a = a_ref[...]
    z = _dot(a, wa_ref[0]) + _dot(a, wa_ref[1]) + _dot(a, wa_ref[2]) + ba_ref[...]
    la = (jnp.minimum(z, 0.0) - jnp.log1p(jnp.exp(-jnp.abs(z)))) * (1.0 / GLA_TEMP)
    la_ref[...] = la
    totals = jnp.sum(la.reshape(nchunk, n, GLA_DK), axis=1)
    tot_ref[...] = totals
    dl_ref[...] = jnp.exp(totals)
    mild = jnp.min(totals) > -GLA_MILD_DECAY
    rowid = lax.broadcasted_iota(jnp.int32, (n, GLA_DK), 0)

    def load(c):
        rows = pl.ds(pl.multiple_of(c * n, n), n)
        return rows, q_ref[rows, :].astype(F32) * scale, k_ref[rows, :].astype(F32)

    def mild_chunk(c, carry):
        rows, q, k = load(c)
        la_c = la_ref[rows, :]
        total = tot_ref[pl.ds(c, 1), :]
        cum = la_c
        shift = 1
        while shift < n:
            cum = cum + jnp.where(rowid >= shift, pltpu.roll(cum, shift, axis=0), 0.0)
            shift *= 2
        if reverse:
            cum = total - cum + la_c
        qd = (q * jnp.exp(cum)).astype(BF16)
        kinv = k * jnp.exp(-cum)
        qd_ref[rows, :] = qd
        kd_ref[rows, :] = (kinv * jnp.exp(total)).astype(BF16)
        att = m_ref[GLA_LEVELS + 1] * _dot_nt(qd, kinv.astype(BF16))
        oi_ref[rows, :] = _dot(att.astype(BF16), v_ref[rows, :]).astype(BF16)
        return carry

    def harsh_chunk(c, carry):
        rows, q, k = load(c)
        ex = _dot_f32(e_ref[...], la_ref[rows, :])
        qd_ref[rows, :] = (q * jnp.exp(ex[0:n])).astype(BF16)
        kd_ref[rows, :] = (k * jnp.exp(ex[n:2 * n])).astype(BF16)
        att = m_ref[GLA_LEVELS] * _dot_nt(q.astype(BF16), k.astype(BF16))
        for lvl in range(GLA_LEVELS):
            qs = (q * jnp.exp(ex[(2 + 2 * lvl) * n:(3 + 2 * lvl) * n])).astype(BF16)
            ks = (k * jnp.exp(ex[(3 + 2 * lvl) * n:(4 + 2 * lvl) * n])).astype(BF16)
            att += m_ref[lvl] * _dot_nt(qs, ks)
        oi_ref[rows, :] = _dot(att.astype(BF16), v_ref[rows, :]).astype(BF16)
        return carry

    @pl.when(mild)
    def _():
        lax.fori_loop(0, nchunk, mild_chunk, 0, unroll=True)

    @pl.when(jnp.logical_not(mild))
    def _():
        lax.fori_loop(0, nchunk, harsh_chunk, 0)


def _gla_blocks(m):
    tb = min(512, m)
    return tb, tb // GLA_CHUNK


def _gla_intra(px, cols, wa_pad, ba, reverse):
    m = px.shape[0]
    tb, nchunk = _gla_blocks(m)
    kb, vb, qb, ab = cols
    e, msk = _gla_tables(reverse)
    return pl.pallas_call(
        functools.partial(_gla_intra_kernel, nchunk=nchunk, reverse=reverse),
        out_shape=[jax.ShapeDtypeStruct((m, GLA_VAL), BF16), jax.ShapeDtypeStruct((m, GLA_KEY), BF16),
                   jax.ShapeDtypeStruct((m, GLA_KEY), BF16), jax.ShapeDtypeStruct((m // GLA_CHUNK, GLA_KEY), F32)],
        grid=(m // tb, GLA_HEADS),
        in_specs=[pl.BlockSpec((tb, GLA_DK), lambda i, h: (i, kb + h)),
                  pl.BlockSpec((tb, GLA_DV), lambda i, h: (i, vb + h)),
                  pl.BlockSpec((tb, GLA_DK), lambda i, h: (i, qb + h)),
                  pl.BlockSpec((tb, LANE), lambda i, h: (i, ab)),
                  pl.BlockSpec((3, LANE, GLA_DK), lambda i, h: (0, 0, h)),
                  pl.BlockSpec((1, GLA_DK), lambda i, h: (0, h)),
                  pl.BlockSpec(e.shape, lambda i, h: (0, 0)),
                  pl.BlockSpec(msk.shape, lambda i, h: (0, 0, 0))],
        out_specs=[pl.BlockSpec((tb, GLA_DV), lambda i, h: (i, h)),
                   pl.BlockSpec((tb, GLA_DK), lambda i, h: (i, h)),
                   pl.BlockSpec((tb, GLA_DK), lambda i, h: (i, h)),
                   pl.BlockSpec((nchunk, GLA_DK), lambda i, h: (i, h))],
        scratch_shapes=[pltpu.VMEM((tb, GLA_DK), F32), pltpu.VMEM((nchunk, GLA_DK), F32)],
        compiler_params=_cp("parallel", "parallel"),
        name="gla_intra_bwd" if reverse else "gla_intra_fwd",
    )(px, px, px, px, wa_pad, ba, e, msk)


def _gla_scan_kernel(*refs, nchunk):
    i = pl.program_id(1)
    n = GLA_CHUNK
    ins, outs, states = refs[:12], refs[12:16], refs[16:]
    dirs = [(ins[6 * d:6 * d + 6], outs[2 * d:2 * d + 2], states[d], bool(d)) for d in range(2)]

    @pl.when(i == 0)
    def _():
        for (_, _, _, _, _, s0_ref), _, s_ref, _ in dirs:
            s_ref[...] = s0_ref[...]

    def chunk(cc, carry):
        for (qd_ref, kd_ref, v_ref, dl_ref, oi_ref, _), (o_ref, _), s_ref, reverse in dirs:
            c = (nchunk - 1 - cc) if reverse else cc
            rows = pl.ds(pl.multiple_of(c * n, n), n)
            st = s_ref[...]
            o_ref[rows, :] = (oi_ref[rows, :].astype(F32) + _dot_nt(qd_ref[rows, :], st.astype(BF16))).astype(BF16)
            s_ref[...] = st * dl_ref[pl.ds(c, 1), :] + _dot_tn(v_ref[rows, :], kd_ref[rows, :])
        return carry

    lax.fori_loop(0, nchunk, chunk, 0, unroll=True)

    @pl.when(i == pl.num_programs(1) - 1)
    def _():
        for _, (_, sf_ref), s_ref, _ in dirs:
            sf_ref[...] = s_ref[...]


def _gla_scan(px, vb, fwd, bwd):
    m = px.shape[0]
    tb, nchunk = _gla_blocks(m)
    nb = m // tb

    def specs(blk):
        return [pl.BlockSpec((tb, GLA_DK), lambda h, i: (blk(i), h)),
                pl.BlockSpec((tb, GLA_DK), lambda h, i: (blk(i), h)),
                pl.BlockSpec((tb, GLA_DV), lambda h, i: (blk(i), vb + h)),
                pl.BlockSpec((nchunk, GLA_DK), lambda h, i: (blk(i), h)),
                pl.BlockSpec((tb, GLA_DV), lambda h, i: (blk(i), h)),
                pl.BlockSpec((None, GLA_DV, GLA_DK), lambda h, i: (h, 0, 0))]

    def out_specs(blk):
        return [pl.BlockSpec((tb, GLA_DV), lambda h, i: (blk(i), h)),
                pl.BlockSpec((None, GLA_DV, GLA_DK), lambda h, i: (h, 0, 0))]

    ahead, back = (lambda i: i), (lambda i: nb - 1 - i)
    out_shape = [jax.ShapeDtypeStruct((m, GLA_VAL), BF16), jax.ShapeDtypeStruct((GLA_HEADS, GLA_DV, GLA_DK), F32)]
    args = []
    for qd, kd, dl, oi, s0 in (fwd, bwd):
        args += [qd, kd, px, dl, oi, s0]
    return pl.pallas_call(
        functools.partial(_gla_scan_kernel, nchunk=nchunk),
        out_shape=out_shape + out_shape,
        grid=(GLA_HEADS, nb),
        in_specs=specs(ahead) + specs(back),
        out_specs=out_specs(ahead) + out_specs(back),
        scratch_shapes=[pltpu.VMEM((GLA_DV, GLA_DK), F32), pltpu.VMEM((GLA_DV, GLA_DK), F32)],
        compiler_params=_cp("parallel", "arbitrary"),
        name="gla_scan",
    )(*args)


def _gla_gate_kernel(of_ref, ob_ref, g_ref, gain_ref, o_ref):
    gain = gain_ref[...]
    for h in range(GLA_HEADS):
        sl = slice(h * GLA_DV, (h + 1) * GLA_DV)
        o = _rms(of_ref[:, sl].astype(F32) + ob_ref[:, sl].astype(F32)) * gain
        g = g_ref[:, sl].astype(F32)
        o_ref[:, sl] = (o * (g * jax.nn.sigmoid(g))).astype(BF16)


def _gla_gate(o_f, o_b, px, g_blk, gain):
    m = o_f.shape[0]
    tm = min(256, m)
    return pl.pallas_call(
        _gla_gate_kernel,
        out_shape=jax.ShapeDtypeStruct((m, GLA_VAL), BF16),
        grid=(m // tm,),
        in_specs=[pl.BlockSpec((tm, GLA_VAL), lambda i: (i, 0)), pl.BlockSpec((tm, GLA_VAL), lambda i: (i, 0)),
                  pl.BlockSpec((tm, GLA_VAL), lambda i: (i, g_blk)), pl.BlockSpec((1, GLA_DV), lambda i: (0, 0))],
        out_specs=pl.BlockSpec((tm, GLA_VAL), lambda i: (i, 0)),
        compiler_params=_cp("parallel"),
        name="gla_gate",
    )(o_f, o_b, px, gain)


def _pad_cols(w, mult):
    pad = (-w.shape[1]) % mult
    return jnp.pad(w, ((0, 0), (0, pad))) if pad else w


def _row(v, width=None):
    v = v.reshape(1, -1).astype(F32)
    return _pad_cols(v, width) if width else v


MM_TN = 512

AB_CKV_BLK = MLA_Q_RANK // MLA_KV_RANK
AB_CONV_COL = MLA_Q_RANK + MLA_KV_RANK
AB_KR_BLK = (AB_CONV_COL + 3 * CONV_WIDTH) // LANE
GLA_K_BLK = 0
GLA_V_BLK = GLA_KEY // GLA_DV
GLA_Q_BLK = (GLA_KEY + GLA_VAL) // GLA_DK
GLA_G_BLK = (2 * GLA_KEY + GLA_VAL) // GLA_VAL
GLA_A_BLK = (2 * GLA_KEY + 2 * GLA_VAL) // LANE


def _ab_w_in(w):
    ckv = w[:, :MLA_KV_RANK]
    kr = w[:, MLA_KV_RANK:MLA_KV_RANK + MLA_ROPE]
    cq = w[:, MLA_KV_RANK + MLA_ROPE:MLA_KV_RANK + MLA_ROPE + MLA_Q_RANK]
    conv = w[:, MLA_KV_RANK + MLA_ROPE + MLA_Q_RANK:]
    return _pad_cols(jnp.concatenate([cq, ckv, conv, kr], axis=1), MM_TN).astype(BF16)


def _ab_w_uq(w):
    w = w.reshape(MLA_Q_RANK, MLA_HEADS, MLA_QK)
    w = jnp.pad(w, ((0, 0), (0, 0), (0, MLA_HEAD_PAD - MLA_QK)))
    return w.reshape(MLA_Q_RANK, MLA_HEADS * MLA_HEAD_PAD).astype(BF16)


def _gla_w_in(w):
    o1, o2 = GLA_KEY, GLA_KEY + GLA_VAL
    o3 = o2 + 2 * GLA_RANK
    k, v, a, q, g = w[:, :o1], w[:, o1:o2], w[:, o2:o3], w[:, o3:o3 + GLA_KEY], w[:, o3 + GLA_KEY:]
    return _pad_cols(jnp.concatenate([k, v, q, g, a], axis=1), MM_TN).astype(BF16)


def _gla_wa_pad(w_a, offset):
    w = jnp.zeros((LANE, GLA_KEY), F32).at[offset:offset + GLA_RANK].set(w_a.astype(F32))
    hi = w.astype(BF16)
    r1 = w - hi.astype(F32)
    mid = r1.astype(BF16)
    lo = (r1 - mid.astype(F32)).astype(BF16)
    return jnp.stack([hi, mid, lo])


def _rope_tables(pos_rows, pos_cols):
    half = MLA_ROPE // 4
    inv_freq = ROPE_BASE ** (-jnp.arange(half, dtype=F32) / half)
    ar = pos_rows.astype(F32)[:, None] * inv_freq[None, :]
    ac = pos_cols.astype(F32)[:, None] * inv_freq[None, :]
    zeros = jnp.zeros((ar.shape[0], LANE - MLA_ROPE), F32)
    cos = jnp.concatenate([jnp.cos(ar), jnp.cos(ar), jnp.cos(ac), jnp.cos(ac), zeros], axis=1)
    sin = jnp.concatenate([-jnp.sin(ar), jnp.sin(ar), -jnp.sin(ac), jnp.sin(ac), zeros], axis=1)
    return cos, sin


def _mixer_mla_conv(hx, hc, p, rope_x, rope_c, need_ctx):
    w_in, w_uq, w_ukv, w_out = p["w_in"], p["w_uq"], p["w_ukv"], p["w_out"]

    def project(h, rope):
        px = _mm([(h[0], w_in)], out_dtype=BF16, modulation=h[1], name="ab_in_proj")
        k, v = _kv_up(px, AB_CKV_BLK, AB_KR_BLK, p["kv_a_gain"], w_ukv, p["k_gain_n"], p["k_gain_r"], *rope)
        return px, k, v

    def branch_out(px, k, v, rope):
        q = _q_up(px, p["q_a_gain"], w_uq, p["q_gain_n"], p["q_gain_r"], *rope, p["k_gain_n"], p["k_gain_r"])
        o = _attention(q, k, v)
        y = _conv(px, AB_CONV_COL, p["conv_w"], p["conv_b"])
        return o, y

    px_c, k_c, v_c = project(hc, rope_c)
    px_x, k_x, v_x = project(hx, rope_x)
    k_all = jnp.concatenate([k_c, k_x], axis=0)
    v_all = jnp.concatenate([v_c, v_x], axis=0)
    out_x = branch_out(px_x, k_all, v_all, rope_x)
    out_c = branch_out(px_c, k_c, v_c, rope_c) if need_ctx else None
    half = MLA_HEADS * MLA_V
    return out_x, out_c, (w_out[:half], w_out[half:])


def _mixer_gla(hx, hc, p, need_ctx):
    cols = (GLA_K_BLK, GLA_V_BLK, GLA_Q_BLK, GLA_A_BLK)
    zero_state = jnp.zeros((GLA_HEADS, GLA_DV, GLA_DK), F32)

    def run(h, s_f, s_b):
        px = _mm([(h[0], p["w_in"])], out_dtype=BF16, modulation=h[1], name="gla_in_proj")
        sides = []
        for reverse, wa, ba, s0 in ((False, p["wa_f"], p["ba_f"], s_f), (True, p["wa_b"], p["ba_b"], s_b)):
            oi, qd, kd, dl = _gla_intra(px, cols, wa, ba, reverse)
            sides.append((qd, kd, dl, oi, s0))
        o_f, sf_fin, o_b, sb_fin = _gla_scan(px, GLA_V_BLK, *sides)
        return px, (o_f, o_b), (sf_fin, sb_fin)

    px_c, o_c, (s_f, s_b) = run(hc, zero_state, zero_state)
    px_x, o_x, _ = run(hx, s_f, s_b)
    out_x = _gla_gate(o_x[0], o_x[1], px_x, GLA_G_BLK, p["o_gain"])
    out_c = _gla_gate(o_c[0], o_c[1], px_c, GLA_G_BLK, p["o_gain"]) if need_ctx else None
    return out_x, out_c


def _moe_dense(xres, gain, shift, scale, gate, router, wg, wu, wd3):
    h, gates = _modulate(xres, gain, shift, scale, router=router, route="dense")
    hid = _moe_up(h, wg, wu, gates)
    wd = wd3.reshape(N_EXPERTS * D_FF, wd3.shape[-1])
    return _mm([(hid, wd)], out_dtype=F32, residual=(xres, gate), name="moe_down")


def _moe(xres, *args):
    routed = xres.shape[0] >= 4 * MOE_ROW_TILE and xres.shape[0] % MOE_TOK_TILE == 0
    return (_moe_routed if routed else _moe_dense)(xres, *args)


def kernel(x, c, ctx, c_ctx, ada_w, ada_b, norm_mix_gain, norm_ffn_gain, ab_w_in, ab_q_a_gain, ab_w_uq, ab_q_gain, ab_kv_a_gain, ab_w_ukv, ab_k_gain, ab_conv_w, ab_conv_b, ab_w_out, gla_w_in, gla_w_af, gla_b_af, gla_w_ab, gla_b_ab, gla_o_gain, gla_w_out, router_w, router_b, moe_w_gate, moe_w_up, moe_w_down):
    b, s, d = x.shape
    assert b == 1 and s % GRID_W == 0
    depth = ada_w.shape[0]
    xs, cx = x[0], ctx[0]
    n_ctx = cx.shape[0]

    mod = _adaln(c.reshape(d, 1), c_ctx.reshape(d, 1), ada_w, ada_b).reshape(depth, 2, 6, 1, d)
    tok = jnp.arange(s, dtype=jnp.int32)
    rope_x = _rope_tables(tok // GRID_W, tok % GRID_W)
    rope_c = (jnp.concatenate([jnp.ones((n_ctx, MLA_ROPE), F32), jnp.zeros((n_ctx, LANE - MLA_ROPE), F32)], axis=1),
              jnp.zeros((n_ctx, LANE), F32))
    wr = _pad_cols(router_w.astype(F32), LANE)
    wr_hi = wr.astype(BF16)
    wr_lo = (wr - wr_hi.astype(F32)).astype(BF16)
    router = (jnp.stack([wr_hi, wr_lo]), router_b.reshape(N_EXPERTS, 1).astype(F32))

    for i in range(depth):
        last = i == depth - 1
        j = i // 2
        mx, mc = mod[i, 0], mod[i, 1]
        gain_mix = _row(norm_mix_gain[i])
        hx = (xs, (gain_mix, mx[0], mx[1]))
        hc = (cx, (gain_mix, mc[0], mc[1]))
        if i % 2 == 0:
            p = dict(
                w_in=_ab_w_in(ab_w_in[j]), w_uq=_ab_w_uq(ab_w_uq[j]), w_ukv=ab_w_ukv[j].astype(BF16),
                w_out=ab_w_out[j].astype(BF16),
                q_a_gain=_row(ab_q_a_gain[j]), kv_a_gain=_row(ab_kv_a_gain[j]),
                q_gain_n=_row(ab_q_gain[j][:MLA_NOPE]), q_gain_r=_row(ab_q_gain[j][MLA_NOPE:], LANE),
                k_gain_n=_row(ab_k_gain[j][:MLA_NOPE]), k_gain_r=_row(ab_k_gain[j][MLA_NOPE:], LANE),
                conv_w=ab_conv_w[j].astype(F32), conv_b=_row(ab_conv_b[j]))
            (o_x, y_x), out_c, (w_o, w_y) = _mixer_mla_conv(hx, hc, p, rope_x, rope_c, not last)
            xs = _mm([(o_x, w_o), (y_x, w_y)], out_dtype=F32, residual=(xs, mx[2]), name="ab_out_proj")
            if not last:
                cx = _mm([(out_c[0], w_o), (out_c[1], w_y)], out_dtype=F32, residual=(cx, mc[2]), name="ab_out_proj")
        else:
            p = dict(
                w_in=_gla_w_in(gla_w_in[j]),
                wa_f=_gla_wa_pad(gla_w_af[j], 0), wa_b=_gla_wa_pad(gla_w_ab[j], GLA_RANK),
                ba_f=_row(gla_b_af[j]), ba_b=_row(gla_b_ab[j]), o_gain=_row(gla_o_gain[j]))
            w_out = gla_w_out[j].astype(BF16)
            y_x, y_c = _mixer_gla(hx, hc, p, not last)
            xs = _mm([(y_x, w_out)], out_dtype=F32, residual=(xs, mx[2]), name="gla_out_proj")
            if not last:
                cx = _mm([(y_c, w_out)], out_dtype=F32, residual=(cx, mc[2]), name="gla_out_proj")
        gain_ffn = _row(norm_ffn_gain[i])
        wg, wu = moe_w_gate[i].astype(BF16), moe_w_up[i].astype(BF16)
        wd = moe_w_down[i].astype(BF16)
        xs = _moe(xs, gain_ffn, mx[3], mx[4], mx[5], router, wg, wu, wd)
        if not last:
            cx = _moe(cx, gain_ffn, mc[3], mc[4], mc[5], router, wg, wu, wd)
    return xs[None]
```
